```python
import math
import jax, jax.numpy as jnp
from jax import lax
import numpy as np

D_MODEL = 1024
BATCH = 4
SEQ = 8192
DEPTH = 4

GRID_W = 64
CTX_LEN = 256
EPS = 1e-6
ADA_SCALE = 0.5
HEAD_DIM = 128
BLOCK = 128
A_Q_HEADS = 8
A_KV_HEADS = 2
A_REP = A_Q_HEADS // A_KV_HEADS
WINDOW = 128
B_Q_HEADS = 8
B_KV_HEADS = 2
B_REP = B_Q_HEADS // B_KV_HEADS
ROPE_THETA = 10000.0
A_Q = A_Q_HEADS * HEAD_DIM
A_KV = A_KV_HEADS * HEAD_DIM
B_Q = B_Q_HEADS * HEAD_DIM
B_KV = B_KV_HEADS * HEAD_DIM
ATTN_SPLITS = [A_Q, A_KV, A_KV, A_Q, B_Q, B_KV, B_KV, B_Q]
ATTN_IN = sum(ATTN_SPLITS)
ATTN_WIDTH = A_Q + B_Q
D_INNER = 2 * D_MODEL
SSM_HEAD_DIM = 64
SSM_HEADS = D_INNER // SSM_HEAD_DIM
SSM_GROUPS = 8
D_STATE = 128
D_CONV = 3
SSM_CHUNK = 128
CONV_DIM = D_INNER + 2 * SSM_GROUPS * D_STATE
SSM_SPLITS = [D_INNER, CONV_DIM, 2 * SSM_HEADS]
SSM_IN = sum(SSM_SPLITS)

kernel_name = "hybrid_ctx_prefix_attn_ssd_backbone"


def split_cols(t, sizes):
    idx = np.cumsum(sizes)[:-1].tolist()
    return jnp.split(t, idx, axis=-1)


def rms_norm(x, w):
    xf = x.astype(jnp.float32)
    y = xf * lax.rsqrt(jnp.mean(xf * xf, axis=-1, keepdims=True) + EPS)
    return (y * w.astype(jnp.float32)).astype(x.dtype)


def axial_rope_tables(n_rows):
    row = jnp.repeat(jnp.arange(n_rows), GRID_W).astype(jnp.float32)
    col = jnp.tile(jnp.arange(GRID_W), n_rows).astype(jnp.float32)
    n_freq = HEAD_DIM // 4
    inv = 1.0 / (ROPE_THETA ** (jnp.arange(n_freq, dtype=jnp.float32) / n_freq))
    ang = jnp.concatenate([row[:, None] * inv, col[:, None] * inv], axis=-1)
    return jnp.cos(ang), jnp.sin(ang)


def apply_rope(x, cos, sin):
    xp = x.astype(jnp.float32).reshape(*x.shape[:-1], HEAD_DIM // 2, 2)
    x0, x1 = xp[..., 0], xp[..., 1]
    c = cos[None, :, None, :]
    s = sin[None, :, None, :]
    out = jnp.stack([x0 * c - x1 * s, x0 * s + x1 * c], axis=-1)
    return out.reshape(x.shape).astype(x.dtype)


def to_heads(t, n):
    return t.reshape(*t.shape[:-1], n, HEAD_DIM)


def group_q(q, n_kv):
    return q.reshape(q.shape[0], q.shape[1], n_kv, q.shape[2] // n_kv, HEAD_DIM)


def gqa_softmax(q, segments, sink):
    scale = HEAD_DIM ** -0.5
    logits = []
    for k, v, mask in segments:
        s = jnp.einsum('bqgrd,bkgd->bgrqk', q, k).astype(jnp.float32) * scale
        if mask is not None:
            s = jnp.where(mask, s, -jnp.inf)
        logits.append(s)
    if sink is not None:
        sk = sink.astype(jnp.float32)[None, :, :, None, None]
        logits.append(jnp.broadcast_to(sk, logits[-1].shape[:-1] + (1,)))
    p = jax.nn.softmax(jnp.concatenate(logits, axis=-1), axis=-1)
    out = None
    off = 0
    for k, v, _ in segments:
        n = k.shape[1]
        o = jnp.einsum('bgrqk,bkgd->bqgrd', p[..., off:off + n].astype(v.dtype), v)
        out = o if out is None else out + o
        off += n
    return out


def windowed_attn(q, k, v, kc, vc, sink):
    b, T = q.shape[:2]
    nb = T // BLOCK
    pad = ((0, 0), (BLOCK, BLOCK), (0, 0), (0, 0))
    kp = jnp.pad(k, pad)
    vp = jnp.pad(v, pad)
    qoff = jnp.arange(BLOCK)
    koff = jnp.arange(3 * BLOCK) - BLOCK

    def block(i):
        s = i * BLOCK
        qi = lax.dynamic_slice_in_dim(q, s, BLOCK, axis=1)
        ki = lax.dynamic_slice_in_dim(kp, s, 3 * BLOCK, axis=1)
        vi = lax.dynamic_slice_in_dim(vp, s, 3 * BLOCK, axis=1)
        qpos = s + qoff
        kpos = s + koff
        mask = (jnp.abs(qpos[:, None] - kpos[None, :]) <= WINDOW) & (kpos >= 0)[None, :] & (kpos < T)[None, :]
        return gqa_softmax(qi, [(ki, vi, mask), (kc, vc, None)], sink)

    out = lax.map(block, jnp.arange(nb))
    return jnp.moveaxis(out, 0, 1).reshape(b, T, -1)


def dense_attn(q, k, v, kc, vc):
    b, T = q.shape[:2]
    nb = T // BLOCK

    def block(i):
        qi = lax.dynamic_slice_in_dim(q, i * BLOCK, BLOCK, axis=1)
        return gqa_softmax(qi, [(k, v, None), (kc, vc, None)], None)

    out = lax.map(block, jnp.arange(nb))
    return jnp.moveaxis(out, 0, 1).reshape(b, T, -1)


def attn_mixer(h, hc, w_in, w_out, sink, q_norm, k_norm, cos, sin, need_ctx):
    qa, ka, va, ga, qb, kb, vb, gb = split_cols(h @ w_in, ATTN_SPLITS)
    qac, kac, vac, gac, qbc, kbc, vbc, gbc = split_cols(hc @ w_in, ATTN_SPLITS)
    sink_g = sink.reshape(A_KV_HEADS, A_REP)
    qa = apply_rope(to_heads(qa, A_Q_HEADS), cos, sin)
    ka = apply_rope(to_heads(ka, A_KV_HEADS), cos, sin)
    va = to_heads(va, A_KV_HEADS)
    kac = to_heads(kac, A_KV_HEADS)
    vac = to_heads(vac, A_KV_HEADS)
    ya = windowed_attn(group_q(qa, A_KV_HEADS), ka, va, kac, vac, sink_g)
    qb = apply_rope(rms_norm(to_heads(qb, B_Q_HEADS), q_norm), cos, sin)
    kb = apply_rope(rms_norm(to_heads(kb, B_KV_HEADS), k_norm), cos, sin)
    vb = to_heads(vb, B_KV_HEADS)
    kbc = rms_norm(to_heads(kbc, B_KV_HEADS), k_norm)
    vbc = to_heads(vbc, B_KV_HEADS)
    yb = dense_attn(group_q(qb, B_KV_HEADS), kb, vb, kbc, vbc)
    y = jnp.concatenate([ya * jax.nn.silu(ga), yb * jax.nn.silu(gb)], axis=-1) @ w_out
    if not need_ctx:
        return y, None
    b, C = hc.shape[:2]
    yac = gqa_softmax(group_q(to_heads(qac, A_Q_HEADS), A_KV_HEADS), [(kac, vac, None)], sink_g).reshape(b, C, A_Q)
    qbc = rms_norm(to_heads(qbc, B_Q_HEADS), q_norm)
    ybc = gqa_softmax(group_q(qbc, B_KV_HEADS), [(kbc, vbc, None)], None).reshape(b, C, B_Q)
    yc = jnp.concatenate([yac * jax.nn.silu(gac), ybc * jax.nn.silu(gbc)], axis=-1) @ w_out
    return y, yc


def dwconv_centered(u, w, bias):
    L = u.shape[1]
    p = D_CONV // 2
    up = jnp.pad(u, ((0, 0), (p, p), (0, 0)))
    out = bias
    for j in range(D_CONV):
        out = out + w[j] * up[:, j:j + L]
    return out


def ssd_chunked(x, dt, A, Bm, Cm, h0):
    b, L, H, P = x.shape
    G, N = Bm.shape[2], Bm.shape[3]
    R = H // G
    Q = SSM_CHUNK
    nc = L // Q
    f32 = jnp.float32
    xdt = x.astype(f32) * dt[..., None]
    a = dt * A
    tri = jnp.tril(jnp.ones((Q, Q), dtype=bool))

    def chunks(t):
        return jnp.moveaxis(t.reshape(b, nc, Q, *t.shape[2:]), 1, 0)

    def step(h, inp):
        xc, ac, Bc, Cc = inp
        acum = jnp.cumsum(ac, axis=1)
        seg = acum[:, :, None, :] - acum[:, None, :, :]
        Lm = jnp.exp(jnp.where(tri[None, :, :, None], seg, -jnp.inf)).reshape(b, Q, Q, G, R)
        CB = jnp.einsum('blgn,bsgn->bgls', Cc, Bc)
        xg = xc.reshape(b, Q, G, R, P)
        y_diag = jnp.einsum('bgls,blsgr,bsgrp->blgrp', CB, Lm, xg)
        hg = h.reshape(b, G, R, P, N)
        y_off = jnp.einsum('blgn,bgrpn->blgrp', Cc, hg) * jnp.exp(acum).reshape(b, Q, G, R)[..., None]
        w_end = jnp.exp(acum[:, -1:, :] - acum).reshape(b, Q, G, R)
        h_new = jnp.exp(acum[:, -1])[:, :, None, None] * h + jnp.einsum('bsgn,bsgr,bsgrp->bgrpn', Bc, w_end, xg).reshape(b, H, P, N)
        return h_new, (y_diag + y_off).reshape(b, Q, H, P)

    h_fin, ys = lax.scan(step, h0.astype(f32), (chunks(xdt), chunks(a), chunks(Bm.astype(f32)), chunks(Cm.astype(f32))))
    y = jnp.moveaxis(ys, 0, 1).reshape(b, L, H, P)
    return y.astype(x.dtype), h_fin


def gated_rms_norm(y, z, w):
    g = (y * jax.nn.silu(z)).astype(jnp.float32)
    gr = g.reshape(*g.shape[:-1], SSM_GROUPS, D_INNER // SSM_GROUPS)
    gr = gr * lax.rsqrt(jnp.mean(gr * gr, axis=-1, keepdims=True) + EPS)
    return (gr.reshape(g.shape) * w.astype(jnp.float32)).astype(y.dtype)


def ssm_mixer(h, hc, w_in, conv_w, conv_b, dt_bias, a_log, d_skip, norm_w, w_out, need_ctx):
    def prep(u):
        bsz, L = u.shape[:2]
        z, xbc, dt = split_cols(u @ w_in, SSM_SPLITS)
        xbc = jax.nn.silu(dwconv_centered(xbc, conv_w, conv_b))
        xs, Bm, Cm = split_cols(xbc, [D_INNER, SSM_GROUPS * D_STATE, SSM_GROUPS * D_STATE])
        xs = xs.reshape(bsz, L, SSM_HEADS, SSM_HEAD_DIM)
        Bm = Bm.reshape(bsz, L, SSM_GROUPS, D_STATE)
        Cm = Cm.reshape(bsz, L, SSM_GROUPS, D_STATE)
        dt = jax.nn.softplus(dt.astype(jnp.float32).reshape(bsz, L, 2, SSM_HEADS) + dt_bias.astype(jnp.float32))
        return z, xs, Bm, Cm, dt

    rev = lambda t: t[:, ::-1]
    A = -jnp.exp(a_log.astype(jnp.float32))
    zc, xc, Bc, Cc, dtc = prep(hc)
    z, xl, Bl, Cl, dtl = prep(h)
    h0 = jnp.zeros((h.shape[0], SSM_HEADS, SSM_HEAD_DIM, D_STATE), jnp.float32)
    yc_f, sc_f = ssd_chunked(xc, dtc[:, :, 0], A[0], Bc, Cc, h0)
    yc_b, sc_b = ssd_chunked(rev(xc), rev(dtc[:, :, 1]), A[1], rev(Bc), rev(Cc), h0)
    y_f, _ = ssd_chunked(xl, dtl[:, :, 0], A[0], Bl, Cl, sc_f)
    y_b, _ = ssd_chunked(rev(xl), rev(dtl[:, :, 1]), A[1], rev(Bl), rev(Cl), sc_b)

    def finish(yf, yb_rev, xs, zz):
        y = yf + rev(yb_rev) + d_skip[:, None] * xs
        y = gated_rms_norm(y.reshape(*xs.shape[:2], D_INNER), zz, norm_w)
        return y @ w_out

    out = finish(y_f, y_b, xl, z)
    out_c = finish(yc_f, yc_b, xc, zc) if need_ctx else None
    return out, out_c


def setup_inputs(seed: int = 0) -> dict:
    key = jax.random.key(seed)
    ks = jax.random.split(key, 24)
    na = (DEPTH + 1) // 2
    ns = DEPTH // 2
    nrm = lambda k, shape, scale: jax.random.normal(k, shape, jnp.float32) * scale
    dt0 = jnp.exp(jax.random.uniform(ks[16], (ns, 2, SSM_HEADS), jnp.float32, math.log(1e-3), math.log(1e-1)))
    return {
        "x": nrm(ks[0], (BATCH, SEQ, D_MODEL), 1.0),
        "c": nrm(ks[1], (BATCH, D_MODEL), 1.0),
        "ctx": nrm(ks[2], (BATCH, CTX_LEN, D_MODEL), 1.0),
        "c_ctx": nrm(ks[3], (D_MODEL,), 1.0),
        "w_ada": nrm(ks[4], (DEPTH, D_MODEL, 3 * D_MODEL), ADA_SCALE * D_MODEL ** -0.5),
        "b_ada": nrm(ks[5], (DEPTH, 3 * D_MODEL), 0.01),
        "norm_pre": 1.0 + nrm(ks[6], (DEPTH, D_MODEL), 0.05),
        "norm_post": 1.0 + nrm(ks[7], (DEPTH, D_MODEL), 0.05),
        "attn_w_in": nrm(ks[8], (na, D_MODEL, ATTN_IN), D_MODEL ** -0.5),
        "attn_w_out": nrm(ks[9], (na, ATTN_WIDTH, D_MODEL), ATTN_WIDTH ** -0.5),
        "attn_sink": nrm(ks[10], (na, A_Q_HEADS), 0.5),
        "attn_q_norm": 1.0 + nrm(ks[11], (na, HEAD_DIM), 0.05),
        "attn_k_norm": 1.0 + nrm(ks[12], (na, HEAD_DIM), 0.05),
        "ssm_w_in": nrm(ks[13], (ns, D_MODEL, SSM_IN), D_MODEL ** -0.5),
        "ssm_conv_w": nrm(ks[14], (ns, D_CONV, CONV_DIM), D_CONV ** -0.5),
        "ssm_conv_b": nrm(ks[15], (ns, CONV_DIM), 0.01),
        "ssm_dt_bias": dt0 + jnp.log(-jnp.expm1(-dt0)),
        "ssm_a_log": jnp.log(jax.random.uniform(ks[17], (ns, 2, SSM_HEADS), jnp.float32, 1.0, 16.0)),
        "ssm_d": 1.0 + nrm(ks[18], (ns, SSM_HEADS), 0.1),
        "ssm_norm": 1.0 + nrm(ks[19], (ns, D_INNER), 0.05),
        "ssm_w_out": nrm(ks[20], (ns, D_INNER, D_MODEL), D_INNER ** -0.5),
    }


def reference(x, c, ctx, c_ctx, w_ada, b_ada, norm_pre, norm_post, attn_w_in, attn_w_out, attn_sink,
              attn_q_norm, attn_k_norm, ssm_w_in, ssm_conv_w, ssm_conv_b, ssm_dt_bias, ssm_a_log, ssm_d,
              ssm_norm, ssm_w_out):
    T = x.shape[1]
    ROWS = T // GRID_W
    cos, sin = axial_rope_tables(ROWS)
    silu_c = jax.nn.silu(c)
    silu_cc = jax.nn.silu(c_ctx)
    for l in range(DEPTH):
        need_ctx = l < DEPTH - 1
        sh, sc, gt = jnp.split(silu_c @ w_ada[l] + b_ada[l], 3, axis=-1)
        shc, scc, gtc = jnp.split(silu_cc @ w_ada[l] + b_ada[l], 3, axis=-1)
        h = rms_norm(x, norm_pre[l]) * (1.0 + sc[:, None, :]) + sh[:, None, :]
        hc = rms_norm(ctx, norm_pre[l]) * (1.0 + scc) + shc
        i = l // 2
        if l % 2 == 0:
            y, yc = attn_mixer(h, hc, attn_w_in[i], attn_w_out[i], attn_sink[i], attn_q_norm[i],
                               attn_k_norm[i], cos, sin, need_ctx)
        else:
            y, yc = ssm_mixer(h, hc, ssm_w_in[i], ssm_conv_w[i], ssm_conv_b[i], ssm_dt_bias[i],
                              ssm_a_log[i], ssm_d[i], ssm_norm[i], ssm_w_out[i], need_ctx)
        x = x + gt[:, None, :] * rms_norm(y, norm_post[l])
        if need_ctx:
            ctx = ctx + gtc * rms_norm(yc, norm_post[l])
    return x
```

```python
import functools

import numpy as np
import jax
import jax.numpy as jnp
from jax import lax
from jax.experimental import pallas as pl
from jax.experimental.pallas import tpu as pltpu

F32 = jnp.float32
BF16 = jnp.bfloat16

EPS = 1e-6
GRID_W = 64
ROPE_THETA = 10000.0
HEAD_DIM = 128
A_Q_HEADS = 8
A_KV_HEADS = 2
B_Q_HEADS = 8
B_KV_HEADS = 2
REP = 4
WINDOW = 128
SSM_HEAD_DIM = 64
SSM_HEADS = 32
SSM_GROUPS = 8
SSM_REP = SSM_HEADS // SSM_GROUPS
D_STATE = 128
SSM_CHUNK = 128
HPAD = 128

V7X_VMEM_BYTES = 64 * 1024 * 1024
VMEM_LIMIT = V7X_VMEM_BYTES - 8 * 1024 * 1024

TM = 256
TQ_A = 128
TQ_B = 256
TK_B = 256

QA0, GA0, QB0, GB0, KA0, VA0, KB0, VB0 = 0, 1024, 2048, 3072, 4096, 4352, 4608, 4864
ATTN_COLS = 5120

NEG = -1e30


def _params(sem, vmem=VMEM_LIMIT):
    return pltpu.CompilerParams(dimension_semantics=sem, vmem_limit_bytes=vmem)


def _silu(t):
    return t * (1.0 / (1.0 + jnp.exp(-t)))


def _rms(t, w):
    return t * lax.rsqrt(jnp.mean(t * t, axis=-1, keepdims=True) + EPS) * w


def _dot(a, b):
    return jnp.dot(a, b, preferred_element_type=F32)


def _dot_nt(a, b):
    return lax.dot_general(a, b, (((1,), (1,)), ((), ())), preferred_element_type=F32)


def _dot_hi(a, b):
    return jnp.dot(a, b, preferred_element_type=F32, precision=lax.Precision.HIGHEST)


def _dot_hi_nt(a, b):
    return lax.dot_general(a, b, (((1,), (1,)), ((), ())), preferred_element_type=F32,
                           precision=lax.Precision.HIGHEST)


def _mod_kernel(cc_ref, w_ref, b_ref, o_ref):
    o_ref[...] = _dot_hi(_silu(cc_ref[...]), w_ref[...]) + b_ref[...]


def _modulation(cc, w_ada, b_ada):
    depth, d, d3 = w_ada.shape
    return pl.pallas_call(
        _mod_kernel,
        grid=(depth,),
        in_specs=[
            pl.BlockSpec((8, d), lambda l: (0, 0)),
            pl.BlockSpec((None, d, d3), lambda l: (l, 0, 0)),
            pl.BlockSpec((None, 1, d3), lambda l: (l, 0, 0)),
        ],
        out_specs=pl.BlockSpec((None, 8, d3), lambda l: (l, 0, 0)),
        out_shape=jax.ShapeDtypeStruct((depth, 8, d3), F32),
        compiler_params=_params(("arbitrary",)),
        name="modulation",
    )(cc, w_ada, b_ada.reshape(depth, 1, d3))


def _pre_norm(x, mod, w):
    return _rms(x, w) * (1.0 + mod[1:2]) + mod[0:1]


def _attn_in_kernel(x_ref, mod_ref, np_ref, w_ref, cos_ref, sin_ref, qn_ref, kn_ref, o_ref):
    hb = _pre_norm(x_ref[...], mod_ref[...], np_ref[...]).astype(BF16)
    cos = cos_ref[...]
    sin = sin_ref[...]
    scale = HEAD_DIM ** -0.5

    def rope(t):
        return t * cos + pltpu.roll(t, HEAD_DIM // 2, 1) * sin

    nblk = 512
    for j in range(ATTN_COLS // nblk):
        c0 = j * nblk
        t = _dot(hb, w_ref[:, c0:c0 + nblk])
        for hh in range(nblk // HEAD_DIM):
            col = c0 + hh * HEAD_DIM
            th = t[:, hh * HEAD_DIM:(hh + 1) * HEAD_DIM]
            if col < GA0:
                th = rope(th) * scale
            elif col < QB0 or GB0 <= col < KA0:
                th = _silu(th)
            elif col < GB0:
                th = rope(_rms(th, qn_ref[...])) * scale
            elif col < VA0:
                th = rope(th)
            elif KB0 <= col < VB0:
                th = rope(_rms(th, kn_ref[...]))
            o_ref[:, col:col + HEAD_DIM] = th.astype(BF16)


def _attn_in(xs, modl, norm_pre, w, cos2, sin2, qn, kn):
    b, s, d = xs.shape
    n_t = s // TM
    n_lat = n_t - 1
    row = lambda bi, i: (bi, i, 0)
    return pl.pallas_call(
        _attn_in_kernel,
        grid=(b, n_t),
        in_specs=[
            pl.BlockSpec((None, TM, d), row),
            pl.BlockSpec((None, None, 3, d), lambda bi, i: (bi, i // n_lat, 0, 0)),
            pl.BlockSpec((1, d), lambda bi, i: (0, 0)),
            pl.BlockSpec((d, ATTN_COLS), lambda bi, i: (0, 0)),
            pl.BlockSpec((TM, HEAD_DIM), lambda bi, i: (i, 0)),
            pl.BlockSpec((TM, HEAD_DIM), lambda bi, i: (i, 0)),
            pl.BlockSpec((1, HEAD_DIM), lambda bi, i: (0, 0)),
            pl.BlockSpec((1, HEAD_DIM), lambda bi, i: (0, 0)),
        ],
        out_specs=pl.BlockSpec((None, TM, ATTN_COLS), row),
        out_shape=jax.ShapeDtypeStruct((b, s, ATTN_COLS), BF16),
        compiler_params=_params(("parallel", "parallel")),
        name="attn_in",
    )(xs, modl, norm_pre, w, cos2, sin2, qn, kn)


def _attn_a_kernel(q_ref, kp_ref, kc_ref, kn_ref, vp_ref, vc_ref, vn_ref, kx_ref, vx_ref,
                   g_ref, sink_ref, o_ref, *, n_lat):
    i = pl.program_id(1)
    tq = TQ_A
    rows = REP * tq
    r = lax.broadcasted_iota(jnp.int32, (rows, tq), 0) & (tq - 1)
    c = lax.broadcasted_iota(jnp.int32, (rows, tq), 1)
    is_lat = i < n_lat
    lim_prev = jnp.where(jnp.logical_and(i > 0, is_lat), 0, tq)
    lim_next = jnp.where(jnp.logical_and(i < n_lat - 1, is_lat), 0, tq)
    m_prev = (c - r) >= lim_prev
    m_next = (r - c) >= lim_next
    pen_cur = jnp.where(is_lat, 0.0, NEG)
    for g in range(A_KV_HEADS):
        gs = slice(g * HEAD_DIM, (g + 1) * HEAD_DIM)
        q = jnp.concatenate(
            [q_ref[:, (g * REP + h) * HEAD_DIM:(g * REP + h + 1) * HEAD_DIM] for h in range(REP)], axis=0)
        sp = jnp.where(m_prev, _dot_nt(q, kp_ref[:, gs]), NEG)
        sc = _dot_nt(q, kc_ref[:, gs]) + pen_cur
        sn = jnp.where(m_next, _dot_nt(q, kn_ref[:, gs]), NEG)
        sx = _dot_nt(q, kx_ref[:, gs])
        sk = jnp.concatenate(
            [jnp.broadcast_to(sink_ref[g * REP + h:g * REP + h + 1, 0:1], (tq, 1)) for h in range(REP)], axis=0)
        m = jnp.maximum(jnp.maximum(jnp.max(sp, axis=1, keepdims=True), jnp.max(sc, axis=1, keepdims=True)),
                        jnp.maximum(jnp.max(sn, axis=1, keepdims=True), jnp.max(sx, axis=1, keepdims=True)))
        m = jnp.maximum(m, sk)
        pp = jnp.exp(sp - m)
        pc = jnp.exp(sc - m)
        pn = jnp.exp(sn - m)
        px = jnp.exp(sx - m)
        den = (jnp.sum(pp, axis=1, keepdims=True) + jnp.sum(pc, axis=1, keepdims=True)
               + jnp.sum(pn, axis=1, keepdims=True) + jnp.sum(px, axis=1, keepdims=True) + jnp.exp(sk - m))
        o = (_dot(pp.astype(BF16), vp_ref[:, gs]) + _dot(pc.astype(BF16), vc_ref[:, gs])
             + _dot(pn.astype(BF16), vn_ref[:, gs]) + _dot(px.astype(BF16), vx_ref[:, gs])) * (1.0 / den)
        for h in range(REP):
            cs = slice((g * REP + h) * HEAD_DIM, (g * REP + h + 1) * HEAD_DIM)
            o_ref[:, cs] = (o[h * tq:(h + 1) * tq] * g_ref[:, cs].astype(F32)).astype(BF16)


def _attn_a(qkv, sink_b, n_lat_tok):
    b, s, _ = qkv.shape
    tq = TQ_A
    n_blk = s // tq
    n_lat = n_lat_tok // tq
    kvw = A_KV_HEADS * HEAD_DIM
    ctx_len = s - n_lat_tok
    qw = A_Q_HEADS * HEAD_DIM
    prev = lambda cb: (lambda bi, i: (bi, jnp.maximum(i - 1, 0), cb))
    cur = lambda cb: (lambda bi, i: (bi, i, cb))
    nxt = lambda cb: (lambda bi, i: (bi, jnp.minimum(i + 1, n_blk - 1), cb))
    ctx = lambda cb: (lambda bi, i: (bi, n_lat_tok // ctx_len, cb))
    kb, vb = KA0 // kvw, VA0 // kvw
    return pl.pallas_call(
        functools.partial(_attn_a_kernel, n_lat=n_lat),
        grid=(b, n_blk),
        in_specs=[
            pl.BlockSpec((None, tq, qw), cur(QA0 // qw)),
            pl.BlockSpec((None, tq, kvw), prev(kb)),
            pl.BlockSpec((None, tq, kvw), cur(kb)),
            pl.BlockSpec((None, tq, kvw), nxt(kb)),
            pl.BlockSpec((None, tq, kvw), prev(vb)),
            pl.BlockSpec((None, tq, kvw), cur(vb)),
            pl.BlockSpec((None, tq, kvw), nxt(vb)),
            pl.BlockSpec((None, ctx_len, kvw), ctx(kb)),
            pl.BlockSpec((None, ctx_len, kvw), ctx(vb)),
            pl.BlockSpec((None, tq, qw), cur(GA0 // qw)),
            pl.BlockSpec((A_Q_HEADS, HEAD_DIM), lambda bi, i: (0, 0)),
        ],
        out_specs=pl.BlockSpec((None, tq, qw), lambda bi, i: (bi, i, 0)),
        out_shape=jax.ShapeDtypeStruct((b, s, qw), BF16),
        compiler_params=_params(("parallel", "parallel")),
        name="attn_window",
    )(qkv, qkv, qkv, qkv, qkv, qkv, qkv, qkv, qkv, qkv, sink_b)


def _attn_b_kernel(q_ref, k_ref, v_ref, g_ref, o_ref, m_sc, l_sc, acc_sc, *, n_lat):
    i = pl.program_id(2)
    tq, tk = TQ_B, TK_B
    n_kv = k_ref.shape[0] // tk
    lo = jnp.where(i < n_lat, 0, n_lat * tq // tk)
    q = jnp.concatenate([q_ref[:, h * HEAD_DIM:(h + 1) * HEAD_DIM] for h in range(REP)], axis=0)
    m_sc[...] = jnp.full(m_sc.shape, NEG, F32)
    l_sc[...] = jnp.zeros(l_sc.shape, F32)
    acc_sc[...] = jnp.zeros(acc_sc.shape, F32)

    def body(ci, carry):
        off = pl.multiple_of(ci * tk, tk)
        s = _dot_nt(q, k_ref[pl.ds(off, tk), :])
        m_old = m_sc[...]
        m_new = jnp.maximum(m_old, jnp.max(s, axis=1, keepdims=True))
        alpha = jnp.exp(m_old - m_new)
        p = jnp.exp(s - m_new)
        l_sc[...] = alpha * l_sc[...] + jnp.sum(p, axis=1, keepdims=True)
        acc_sc[...] = alpha * acc_sc[...] + _dot(p.astype(BF16), v_ref[pl.ds(off, tk), :])
        m_sc[...] = m_new
        return carry

    lax.fori_loop(lo, n_kv, body, 0)
    o = acc_sc[...] * (1.0 / l_sc[...])
    for h in range(REP):
        cs = slice(h * HEAD_DIM, (h + 1) * HEAD_DIM)
        o_ref[:, cs] = (o[h * tq:(h + 1) * tq] * g_ref[:, cs].astype(F32)).astype(BF16)


def _attn_b(qkv, n_lat_tok):
    b, s, _ = qkv.shape
    tq = TQ_B
    n_t = s // tq
    n_lat = n_lat_tok // tq
    gw = REP * HEAD_DIM
    return pl.pallas_call(
        functools.partial(_attn_b_kernel, n_lat=n_lat),
        grid=(b, B_KV_HEADS, n_t),
        in_specs=[
            pl.BlockSpec((None, tq, gw), lambda bi, g, i: (bi, i, QB0 // gw + g)),
            pl.BlockSpec((None, s, HEAD_DIM), lambda bi, g, i: (bi, 0, KB0 // HEAD_DIM + g)),
            pl.BlockSpec((None, s, HEAD_DIM), lambda bi, g, i: (bi, 0, VB0 // HEAD_DIM + g)),
            pl.BlockSpec((None, tq, gw), lambda bi, g, i: (bi, i, GB0 // gw + g)),
        ],
        out_specs=pl.BlockSpec((None, tq, gw), lambda bi, g, i: (bi, i, g)),
        out_shape=jax.ShapeDtypeStruct((b, s, B_Q_HEADS * HEAD_DIM), BF16),
        scratch_shapes=[
            pltpu.VMEM((REP * tq, 1), F32),
            pltpu.VMEM((REP * tq, 1), F32),
            pltpu.VMEM((REP * tq, HEAD_DIM), F32),
        ],
        compiler_params=_params(("parallel", "parallel", "parallel")),
        name="attn_dense",
    )(qkv, qkv, qkv, qkv)


def _residual(x, y, mod, w_post):
    return x + mod[2:3] * _rms(y, w_post)


def _attn_out_kernel(ya_ref, yb_ref, wa_ref, wb_ref, x_ref, mod_ref, np_ref, o_ref):
    y = _dot(ya_ref[...], wa_ref[...]) + _dot(yb_ref[...], wb_ref[...])
    o_ref[...] = _residual(x_ref[...], y, mod_ref[...], np_ref[...])


def _attn_out(ya, yb, wa, wb, xs, modl, norm_post, n_tiles):
    b, s, d = xs.shape
    n_lat = s // TM - 1
    row = lambda bi, i: (bi, i, 0)
    const = lambda bi, i: (0, 0)
    return pl.pallas_call(
        _attn_out_kernel,
        grid=(b, n_tiles),
        in_specs=[
            pl.BlockSpec((None, TM, ya.shape[-1]), row),
            pl.BlockSpec((None, TM, yb.shape[-1]), row),
            pl.BlockSpec(wa.shape, const),
            pl.BlockSpec(wb.shape, const),
            pl.BlockSpec((None, TM, d), row),
            pl.BlockSpec((None, None, 3, d), lambda bi, i: (bi, i // n_lat, 0, 0)),
            pl.BlockSpec((1, d), const),
        ],
        out_specs=pl.BlockSpec((None, TM, d), row),
        out_shape=jax.ShapeDtypeStruct((b, n_tiles * TM, d), F32),
        compiler_params=_params(("parallel", "parallel")),
        name="attn_out",
    )(ya, yb, wa, wb, xs, modl, norm_post)


def _softplus(t):
    return jnp.maximum(t, 0.0) + jnp.log(1.0 + jnp.exp(-jnp.abs(t)))


def _ssm_in_kernel(x_ref, xp_ref, xn_ref, mod_ref, np_ref, wz_ref, wx_ref, wdt_ref, wdtt_ref,
                   cw_ref, cb_ref, dtb_ref, dtbt_ref, z_ref, xbc_ref, dt_ref, dtt_ref, *, n_lat):
    i = pl.program_id(1)
    tm = TM
    mod = mod_ref[...]
    w_pre = np_ref[...]
    hb = _pre_norm(x_ref[...], mod, w_pre).astype(BF16)
    hp = _pre_norm(xp_ref[...], mod, w_pre).astype(BF16)
    hn = _pre_norm(xn_ref[...], mod, w_pre).astype(BF16)
    ext = jnp.concatenate([hp, hb, hn], axis=0)

    z_ref[...] = _silu(_dot(hb, wz_ref[...])).astype(BF16)
    dt_ref[...] = _softplus(_dot(hb, wdt_ref[...]) + dtb_ref[...])
    dtt_ref[...] = _softplus(_dot_nt(wdtt_ref[...], hb) + dtbt_ref[...])

    is_ctx = i >= n_lat
    has_prev = jnp.logical_and(i > 0, jnp.logical_not(is_ctx)).astype(F32)
    has_next = jnp.logical_and(i < n_lat - 1, jnp.logical_not(is_ctx)).astype(F32)
    rr = lax.broadcasted_iota(jnp.int32, (tm + 16, 1), 0)
    keep = jnp.where(rr < 8, has_prev, jnp.where(rr >= tm + 8, has_next, 1.0))
    nblk = 512
    for j in range(wx_ref.shape[1] // nblk):
        cs = slice(j * nblk, (j + 1) * nblk)
        u = _dot(ext, wx_ref[:, cs]) * keep
        up = pltpu.roll(u, 1, 0)[8:8 + tm]
        un = pltpu.roll(u, tm + 15, 0)[8:8 + tm]
        conv = cb_ref[:, cs] + cw_ref[0:1, cs] * up + cw_ref[1:2, cs] * u[8:8 + tm] + cw_ref[2:3, cs] * un
        xbc_ref[:, cs] = _silu(conv).astype(BF16)


def _ssm_in(xs, modl, norm_pre, wz, wx, wdt, wdtt, cw, cb, dtb, dtbt):
    b, s, d = xs.shape
    n_t = s // TM
    n_lat = n_t - 1
    row = lambda bi, i: (bi, i, 0)
    const = lambda bi, i: (0, 0)
    r8 = TM // 8
    n8 = s // 8
    nh2 = wdt.shape[1]
    return pl.pallas_call(
        functools.partial(_ssm_in_kernel, n_lat=n_lat),
        grid=(b, n_t),
        in_specs=[
            pl.BlockSpec((None, TM, d), row),
            pl.BlockSpec((None, 8, d), lambda bi, i: (bi, jnp.maximum(i * r8 - 1, 0), 0)),
            pl.BlockSpec((None, 8, d), lambda bi, i: (bi, jnp.minimum((i + 1) * r8, n8 - 1), 0)),
            pl.BlockSpec((None, None, 3, d), lambda bi, i: (bi, i // n_lat, 0, 0)),
            pl.BlockSpec((1, d), const),
            pl.BlockSpec(wz.shape, const),
            pl.BlockSpec(wx.shape, const),
            pl.BlockSpec(wdt.shape, const),
            pl.BlockSpec(wdtt.shape, const),
            pl.BlockSpec(cw.shape, const),
            pl.BlockSpec(cb.shape, const),
            pl.BlockSpec(dtb.shape, const),
            pl.BlockSpec(dtbt.shape, const),
        ],
        out_specs=[
            pl.BlockSpec((None, TM, wz.shape[1]), row),
            pl.BlockSpec((None, TM, wx.shape[1]), row),
            pl.BlockSpec((None, TM, nh2), row),
            pl.BlockSpec((None, nh2, TM), lambda bi, i: (bi, 0, i)),
        ],
        out_shape=[
            jax.ShapeDtypeStruct((b, s, wz.shape[1]), BF16),
            jax.ShapeDtypeStruct((b, s, wx.shape[1]), BF16),
            jax.ShapeDtypeStruct((b, s, nh2), F32),
            jax.ShapeDtypeStruct((b, nh2, s), F32),
        ],
        compiler_params=_params(("parallel", "parallel")),
        name="ssm_in",
    )(xs, xs, xs, modl, norm_pre, wz, wx, wdt, wdtt, cw, cb, dtb, dtbt)


def _ssd_direction(xbc_ref, dt, dtt, a_row, a_col, expand, h_sc, y_ref, reverse, hoff):
    q_len = SSM_CHUNK
    d_inner = SSM_HEADS * SSM_HEAD_DIM
    gw = SSM_REP * SSM_HEAD_DIM
    row = lax.broadcasted_iota(jnp.int32, (q_len, q_len), 0)
    col = lax.broadcasted_iota(jnp.int32, (q_len, q_len), 1)
    lower = row >= col
    upper = row <= col
    mask = upper if reverse else lower
    tri = mask.astype(F32)
    tri_t = (lower if reverse else upper).astype(F32)

    a = dt * a_row
    at = dtt * a_col
    acum = _dot_hi(tri, a)
    acum_t = _dot_hi(at, tri_t)
    total = jnp.sum(a, axis=0, keepdims=True)

    x = xbc_ref[:, 0:d_inner].astype(F32)
    xdt_f = x * _dot_hi(dt, expand)
    xdt = xdt_f.astype(BF16)
    xw = (xdt_f * _dot_hi(jnp.exp(total - acum), expand)).astype(BF16)
    eac = _dot_hi(jnp.exp(acum), expand)
    etot = _dot_hi(jnp.broadcast_to(jnp.exp(total), (8, a.shape[1])), expand)[0:1]
    head_of_col = jnp.right_shift(lax.broadcasted_iota(jnp.int32, (q_len, gw), 1),
                                  SSM_HEAD_DIM.bit_length() - 1)

    for g in range(SSM_GROUPS):
        b_g = xbc_ref[:, d_inner + g * D_STATE:d_inner + (g + 1) * D_STATE]
        c_g = xbc_ref[:, d_inner + (SSM_GROUPS + g) * D_STATE:d_inner + (SSM_GROUPS + g + 1) * D_STATE]
        cb = _dot_nt(c_g, b_g)
        h_g = h_sc[g]
        gsl = slice(g * gw, (g + 1) * gw)
        y_off = _dot(c_g, h_g.astype(BF16)) * eac[:, gsl]
        xdt_g = xdt[:, gsl]
        ms = []
        xs_blk = []
        for r in range(SSM_REP):
            h = hoff + g * SSM_REP + r
            seg = acum[:, h:h + 1] - acum_t[h:h + 1, :]
            decay = jnp.exp(jnp.where(mask, seg, NEG))
            ms.append((cb * decay).astype(BF16))
            xs_blk.append(jnp.where(head_of_col == r, xdt_g, jnp.zeros_like(xdt_g)))
        y_diag = _dot(jnp.concatenate(ms, axis=1), jnp.concatenate(xs_blk, axis=0))
        y_ref[:, gsl] = (y_diag + y_off).astype(BF16)
        b_t = b_g.astype(F32).T.astype(BF16)
        h_sc[g] = h_g * etot[:, gsl] + _dot(b_t, xw[:, gsl])


def _ssd_kernel(xf_ref, xb_ref, dtf_ref, dtb_ref, dttf_ref, dttb_ref, alog_ref, alogt_ref, expf_ref, expb_ref,
                yf_ref, yb_ref, hf_sc, hb_sc):
    @pl.when(pl.program_id(1) == 0)
    def _():
        hf_sc[...] = jnp.zeros(hf_sc.shape, F32)
        hb_sc[...] = jnp.zeros(hb_sc.shape, F32)

    a_row = -jnp.exp(alog_ref[...])
    a_col = -jnp.exp(alogt_ref[...])
    _ssd_direction(xf_ref, dtf_ref[...], dttf_ref[...], a_row, a_col, expf_ref[...], hf_sc, yf_ref, False, 0)
    _ssd_direction(xb_ref, dtb_ref[...], dttb_ref[...], a_row, a_col, expb_ref[...], hb_sc, yb_ref, True,
                   SSM_HEADS)


def _ssd(xbc, dt, dtt, alog, alogt, expand_f, expand_b, n_lat_tok):
    b, s, cw = xbc.shape
    q_len = SSM_CHUNK
    n_c = s // q_len
    n_lat = n_lat_tok // q_len
    d_inner = SSM_HEADS * SSM_HEAD_DIM
    nh2 = dt.shape[-1]
    cf = lambda j: (j + n_lat) % n_c
    cbk = lambda j: n_c - 1 - j
    const = lambda bi, j: (0, 0)
    return pl.pallas_call(
        _ssd_kernel,
        grid=(b, n_c),
        in_specs=[
            pl.BlockSpec((None, q_len, cw), lambda bi, j: (bi, cf(j), 0)),
            pl.BlockSpec((None, q_len, cw), lambda bi, j: (bi, cbk(j), 0)),
            pl.BlockSpec((None, q_len, nh2), lambda bi, j: (bi, cf(j), 0)),
            pl.BlockSpec((None, q_len, nh2), lambda bi, j: (bi, cbk(j), 0)),
            pl.BlockSpec((None, nh2, q_len), lambda bi, j: (bi, 0, cf(j))),
            pl.BlockSpec((None, nh2, q_len), lambda bi, j: (bi, 0, cbk(j))),
            pl.BlockSpec(alog.shape, const),
            pl.BlockSpec(alogt.shape, const),
            pl.BlockSpec(expand_f.shape, const),
            pl.BlockSpec(expand_b.shape, const),
        ],
        out_specs=[
            pl.BlockSpec((None, q_len, d_inner), lambda bi, j: (bi, cf(j), 0)),
            pl.BlockSpec((None, q_len, d_inner), lambda bi, j: (bi, cbk(j), 0)),
        ],
        out_shape=[jax.ShapeDtypeStruct((b, s, d_inner), BF16)] * 2,
        scratch_shapes=[pltpu.VMEM((SSM_GROUPS, D_STATE, SSM_REP * SSM_HEAD_DIM), F32)] * 2,
        compiler_params=_params(("parallel", "arbitrary")),
        name="ssd_scan",
    )(xbc, xbc, dt, dt, dtt, dtt, alog, alogt, expand_f, expand_b)


def _ssm_out_kernel(yf_ref, yb_ref, xs_ref, z_ref, dsk_ref, nw_ref, w_ref, x_ref, mod_ref, np_ref, o_ref):
    y = yf_ref[...].astype(F32) + yb_ref[...].astype(F32) + dsk_ref[...] * xs_ref[...].astype(F32)
    gated = y * z_ref[...].astype(F32)
    gsz = gated.shape[1] // SSM_GROUPS
    parts = []
    for g in range(SSM_GROUPS):
        t = gated[:, g * gsz:(g + 1) * gsz]
        parts.append(t * lax.rsqrt(jnp.mean(t * t, axis=-1, keepdims=True) + EPS))
    gn = (jnp.concatenate(parts, axis=1) * nw_ref[...]).astype(BF16)
    o_ref[...] = _residual(x_ref[...], _dot(gn, w_ref[...]), mod_ref[...], np_ref[...])


def _ssm_out(yf, yb, xbc, z, dsk, nw, w, xs, modl, norm_post, n_tiles):
    b, s, d = xs.shape
    n_lat = s // TM - 1
    di = yf.shape[-1]
    row = lambda bi, i: (bi, i, 0)
    const = lambda bi, i: (0, 0)
    return pl.pallas_call(
        _ssm_out_kernel,
        grid=(b, n_tiles),
        in_specs=[
            pl.BlockSpec((None, TM, di), row),
            pl.BlockSpec((None, TM, di), row),
            pl.BlockSpec((None, TM, di), row),
            pl.BlockSpec((None, TM, di), row),
            pl.BlockSpec((1, di), const),
            pl.BlockSpec((1, di), const),
            pl.BlockSpec(w.shape, const),
            pl.BlockSpec((None, TM, d), row),
            pl.BlockSpec((None, None, 3, d), lambda bi, i: (bi, i // n_lat, 0, 0)),
            pl.BlockSpec((1, d), const),
        ],
        out_specs=pl.BlockSpec((None, TM, d), row),
        out_shape=jax.ShapeDtypeStruct((b, n_tiles * TM, d), F32),
        compiler_params=_params(("parallel", "parallel")),
        name="ssm_out",
    )(yf, yb, xbc, z, dsk, nw, w, xs, modl, norm_post)


def _rope_tables(n_lat_tok, n_ctx_tok):
    t = np.arange(n_lat_tok)
    n_freq = HEAD_DIM // 4
    inv = 1.0 / (ROPE_THETA ** (jnp.arange(n_freq, dtype=F32) / n_freq))
    rowp = jnp.asarray(t // GRID_W, F32)
    colp = jnp.asarray(t % GRID_W, F32)
    ang = jnp.concatenate([rowp[:, None] * inv, colp[:, None] * inv], axis=-1)
    cos, sin = jnp.cos(ang), jnp.sin(ang)
    cos2 = jnp.concatenate([cos, cos], axis=-1)
    sin2 = jnp.concatenate([-sin, sin], axis=-1)
    cos2 = jnp.concatenate([cos2, jnp.ones((n_ctx_tok, HEAD_DIM), F32)], axis=0)
    sin2 = jnp.concatenate([sin2, jnp.zeros((n_ctx_tok, HEAD_DIM), F32)], axis=0)
    return cos2, sin2


_DEINT = np.concatenate([np.arange(0, HEAD_DIM, 2), np.arange(1, HEAD_DIM, 2)])


def _attn_weight_columns():
    qa, ka, va, ga, qb, kb, vb, gb = 0, 1024, 1280, 1536, 2560, 3584, 3840, 4096

    def heads(start, n, perm):
        base = start + HEAD_DIM * np.arange(n)[:, None]
        return (base + (_DEINT if perm else np.arange(HEAD_DIM))[None, :]).reshape(-1)

    return np.concatenate([
        heads(qa, 8, True), heads(ga, 8, False), heads(qb, 8, True), heads(gb, 8, False),
        heads(ka, 2, True), heads(va, 2, False), heads(kb, 2, True), heads(vb, 2, False)])


_ATTN_COLS_IDX = _attn_weight_columns()


def kernel(x, c, ctx, c_ctx, w_ada, b_ada, norm_pre, norm_post, attn_w_in, attn_w_out, attn_sink,
           attn_q_norm, attn_k_norm, ssm_w_in, ssm_conv_w, ssm_conv_b, ssm_dt_bias, ssm_a_log, ssm_d,
           ssm_norm, ssm_w_out):
    bsz, n_lat_tok, d = x.shape
    n_ctx_tok = ctx.shape[1]
    depth = w_ada.shape[0]
    assert n_ctx_tok == TM and n_lat_tok % TM == 0 and bsz <= 7
    d_inner = SSM_HEADS * SSM_HEAD_DIM
    bc_w = 2 * SSM_GROUPS * D_STATE

    xs = jnp.concatenate([x, ctx], axis=1)
    cc = jnp.zeros((8, d), F32).at[:bsz].set(c).at[bsz].set(c_ctx)
    mod = _modulation(cc, w_ada, b_ada)
    mod = mod.reshape(depth, 8, 3, d)
    cos2, sin2 = _rope_tables(n_lat_tok, n_ctx_tok)
    eye_rep = np.repeat(np.eye(SSM_HEADS, dtype=np.float32), SSM_HEAD_DIM, axis=1)
    expand_f = jnp.asarray(np.pad(eye_rep, ((0, HPAD - SSM_HEADS), (0, 0))))
    expand_b = jnp.asarray(np.pad(eye_rep, ((SSM_HEADS, HPAD - 2 * SSM_HEADS), (0, 0))))

    for l in range(depth):
        last = l == depth - 1
        n_tiles = (n_lat_tok if last else n_lat_tok + n_ctx_tok) // TM
        modl = jnp.stack([mod[l, :bsz], jnp.broadcast_to(mod[l, bsz], (bsz, 3, d))], axis=1)
        npre = norm_pre[l].reshape(1, d)
        npost = norm_post[l].reshape(1, d)
        i = l // 2
        if l % 2 == 0:
            w = attn_w_in[i][:, _ATTN_COLS_IDX].astype(BF16)
            qn = attn_q_norm[i][_DEINT].reshape(1, HEAD_DIM)
            kn = attn_k_norm[i][_DEINT].reshape(1, HEAD_DIM)
            qkv = _attn_in(xs, modl, npre, w, cos2, sin2, qn, kn)
            sink_b = jnp.broadcast_to(attn_sink[i][:, None], (A_Q_HEADS, HEAD_DIM))
            ya = _attn_a(qkv, sink_b, n_lat_tok)
            yb = _attn_b(qkv, n_lat_tok)
            wo = attn_w_out[i].astype(BF16)
            aq = A_Q_HEADS * HEAD_DIM
            xs = _attn_out(ya, yb, wo[:aq], wo[aq:], xs, modl, npost, n_tiles)
        else:
            w = ssm_w_in[i]
            wz = w[:, :d_inner].astype(BF16)
            wx = w[:, d_inner:2 * d_inner + bc_w].astype(BF16)
            wdt = jnp.pad(w[:, 2 * d_inner + bc_w:], ((0, 0), (0, HPAD - 2 * SSM_HEADS))).astype(BF16)
            dtb = jnp.pad(ssm_dt_bias[i].reshape(1, -1), ((0, 0), (0, HPAD - 2 * SSM_HEADS)))
            z, xbc, dt, dtt = _ssm_in(xs, modl, npre, wz, wx, wdt, wdt.T, ssm_conv_w[i],
                                      ssm_conv_b[i].reshape(1, -1), dtb, dtb.reshape(-1, 1))
            alog = jnp.pad(ssm_a_log[i].reshape(1, -1), ((0, 0), (0, HPAD - 2 * SSM_HEADS)))
            yf, ybk = _ssd(xbc, dt, dtt, alog, alog.reshape(-1, 1), expand_f, expand_b, n_lat_tok)
            dsk = jnp.repeat(ssm_d[i], SSM_HEAD_DIM).reshape(1, d_inner)
            xs = _ssm_out(yf, ybk, xbc, z, dsk, ssm_norm[i].reshape(1, d_inner),
                          ssm_w_out[i].astype(BF16), xs, modl, npost, n_tiles)
    return xs
```

```python
import functools

import numpy as np
import jax
import jax.numpy as jnp
from jax import lax
from jax.experimental import pallas as pl
from jax.experimental.pallas import tpu as pltpu

F32 = jnp.float32
BF16 = jnp.bfloat16

EPS = 1e-6
GRID_W = 64
ROPE_THETA = 10000.0
HEAD_DIM = 128
A_Q_HEADS = 8
A_KV_HEADS = 2
B_Q_HEADS = 8
B_KV_HEADS = 2
REP = 4
WINDOW = 128
SSM_HEAD_DIM = 64
SSM_HEADS = 32
SSM_GROUPS = 8
SSM_REP = SSM_HEADS // SSM_GROUPS
D_STATE = 128
SSM_CHUNK = 128
HPAD = 128

V7X_VMEM_BYTES = 64 * 1024 * 1024
VMEM_LIMIT = V7X_VMEM_BYTES - 8 * 1024 * 1024

TM = 256
TQ_A = 128
TQ_B = 256
TK_B = 512
LANES = 128
LOG2E = 1.4426950408889634

QA0, GA0, QB0, GB0, KA0, VA0, KB0, VB0 = 0, 1024, 2048, 3072, 4096, 4352, 4608, 4864
ATTN_COLS = 5120

NEG = -1e30


def _params(sem, vmem=VMEM_LIMIT):
    return pltpu.CompilerParams(dimension_semantics=sem, vmem_limit_bytes=vmem)


def _silu(t):
    return t * (1.0 / (1.0 + jnp.exp(-t)))


def _rms(t, w):
    return t * lax.rsqrt(jnp.mean(t * t, axis=-1, keepdims=True) + EPS) * w


def _dot(a, b):
    return jnp.dot(a, b, preferred_element_type=F32)


def _dot_nt(a, b):
    return lax.dot_general(a, b, (((1,), (1,)), ((), ())), preferred_element_type=F32)


def _dot_hi(a, b):
    return jnp.dot(a, b, preferred_element_type=F32, precision=lax.Precision.HIGHEST)


def _dot_hi_nt(a, b):
    return lax.dot_general(a, b, (((1,), (1,)), ((), ())), preferred_element_type=F32,
                           precision=lax.Precision.HIGHEST)


def _mod_kernel(cc_ref, w_ref, b_ref, o_ref):
    o_ref[...] = _dot_hi(_silu(cc_ref[...]), w_ref[...]) + b_ref[...]


def _modulation(cc, w_ada, b_ada):
    depth, d, d3 = w_ada.shape
    return pl.pallas_call(
        _mod_kernel,
        grid=(depth,),
        in_specs=[
            pl.BlockSpec((8, d), lambda l: (0, 0)),
            pl.BlockSpec((None, d, d3), lambda l: (l, 0, 0)),
            pl.BlockSpec((None, 1, d3), lambda l: (l, 0, 0)),
        ],
        out_specs=pl.BlockSpec((None, 8, d3), lambda l: (l, 0, 0)),
        out_shape=jax.ShapeDtypeStruct((depth, 8, d3), F32),
        compiler_params=_params(("arbitrary",)),
        name="modulation",
    )(cc, w_ada, b_ada.reshape(depth, 1, d3))


def _pre_norm(x, mod, w):
    return _rms(x, w) * (1.0 + mod[1:2]) + mod[0:1]


def _attn_in_kernel(x_ref, mod_ref, np_ref, w_ref, cos_ref, sin_ref, qn_ref, kn_ref, o_ref):
    hb = _pre_norm(x_ref[...], mod_ref[...], np_ref[...]).astype(BF16)
    cos = cos_ref[...]
    sin = sin_ref[...]
    scale = HEAD_DIM ** -0.5

    def rope(t):
        return t * cos + pltpu.roll(t, HEAD_DIM // 2, 1) * sin

    nblk = 512
    for j in range(ATTN_COLS // nblk):
        c0 = j * nblk
        t = _dot(hb, w_ref[:, c0:c0 + nblk])
        for hh in range(nblk // HEAD_DIM):
            col = c0 + hh * HEAD_DIM
            th = t[:, hh * HEAD_DIM:(hh + 1) * HEAD_DIM]
            if col < GA0:
                th = rope(th) * scale
            elif col < QB0 or GB0 <= col < KA0:
                th = _silu(th)
            elif col < GB0:
                th = rope(_rms(th, qn_ref[...])) * (scale * LOG2E)
            elif col < VA0:
                th = rope(th)
            elif KB0 <= col < VB0:
                th = rope(_rms(th, kn_ref[...]))
            o_ref[:, col:col + HEAD_DIM] = th.astype(BF16)


def _attn_in(xs, modl, norm_pre, w, cos2, sin2, qn, kn):
    b, s, d = xs.shape
    n_t = s // TM
    n_lat = n_t - 1
    row = lambda bi, i: (bi, i, 0)
    return pl.pallas_call(
        _attn_in_kernel,
        grid=(b, n_t),
        in_specs=[
            pl.BlockSpec((None, TM, d), row),
            pl.BlockSpec((None, None, 3, d), lambda bi, i: (bi, i // n_lat, 0, 0)),
            pl.BlockSpec((1, d), lambda bi, i: (0, 0)),
            pl.BlockSpec((d, ATTN_COLS), lambda bi, i: (0, 0)),
            pl.BlockSpec((TM, HEAD_DIM), lambda bi, i: (i, 0)),
            pl.BlockSpec((TM, HEAD_DIM), lambda bi, i: (i, 0)),
            pl.BlockSpec((1, HEAD_DIM), lambda bi, i: (0, 0)),
            pl.BlockSpec((1, HEAD_DIM), lambda bi, i: (0, 0)),
        ],
        out_specs=pl.BlockSpec((None, TM, ATTN_COLS), row),
        out_shape=jax.ShapeDtypeStruct((b, s, ATTN_COLS), BF16),
        compiler_params=_params(("parallel", "parallel")),
        name="attn_in",
    )(xs, modl, norm_pre, w, cos2, sin2, qn, kn)


def _attn_a_kernel(q_ref, kp_ref, kc_ref, kn_ref, vp_ref, vc_ref, vn_ref, kx_ref, vx_ref,
                   g_ref, sink_ref, o_ref, *, n_lat):
    i = pl.program_id(1)
    tq = TQ_A
    rows = REP * tq
    r = lax.broadcasted_iota(jnp.int32, (rows, tq), 0) & (tq - 1)
    c = lax.broadcasted_iota(jnp.int32, (rows, tq), 1)
    is_lat = i < n_lat
    lim_prev = jnp.where(jnp.logical_and(i > 0, is_lat), 0, tq)
    lim_next = jnp.where(jnp.logical_and(i < n_lat - 1, is_lat), 0, tq)
    m_prev = (c - r) >= lim_prev
    m_next = (r - c) >= lim_next
    pen_cur = jnp.where(is_lat, 0.0, NEG)
    for g in range(A_KV_HEADS):
        gs = slice(g * HEAD_DIM, (g + 1) * HEAD_DIM)
        q = jnp.concatenate(
            [q_ref[:, (g * REP + h) * HEAD_DIM:(g * REP + h + 1) * HEAD_DIM] for h in range(REP)], axis=0)
        sp = jnp.where(m_prev, _dot_nt(q, kp_ref[:, gs]), NEG)
        sc = _dot_nt(q, kc_ref[:, gs]) + pen_cur
        sn = jnp.where(m_next, _dot_nt(q, kn_ref[:, gs]), NEG)
        sx = _dot_nt(q, kx_ref[:, gs])
        sk = jnp.concatenate(
            [jnp.broadcast_to(sink_ref[g * REP + h:g * REP + h + 1, 0:1], (tq, 1)) for h in range(REP)], axis=0)
        m = jnp.maximum(jnp.maximum(jnp.max(sp, axis=1, keepdims=True), jnp.max(sc, axis=1, keepdims=True)),
                        jnp.maximum(jnp.max(sn, axis=1, keepdims=True), jnp.max(sx, axis=1, keepdims=True)))
        m = jnp.maximum(m, sk)
        pp = jnp.exp(sp - m)
        pc = jnp.exp(sc - m)
        pn = jnp.exp(sn - m)
        px = jnp.exp(sx - m)
        den = (jnp.sum(pp, axis=1, keepdims=True) + jnp.sum(pc, axis=1, keepdims=True)
               + jnp.sum(pn, axis=1, keepdims=True) + jnp.sum(px, axis=1, keepdims=True) + jnp.exp(sk - m))
        o = (_dot(pp.astype(BF16), vp_ref[:, gs]) + _dot(pc.astype(BF16), vc_ref[:, gs])
             + _dot(pn.astype(BF16), vn_ref[:, gs]) + _dot(px.astype(BF16), vx_ref[:, gs])) * (1.0 / den)
        for h in range(REP):
            cs = slice((g * REP + h) * HEAD_DIM, (g * REP + h + 1) * HEAD_DIM)
            o_ref[:, cs] = (o[h * tq:(h + 1) * tq] * g_ref[:, cs].astype(F32)).astype(BF16)


def _attn_a(qkv, sink_b, n_lat_tok):
    b, s, _ = qkv.shape
    tq = TQ_A
    n_blk = s // tq
    n_lat = n_lat_tok // tq
    kvw = A_KV_HEADS * HEAD_DIM
    ctx_len = s - n_lat_tok
    qw = A_Q_HEADS * HEAD_DIM
    prev = lambda cb: (lambda bi, i: (bi, jnp.maximum(i - 1, 0), cb))
    cur = lambda cb: (lambda bi, i: (bi, i, cb))
    nxt = lambda cb: (lambda bi, i: (bi, jnp.minimum(i + 1, n_blk - 1), cb))
    ctx = lambda cb: (lambda bi, i: (bi, n_lat_tok // ctx_len, cb))
    kb, vb = KA0 // kvw, VA0 // kvw
    return pl.pallas_call(
        functools.partial(_attn_a_kernel, n_lat=n_lat),
        grid=(b, n_blk),
        in_specs=[
            pl.BlockSpec((None, tq, qw), cur(QA0 // qw)),
            pl.BlockSpec((None, tq, kvw), prev(kb)),
            pl.BlockSpec((None, tq, kvw), cur(kb)),
            pl.BlockSpec((None, tq, kvw), nxt(kb)),
            pl.BlockSpec((None, tq, kvw), prev(vb)),
            pl.BlockSpec((None, tq, kvw), cur(vb)),
            pl.BlockSpec((None, tq, kvw), nxt(vb)),
            pl.BlockSpec((None, ctx_len, kvw), ctx(kb)),
            pl.BlockSpec((None, ctx_len, kvw), ctx(vb)),
            pl.BlockSpec((None, tq, qw), cur(GA0 // qw)),
            pl.BlockSpec((A_Q_HEADS, HEAD_DIM), lambda bi, i: (0, 0)),
        ],
        out_specs=pl.BlockSpec((None, tq, qw), lambda bi, i: (bi, i, 0)),
        out_shape=jax.ShapeDtypeStruct((b, s, qw), BF16),
        compiler_params=_params(("parallel", "parallel")),
        name="attn_window",
    )(qkv, qkv, qkv, qkv, qkv, qkv, qkv, qkv, qkv, qkv, sink_b)


def _attn_b_kernel(q_ref, k_ref, v_ref, g_ref, o_ref, m_sc, l_sc, acc_sc, *, n_lat, n_lat_tok):
    i = pl.program_id(2)
    tq, tk = TQ_B, TK_B
    n_ctx_tok = k_ref.shape[0] - n_lat_tok
    n_main = jnp.where(i < n_lat, n_lat_tok // tk, 0)
    q = jnp.concatenate([q_ref[:, h * HEAD_DIM:(h + 1) * HEAD_DIM] for h in range(REP)], axis=0)
    m_sc[...] = jnp.full(m_sc.shape, NEG, F32)
    l_sc[...] = jnp.zeros(l_sc.shape, F32)
    acc_sc[...] = jnp.zeros(acc_sc.shape, F32)

    def step(k, v):
        s = _dot_nt(q, k)
        m_old = m_sc[...]
        m_new = jnp.maximum(m_old, jnp.max(s, axis=1, keepdims=True))
        alpha = jnp.exp2(m_old - m_new)
        p = jnp.exp2(s - pltpu.repeat(m_new, s.shape[1] // LANES, axis=1))
        l_sc[...] = alpha * l_sc[...] + jnp.sum(p, axis=1, keepdims=True)
        acc_sc[...] = alpha * acc_sc[...] + _dot(p.astype(BF16), v)
        m_sc[...] = m_new

    def body(ci, carry):
        off = pl.multiple_of(ci * tk, tk)
        step(k_ref[pl.ds(off, tk), :], v_ref[pl.ds(off, tk), :])
        return carry

    lax.fori_loop(0, n_main, body, 0)
    step(k_ref[n_lat_tok:n_lat_tok + n_ctx_tok, :], v_ref[n_lat_tok:n_lat_tok + n_ctx_tok, :])
    o = acc_sc[...] * (1.0 / l_sc[...])
    for h in range(REP):
        cs = slice(h * HEAD_DIM, (h + 1) * HEAD_DIM)
        o_ref[:, cs] = (o[h * tq:(h + 1) * tq] * g_ref[:, cs].astype(F32)).astype(BF16)


def _attn_b(qkv, n_lat_tok):
    b, s, _ = qkv.shape
    tq = TQ_B
    n_t = s // tq
    n_lat = n_lat_tok // tq
    gw = REP * HEAD_DIM
    return pl.pallas_call(
        functools.partial(_attn_b_kernel, n_lat=n_lat, n_lat_tok=n_lat_tok),
        grid=(b, B_KV_HEADS, n_t),
        in_specs=[
            pl.BlockSpec((None, tq, gw), lambda bi, g, i: (bi, i, QB0 // gw + g)),
            pl.BlockSpec((None, s, HEAD_DIM), lambda bi, g, i: (bi, 0, KB0 // HEAD_DIM + g)),
            pl.BlockSpec((None, s, HEAD_DIM), lambda bi, g, i: (bi, 0, VB0 // HEAD_DIM + g)),
            pl.BlockSpec((None, tq, gw), lambda bi, g, i: (bi, i, GB0 // gw + g)),
        ],
        out_specs=pl.BlockSpec((None, tq, gw), lambda bi, g, i: (bi, i, g)),
        out_shape=jax.ShapeDtypeStruct((b, s, B_Q_HEADS * HEAD_DIM), BF16),
        scratch_shapes=[
            pltpu.VMEM((REP * tq, LANES), F32),
            pltpu.VMEM((REP * tq, LANES), F32),
            pltpu.VMEM((REP * tq, HEAD_DIM), F32),
        ],
        compiler_params=_params(("parallel", "parallel", "parallel")),
        name="attn_dense",
    )(qkv, qkv, qkv, qkv)


def _residual(x, y, mod, w_post):
    return x + mod[2:3] * _rms(y, w_post)


def _attn_out_kernel(ya_ref, yb_ref, wa_ref, wb_ref, x_ref, mod_ref, np_ref, o_ref):
    y = _dot(ya_ref[...], wa_ref[...]) + _dot(yb_ref[...], wb_ref[...])
    o_ref[...] = _residual(x_ref[...], y, mod_ref[...], np_ref[...])


def _attn_out(ya, yb, wa, wb, xs, modl, norm_post, n_tiles):
    b, s, d = xs.shape
    n_lat = s // TM - 1
    row = lambda bi, i: (bi, i, 0)
    const = lambda bi, i: (0, 0)
    return pl.pallas_call(
        _attn_out_kernel,
        grid=(b, n_tiles),
        in_specs=[
            pl.BlockSpec((None, TM, ya.shape[-1]), row),
            pl.BlockSpec((None, TM, yb.shape[-1]), row),
            pl.BlockSpec(wa.shape, const),
            pl.BlockSpec(wb.shape, const),
            pl.BlockSpec((None, TM, d), row),
            pl.BlockSpec((None, None, 3, d), lambda bi, i: (bi, i // n_lat, 0, 0)),
            pl.BlockSpec((1, d), const),
        ],
        out_specs=pl.BlockSpec((None, TM, d), row),
        out_shape=jax.ShapeDtypeStruct((b, n_tiles * TM, d), F32),
        compiler_params=_params(("parallel", "parallel")),
        name="attn_out",
    )(ya, yb, wa, wb, xs, modl, norm_post)


def _softplus(t):
    return jnp.maximum(t, 0.0) + jnp.log(1.0 + jnp.exp(-jnp.abs(t)))


def _ssm_in_kernel(x_ref, xp_ref, xn_ref, mod_ref, np_ref, wz_ref, wx_ref, wdt_ref, wdtt_ref,
                   cw_ref, cb_ref, dtb_ref, dtbt_ref, z_ref, xbc_ref, dt_ref, dtt_ref, *, n_lat):
    i = pl.program_id(1)
    tm = TM
    mod = mod_ref[...]
    w_pre = np_ref[...]
    hb = _pre_norm(x_ref[...], mod, w_pre).astype(BF16)
    hp = _pre_norm(xp_ref[...], mod, w_pre).astype(BF16)
    hn = _pre_norm(xn_ref[...], mod, w_pre).astype(BF16)
    ext = jnp.concatenate([hp, hb, hn], axis=0)

    z_ref[...] = _silu(_dot(hb, wz_ref[...])).astype(BF16)
    dt_ref[...] = _softplus(_dot(hb, wdt_ref[...]) + dtb_ref[...])
    dtt_ref[...] = _softplus(_dot_nt(wdtt_ref[...], hb) + dtbt_ref[...])

    is_ctx = i >= n_lat
    has_prev = jnp.logical_and(i > 0, jnp.logical_not(is_ctx)).astype(F32)
    has_next = jnp.logical_and(i < n_lat - 1, jnp.logical_not(is_ctx)).astype(F32)
    rr = lax.broadcasted_iota(jnp.int32, (tm + 16, 1), 0)
    keep = jnp.where(rr < 8, has_prev, jnp.where(rr >= tm + 8, has_next, 1.0))
    nblk = 512
    for j in range(wx_ref.shape[1] // nblk):
        cs = slice(j * nblk, (j + 1) * nblk)
        u = _dot(ext, wx_ref[:, cs]) * keep
        up = pltpu.roll(u, 1, 0)[8:8 + tm]
        un = pltpu.roll(u, tm + 15, 0)[8:8 + tm]
        conv = cb_ref[:, cs] + cw_ref[0:1, cs] * up + cw_ref[1:2, cs] * u[8:8 + tm] + cw_ref[2:3, cs] * un
        xbc_ref[:, cs] = _silu(conv).astype(BF16)


def _ssm_in(xs, modl, norm_pre, wz, wx, wdt, wdtt, cw, cb, dtb, dtbt):
    b, s, d = xs.shape
    n_t = s // TM
    n_lat = n_t - 1
    row = lambda bi, i: (bi, i, 0)
    const = lambda bi, i: (0, 0)
    r8 = TM // 8
    n8 = s // 8
    nh2 = wdt.shape[1]
    return pl.pallas_call(
        functools.partial(_ssm_in_kernel, n_lat=n_lat),
        grid=(b, n_t),
        in_specs=[
            pl.BlockSpec((None, TM, d), row),
            pl.BlockSpec((None, 8, d), lambda bi, i: (bi, jnp.maximum(i * r8 - 1, 0), 0)),
            pl.BlockSpec((None, 8, d), lambda bi, i: (bi, jnp.minimum((i + 1) * r8, n8 - 1), 0)),
            pl.BlockSpec((None, None, 3, d), lambda bi, i: (bi, i // n_lat, 0, 0)),
            pl.BlockSpec((1, d), const),
            pl.BlockSpec(wz.shape, const),
            pl.BlockSpec(wx.shape, const),
            pl.BlockSpec(wdt.shape, const),
            pl.BlockSpec(wdtt.shape, const),
            pl.BlockSpec(cw.shape, const),
            pl.BlockSpec(cb.shape, const),
            pl.BlockSpec(dtb.shape, const),
            pl.BlockSpec(dtbt.shape, const),
        ],
        out_specs=[
            pl.BlockSpec((None, TM, wz.shape[1]), row),
            pl.BlockSpec((None, TM, wx.shape[1]), row),
            pl.BlockSpec((None, TM, nh2), row),
            pl.BlockSpec((None, nh2, TM), lambda bi, i: (bi, 0, i)),
        ],
        out_shape=[
            jax.ShapeDtypeStruct((b, s, wz.shape[1]), BF16),
            jax.ShapeDtypeStruct((b, s, wx.shape[1]), BF16),
            jax.ShapeDtypeStruct((b, s, nh2), F32),
            jax.ShapeDtypeStruct((b, nh2, s), F32),
        ],
        compiler_params=_params(("parallel", "parallel")),
        name="ssm_in",
    )(xs, xs, xs, modl, norm_pre, wz, wx, wdt, wdtt, cw, cb, dtb, dtbt)


def _ssd_direction(xbc_ref, dt, dtt, a_row, a_col, expand, h_sc, y_ref, reverse, hoff):
    q_len = SSM_CHUNK
    d_inner = SSM_HEADS * SSM_HEAD_DIM
    gw = SSM_REP * SSM_HEAD_DIM
    row = lax.broadcasted_iota(jnp.int32, (q_len, q_len), 0)
    col = lax.broadcasted_iota(jnp.int32, (q_len, q_len), 1)
    lower = row >= col
    upper = row <= col
    mask = upper if reverse else lower
    tri = mask.astype(F32)
    tri_t = (lower if reverse else upper).astype(F32)

    a = dt * a_row
    at = dtt * a_col
    acum = _dot_hi(tri, a)
    acum_t = _dot_hi(at, tri_t)
    total = jnp.sum(a, axis=0, keepdims=True)

    x = xbc_ref[:, 0:d_inner].astype(F32)
    xdt_f = x * _dot_hi(dt, expand)
    xdt = xdt_f.astype(BF16)
    xw = (xdt_f * _dot_hi(jnp.exp(total - acum), expand)).astype(BF16)
    eac = _dot_hi(jnp.exp(acum), expand)
    etot = _dot_hi(jnp.broadcast_to(jnp.exp(total), (8, a.shape[1])), expand)[0:1]
    head_of_col = jnp.right_shift(lax.broadcasted_iota(jnp.int32, (q_len, gw), 1),
                                  SSM_HEAD_DIM.bit_length() - 1)

    for g in range(SSM_GROUPS):
        b_g = xbc_ref[:, d_inner + g * D_STATE:d_inner + (g + 1) * D_STATE]
        c_g = xbc_ref[:, d_inner + (SSM_GROUPS + g) * D_STATE:d_inner + (SSM_GROUPS + g + 1) * D_STATE]
        cb = _dot_nt(c_g, b_g)
        h_g = h_sc[g]
        gsl = slice(g * gw, (g + 1) * gw)
        y_off = _dot(c_g, h_g.astype(BF16)) * eac[:, gsl]
        xdt_g = xdt[:, gsl]
        ms = []
        xs_blk = []
        for r in range(SSM_REP):
            h = hoff + g * SSM_REP + r
            seg = acum[:, h:h + 1] - acum_t[h:h + 1, :]
            decay = jnp.exp(jnp.where(mask, seg, NEG))
            ms.append((cb * decay).astype(BF16))
            xs_blk.append(jnp.where(head_of_col == r, xdt_g, jnp.zeros_like(xdt_g)))
        y_diag = _dot(jnp.concatenate(ms, axis=1), jnp.concatenate(xs_blk, axis=0))
        y_ref[:, gsl] = (y_diag + y_off).astype(BF16)
        b_t = b_g.astype(F32).T.astype(BF16)
        h_sc[g] = h_g * etot[:, gsl] + _dot(b_t, xw[:, gsl])


def _ssd_kernel(xf_ref, xb_ref, dtf_ref, dtb_ref, dttf_ref, dttb_ref, alog_ref, alogt_ref, expf_ref, expb_ref,
                yf_ref, yb_ref, hf_sc, hb_sc):
    @pl.when(pl.program_id(1) == 0)
    def _():
        hf_sc[...] = jnp.zeros(hf_sc.shape, F32)
        hb_sc[...] = jnp.zeros(hb_sc.shape, F32)

    a_row = -jnp.exp(alog_ref[...])
    a_col = -jnp.exp(alogt_ref[...])
    _ssd_direction(xf_ref, dtf_ref[...], dttf_ref[...], a_row, a_col, expf_ref[...], hf_sc, yf_ref, False, 0)
    _ssd_direction(xb_ref, dtb_ref[...], dttb_ref[...], a_row, a_col, expb_ref[...], hb_sc, yb_ref, True,
                   SSM_HEADS)


def _ssd(xbc, dt, dtt, alog, alogt, expand_f, expand_b, n_lat_tok):
    b, s, cw = xbc.shape
    q_len = SSM_CHUNK
    n_c = s // q_len
    n_lat = n_lat_tok // q_len
    d_inner = SSM_HEADS * SSM_HEAD_DIM
    nh2 = dt.shape[-1]
    cf = lambda j: (j + n_lat) % n_c
    cbk = lambda j: n_c - 1 - j
    const = lambda bi, j: (0, 0)
    return pl.pallas_call(
        _ssd_kernel,
        grid=(b, n_c),
        in_specs=[
            pl.BlockSpec((None, q_len, cw), lambda bi, j: (bi, cf(j), 0)),
            pl.BlockSpec((None, q_len, cw), lambda bi, j: (bi, cbk(j), 0)),
            pl.BlockSpec((None, q_len, nh2), lambda bi, j: (bi, cf(j), 0)),
            pl.BlockSpec((None, q_len, nh2), lambda bi, j: (bi, cbk(j), 0)),
            pl.BlockSpec((None, nh2, q_len), lambda bi, j: (bi, 0, cf(j))),
            pl.BlockSpec((None, nh2, q_len), lambda bi, j: (bi, 0, cbk(j))),
            pl.BlockSpec(alog.shape, const),
            pl.BlockSpec(alogt.shape, const),
            pl.BlockSpec(expand_f.shape, const),
            pl.BlockSpec(expand_b.shape, const),
        ],
        out_specs=[
            pl.BlockSpec((None, q_len, d_inner), lambda bi, j: (bi, cf(j), 0)),
            pl.BlockSpec((None, q_len, d_inner), lambda bi, j: (bi, cbk(j), 0)),
        ],
        out_shape=[jax.ShapeDtypeStruct((b, s, d_inner), BF16)] * 2,
        scratch_shapes=[pltpu.VMEM((SSM_GROUPS, D_STATE, SSM_REP * SSM_HEAD_DIM), F32)] * 2,
        compiler_params=_params(("parallel", "arbitrary")),
        name="ssd_scan",
    )(xbc, xbc, dt, dt, dtt, dtt, alog, alogt, expand_f, expand_b)


def _ssm_out_kernel(yf_ref, yb_ref, xs_ref, z_ref, dsk_ref, nw_ref, w_ref, x_ref, mod_ref, np_ref, o_ref):
    y = yf_ref[...].astype(F32) + yb_ref[...].astype(F32) + dsk_ref[...] * xs_ref[...].astype(F32)
    gated = y * z_ref[...].astype(F32)
    gsz = gated.shape[1] // SSM_GROUPS
    parts = []
    for g in range(SSM_GROUPS):
        t = gated[:, g * gsz:(g + 1) * gsz]
        parts.append(t * lax.rsqrt(jnp.mean(t * t, axis=-1, keepdims=True) + EPS))
    gn = (jnp.concatenate(parts, axis=1) * nw_ref[...]).astype(BF16)
    o_ref[...] = _residual(x_ref[...], _dot(gn, w_ref[...]), mod_ref[...], np_ref[...])


def _ssm_out(yf, yb, xbc, z, dsk, nw, w, xs, modl, norm_post, n_tiles):
    b, s, d = xs.shape
    n_lat = s // TM - 1
    di = yf.shape[-1]
    row = lambda bi, i: (bi, i, 0)
    const = lambda bi, i: (0, 0)
    return pl.pallas_call(
        _ssm_out_kernel,
        grid=(b, n_tiles),
        in_specs=[
            pl.BlockSpec((None, TM, di), row),
            pl.BlockSpec((None, TM, di), row),
            pl.BlockSpec((None, TM, di), row),
            pl.BlockSpec((None, TM, di), row),
            pl.BlockSpec((1, di), const),
            pl.BlockSpec((1, di), const),
            pl.BlockSpec(w.shape, const),
            pl.BlockSpec((None, TM, d), row),
            pl.BlockSpec((None, None, 3, d), lambda bi, i: (bi, i // n_lat, 0, 0)),
            pl.BlockSpec((1, d), const),
        ],
        out_specs=pl.BlockSpec((None, TM, d), row),
        out_shape=jax.ShapeDtypeStruct((b, n_tiles * TM, d), F32),
        compiler_params=_params(("parallel", "parallel")),
        name="ssm_out",
    )(yf, yb, xbc, z, dsk, nw, w, xs, modl, norm_post)


def _rope_tables(n_lat_tok, n_ctx_tok):
    t = np.arange(n_lat_tok)
    n_freq = HEAD_DIM // 4
    inv = 1.0 / (ROPE_THETA ** (jnp.arange(n_freq, dtype=F32) / n_freq))
    rowp = jnp.asarray(t // GRID_W, F32)
    colp = jnp.asarray(t % GRID_W, F32)
    ang = jnp.concatenate([rowp[:, None] * inv, colp[:, None] * inv], axis=-1)
    cos, sin = jnp.cos(ang), jnp.sin(ang)
    cos2 = jnp.concatenate([cos, cos], axis=-1)
    sin2 = jnp.concatenate([-sin, sin], axis=-1)
    cos2 = jnp.concatenate([cos2, jnp.ones((n_ctx_tok, HEAD_DIM), F32)], axis=0)
    sin2 = jnp.concatenate([sin2, jnp.zeros((n_ctx_tok, HEAD_DIM), F32)], axis=0)
    return cos2, sin2


_DEINT = np.concatenate([np.arange(0, HEAD_DIM, 2), np.arange(1, HEAD_DIM, 2)])


def _attn_weight_columns():
    qa, ka, va, ga, qb, kb, vb, gb = 0, 1024, 1280, 1536, 2560, 3584, 3840, 4096

    def heads(start, n, perm):
        base = start + HEAD_DIM * np.arange(n)[:, None]
        return (base + (_DEINT if perm else np.arange(HEAD_DIM))[None, :]).reshape(-1)

    return np.concatenate([
        heads(qa, 8, True), heads(ga, 8, False), heads(qb, 8, True), heads(gb, 8, False),
        heads(ka, 2, True), heads(va, 2, False), heads(kb, 2, True), heads(vb, 2, False)])


_ATTN_COLS_IDX = _attn_weight_columns()


def kernel(x, c, ctx, c_ctx, w_ada, b_ada, norm_pre, norm_post, attn_w_in, attn_w_out, attn_sink,
           attn_q_norm, attn_k_norm, ssm_w_in, ssm_conv_w, ssm_conv_b, ssm_dt_bias, ssm_a_log, ssm_d,
           ssm_norm, ssm_w_out):
    bsz, n_lat_tok, d = x.shape
    n_ctx_tok = ctx.shape[1]
    depth = w_ada.shape[0]
    assert n_ctx_tok == TM and n_lat_tok % TM == 0 and bsz <= 7
    d_inner = SSM_HEADS * SSM_HEAD_DIM
    bc_w = 2 * SSM_GROUPS * D_STATE

    xs = jnp.concatenate([x, ctx], axis=1)
    cc = jnp.zeros((8, d), F32).at[:bsz].set(c).at[bsz].set(c_ctx)
    mod = _modulation(cc, w_ada, b_ada)
    mod = mod.reshape(depth, 8, 3, d)
    cos2, sin2 = _rope_tables(n_lat_tok, n_ctx_tok)
    eye_rep = np.repeat(np.eye(SSM_HEADS, dtype=np.float32), SSM_HEAD_DIM, axis=1)
    expand_f = jnp.asarray(np.pad(eye_rep, ((0, HPAD - SSM_HEADS), (0, 0))))
    expand_b = jnp.asarray(np.pad(eye_rep, ((SSM_HEADS, HPAD - 2 * SSM_HEADS), (0, 0))))

    for l in range(depth):
        last = l == depth - 1
        n_tiles = (n_lat_tok if last else n_lat_tok + n_ctx_tok) // TM
        modl = jnp.stack([mod[l, :bsz], jnp.broadcast_to(mod[l, bsz], (bsz, 3, d))], axis=1)
        npre = norm_pre[l].reshape(1, d)
        npost = norm_post[l].reshape(1, d)
        i = l // 2
        if l % 2 == 0:
            w = attn_w_in[i][:, _ATTN_COLS_IDX].astype(BF16)
            qn = attn_q_norm[i][_DEINT].reshape(1, HEAD_DIM)
            kn = attn_k_norm[i][_DEINT].reshape(1, HEAD_DIM)
            qkv = _attn_in(xs, modl, npre, w, cos2, sin2, qn, kn)
            sink_b = jnp.broadcast_to(attn_sink[i][:, None], (A_Q_HEADS, HEAD_DIM))
            ya = _attn_a(qkv, sink_b, n_lat_tok)
            yb = _attn_b(qkv, n_lat_tok)
            wo = attn_w_out[i].astype(BF16)
            aq = A_Q_HEADS * HEAD_DIM
            xs = _attn_out(ya, yb, wo[:aq], wo[aq:], xs, modl, npost, n_tiles)
        else:
            w = ssm_w_in[i]
            wz = w[:, :d_inner].astype(BF16)
            wx = w[:, d_inner:2 * d_inner + bc_w].astype(BF16)
            wdt = jnp.pad(w[:, 2 * d_inner + bc_w:], ((0, 0), (0, HPAD - 2 * SSM_HEADS))).astype(BF16)
            dtb = jnp.pad(ssm_dt_bias[i].reshape(1, -1), ((0, 0), (0, HPAD - 2 * SSM_HEADS)))
            z, xbc, dt, dtt = _ssm_in(xs, modl, npre, wz, wx, wdt, wdt.T, ssm_conv_w[i],
                                      ssm_conv_b[i].reshape(1, -1), dtb, dtb.reshape(-1, 1))
            alog = jnp.pad(ssm_a_log[i].reshape(1, -1), ((0, 0), (0, HPAD - 2 * SSM_HEADS)))
            yf, ybk = _ssd(xbc, dt, dtt, alog, alog.reshape(-1, 1), expand_f, expand_b, n_lat_tok)
            dsk = jnp.repeat(ssm_d[i], SSM_HEAD_DIM).reshape(1, d_inner)
            xs = _ssm_out(yf, ybk, xbc, z, dsk, ssm_norm[i].reshape(1, d_inner),
                          ssm_w_out[i].astype(BF16), xs, modl, npost, n_tiles)
    return xs
```

```python
import functools

import numpy as np
import jax
import jax.numpy as jnp
from jax import lax
from jax.experimental import pallas as pl
from jax.experimental.pallas import tpu as pltpu

F32 = jnp.float32
BF16 = jnp.bfloat16

EPS = 1e-6
GRID_W = 64
ROPE_THETA = 10000.0
HEAD_DIM = 128
A_Q_HEADS = 8
A_KV_HEADS = 2
B_Q_HEADS = 8
B_KV_HEADS = 2
REP = 4
WINDOW = 128
SSM_HEAD_DIM = 64
SSM_HEADS = 32
SSM_GROUPS = 8
SSM_REP = SSM_HEADS // SSM_GROUPS
D_STATE = 128
SSM_CHUNK = 128
HPAD = 128

V7X_VMEM_BYTES = 64 * 1024 * 1024
VMEM_LIMIT = V7X_VMEM_BYTES - 8 * 1024 * 1024

TM = 256
TQ_A = 128
TQ_B = 256
TK_B = 512
LANES = 128
LOG2E = 1.4426950408889634

QA0, GA0, QB0, GB0, KA0, VA0, KB0, VB0 = 0, 1024, 2048, 3072, 4096, 4352, 4608, 4864
ATTN_COLS = 5120

NEG = -1e30


def _params(sem, vmem=VMEM_LIMIT):
    return pltpu.CompilerParams(dimension_semantics=sem, vmem_limit_bytes=vmem)


def _silu(t):
    return t * (1.0 / (1.0 + jnp.exp(-t)))


def _rms(t, w):
    return t * lax.rsqrt(jnp.mean(t * t, axis=-1, keepdims=True) + EPS) * w


def _dot(a, b):
    return jnp.dot(a, b, preferred_element_type=F32)


def _dot_nt(a, b):
    return lax.dot_general(a, b, (((1,), (1,)), ((), ())), preferred_element_type=F32)


def _dot_hi(a, b):
    return jnp.dot(a, b, preferred_element_type=F32, precision=lax.Precision.HIGHEST)


def _mod_kernel(cc_ref, w_ref, b_ref, o_ref):
    o_ref[...] = _dot_hi(_silu(cc_ref[...]), w_ref[...]) + b_ref[...]


def _modulation(cc, w_ada, b_ada):
    depth, d, d3 = w_ada.shape
    return pl.pallas_call(
        _mod_kernel,
        grid=(depth,),
        in_specs=[
            pl.BlockSpec((8, d), lambda l: (0, 0)),
            pl.BlockSpec((None, d, d3), lambda l: (l, 0, 0)),
            pl.BlockSpec((None, 1, d3), lambda l: (l, 0, 0)),
        ],
        out_specs=pl.BlockSpec((None, 8, d3), lambda l: (l, 0, 0)),
        out_shape=jax.ShapeDtypeStruct((depth, 8, d3), F32),
        compiler_params=_params(("arbitrary",)),
        name="modulation",
    )(cc, w_ada, b_ada.reshape(depth, 1, d3))


def _pre_norm(x, mod, w):
    return _rms(x, w) * (1.0 + mod[1:2]) + mod[0:1]


def _attn_in_kernel(x_ref, mod_ref, np_ref, w_ref, cos_ref, sin_ref, qn_ref, kn_ref, o_ref):
    hb = _pre_norm(x_ref[...], mod_ref[...], np_ref[...]).astype(BF16)
    cos = cos_ref[...]
    sin = sin_ref[...]
    scale = HEAD_DIM ** -0.5

    def rope(t):
        return t * cos + pltpu.roll(t, HEAD_DIM // 2, 1) * sin

    nblk = 512
    for j in range(ATTN_COLS // nblk):
        c0 = j * nblk
        t = _dot(hb, w_ref[:, c0:c0 + nblk])
        for hh in range(nblk // HEAD_DIM):
            col = c0 + hh * HEAD_DIM
            th = t[:, hh * HEAD_DIM:(hh + 1) * HEAD_DIM]
            if col < GA0:
                th = rope(th) * scale
            elif col < QB0 or GB0 <= col < KA0:
                th = _silu(th)
            elif col < GB0:
                th = rope(_rms(th, qn_ref[...])) * (scale * LOG2E)
            elif col < VA0:
                th = rope(th)
            elif KB0 <= col < VB0:
                th = rope(_rms(th, kn_ref[...]))
            o_ref[:, col:col + HEAD_DIM] = th.astype(BF16)


def _attn_in(xs, modl, norm_pre, w, cos2, sin2, qn, kn):
    b, s, d = xs.shape
    n_t = s // TM
    n_lat = n_t - 1
    row = lambda bi, i: (bi, i, 0)
    return pl.pallas_call(
        _attn_in_kernel,
        grid=(b, n_t),
        in_specs=[
            pl.BlockSpec((None, TM, d), row),
            pl.BlockSpec((None, None, 3, d), lambda bi, i: (bi, i // n_lat, 0, 0)),
            pl.BlockSpec((1, d), lambda bi, i: (0, 0)),
            pl.BlockSpec((d, ATTN_COLS), lambda bi, i: (0, 0)),
            pl.BlockSpec((TM, HEAD_DIM), lambda bi, i: (i, 0)),
            pl.BlockSpec((TM, HEAD_DIM), lambda bi, i: (i, 0)),
            pl.BlockSpec((1, HEAD_DIM), lambda bi, i: (0, 0)),
            pl.BlockSpec((1, HEAD_DIM), lambda bi, i: (0, 0)),
        ],
        out_specs=pl.BlockSpec((None, TM, ATTN_COLS), row),
        out_shape=jax.ShapeDtypeStruct((b, s, ATTN_COLS), BF16),
        compiler_params=_params(("parallel", "parallel")),
        name="attn_in",
    )(xs, modl, norm_pre, w, cos2, sin2, qn, kn)


def _attn_a_kernel(q_ref, kp_ref, kc_ref, kn_ref, vp_ref, vc_ref, vn_ref, kx_ref, vx_ref,
                   g_ref, sink_ref, o_ref, *, n_lat):
    i = pl.program_id(1)
    tq = TQ_A
    rows = REP * tq
    r = lax.broadcasted_iota(jnp.int32, (rows, tq), 0) & (tq - 1)
    c = lax.broadcasted_iota(jnp.int32, (rows, tq), 1)
    is_lat = i < n_lat
    lim_prev = jnp.where(jnp.logical_and(i > 0, is_lat), 0, tq)
    lim_next = jnp.where(jnp.logical_and(i < n_lat - 1, is_lat), 0, tq)
    m_prev = (c - r) >= lim_prev
    m_next = (r - c) >= lim_next
    pen_cur = jnp.where(is_lat, 0.0, NEG)
    for g in range(A_KV_HEADS):
        gs = slice(g * HEAD_DIM, (g + 1) * HEAD_DIM)
        q = jnp.concatenate(
            [q_ref[:, (g * REP + h) * HEAD_DIM:(g * REP + h + 1) * HEAD_DIM] for h in range(REP)], axis=0)
        sp = jnp.where(m_prev, _dot_nt(q, kp_ref[:, gs]), NEG)
        sc = _dot_nt(q, kc_ref[:, gs]) + pen_cur
        sn = jnp.where(m_next, _dot_nt(q, kn_ref[:, gs]), NEG)
        sx = _dot_nt(q, kx_ref[:, gs])
        sk = jnp.concatenate(
            [jnp.broadcast_to(sink_ref[g * REP + h:g * REP + h + 1, 0:1], (tq, 1)) for h in range(REP)], axis=0)
        sx0, sx1 = sx[:, 0:tq], sx[:, tq:2 * tq]
        m_t = jnp.maximum(jnp.maximum(jnp.maximum(sp, sc), jnp.maximum(sn, sx0)), sx1)
        m = jnp.maximum(jnp.max(m_t, axis=1, keepdims=True), sk)
        pp = jnp.exp(sp - m)
        pc = jnp.exp(sc - m)
        pn = jnp.exp(sn - m)
        px = jnp.exp(sx - m)
        den_t = (pp + pc) + (pn + px[:, 0:tq]) + px[:, tq:2 * tq]
        den = jnp.sum(den_t, axis=1, keepdims=True) + jnp.exp(sk - m)
        o = (_dot(pp.astype(BF16), vp_ref[:, gs]) + _dot(pc.astype(BF16), vc_ref[:, gs])
             + _dot(pn.astype(BF16), vn_ref[:, gs]) + _dot(px.astype(BF16), vx_ref[:, gs])) * (1.0 / den)
        for h in range(REP):
            cs = slice((g * REP + h) * HEAD_DIM, (g * REP + h + 1) * HEAD_DIM)
            o_ref[:, cs] = (o[h * tq:(h + 1) * tq] * g_ref[:, cs].astype(F32)).astype(BF16)


def _attn_a(qkv, sink_b, n_lat_tok):
    b, s, _ = qkv.shape
    tq = TQ_A
    n_blk = s // tq
    n_lat = n_lat_tok // tq
    kvw = A_KV_HEADS * HEAD_DIM
    ctx_len = s - n_lat_tok
    assert ctx_len == 2 * tq and n_lat_tok % ctx_len == 0
    qw = A_Q_HEADS * HEAD_DIM
    prev = lambda cb: (lambda bi, i: (bi, jnp.maximum(i - 1, 0), cb))
    cur = lambda cb: (lambda bi, i: (bi, i, cb))
    nxt = lambda cb: (lambda bi, i: (bi, jnp.minimum(i + 1, n_blk - 1), cb))
    ctx = lambda cb: (lambda bi, i: (bi, n_lat_tok // ctx_len, cb))
    kb, vb = KA0 // kvw, VA0 // kvw
    return pl.pallas_call(
        functools.partial(_attn_a_kernel, n_lat=n_lat),
        grid=(b, n_blk),
        in_specs=[
            pl.BlockSpec((None, tq, qw), cur(QA0 // qw)),
            pl.BlockSpec((None, tq, kvw), prev(kb)),
            pl.BlockSpec((None, tq, kvw), cur(kb)),
            pl.BlockSpec((None, tq, kvw), nxt(kb)),
            pl.BlockSpec((None, tq, kvw), prev(vb)),
            pl.BlockSpec((None, tq, kvw), cur(vb)),
            pl.BlockSpec((None, tq, kvw), nxt(vb)),
            pl.BlockSpec((None, ctx_len, kvw), ctx(kb)),
            pl.BlockSpec((None, ctx_len, kvw), ctx(vb)),
            pl.BlockSpec((None, tq, qw), cur(GA0 // qw)),
            pl.BlockSpec((A_Q_HEADS, HEAD_DIM), lambda bi, i: (0, 0)),
        ],
        out_specs=pl.BlockSpec((None, tq, qw), lambda bi, i: (bi, i, 0)),
        out_shape=jax.ShapeDtypeStruct((b, s, qw), BF16),
        compiler_params=_params(("parallel", "parallel")),
        name="attn_window",
    )(qkv, qkv, qkv, qkv, qkv, qkv, qkv, qkv, qkv, qkv, sink_b)


def _attn_b_kernel(q_ref, k_ref, v_ref, g_ref, o_ref, m_sc, l_sc, acc_sc, p_sc, alpha_sc, *, n_lat, n_lat_tok):
    i = pl.program_id(2)
    tq, tk = TQ_B, TK_B
    n_ctx_tok = k_ref.shape[0] - n_lat_tok
    q = jnp.concatenate([q_ref[:, h * HEAD_DIM:(h + 1) * HEAD_DIM] for h in range(REP)], axis=0)
    m_sc[...] = jnp.full(m_sc.shape, NEG, F32)
    l_sc[...] = jnp.zeros(l_sc.shape, F32)
    acc_sc[...] = jnp.zeros(acc_sc.shape, F32)

    rb = 128

    def scores(k, slot):
        nk = k.shape[0]
        s = _dot_nt(q, k)
        for r0 in range(0, REP * tq, rb):
            rs = slice(r0, r0 + rb)
            s_b = s[rs]
            m_old = m_sc[rs]
            m_new = jnp.maximum(m_old, jnp.max(s_b, axis=1, keepdims=True))
            alpha = jnp.exp2(m_old - m_new)
            p = jnp.exp2(s_b - jnp.tile(m_new, (1, nk // LANES)))
            l_sc[rs] = alpha * l_sc[rs] + jnp.sum(p, axis=1, keepdims=True)
            m_sc[rs] = m_new
            alpha_sc[slot, rs] = alpha
            p_sc[slot, rs, 0:nk] = p.astype(BF16)

    def accumulate(slot, v):
        nk = v.shape[0]
        acc_sc[...] = alpha_sc[slot] * acc_sc[...] + _dot(p_sc[slot, :, 0:nk], v)

    def kv(ref, ci):
        if isinstance(ci, int):
            return ref[ci * tk:(ci + 1) * tk, :]
        return ref[pl.ds(pl.multiple_of(ci * tk, tk), tk), :]

    n_main = n_lat_tok // tk

    assert n_main % 2 == 0

    @pl.when(i < n_lat)
    def _():
        scores(kv(k_ref, 0), 0)

        def body(j, carry):
            c1 = 2 * j + 1
            scores(kv(k_ref, c1), 1)
            accumulate(0, kv(v_ref, c1 - 1))
            scores(kv(k_ref, c1 + 1), 0)
            accumulate(1, kv(v_ref, c1))
            return carry

        lax.fori_loop(0, n_main // 2 - 1, body, 0)
        scores(kv(k_ref, n_main - 1), 1)
        accumulate(0, kv(v_ref, n_main - 2))
        accumulate(1, kv(v_ref, n_main - 1))

    scores(k_ref[n_lat_tok:n_lat_tok + n_ctx_tok, :], 0)
    accumulate(0, v_ref[n_lat_tok:n_lat_tok + n_ctx_tok, :])
    o = acc_sc[...] * (1.0 / l_sc[...])
    for h in range(REP):
        cs = slice(h * HEAD_DIM, (h + 1) * HEAD_DIM)
        o_ref[:, cs] = (o[h * tq:(h + 1) * tq] * g_ref[:, cs].astype(F32)).astype(BF16)


def _attn_b(qkv, n_lat_tok):
    b, s, _ = qkv.shape
    tq = TQ_B
    n_t = s // tq
    n_lat = n_lat_tok // tq
    gw = REP * HEAD_DIM
    return pl.pallas_call(
        functools.partial(_attn_b_kernel, n_lat=n_lat, n_lat_tok=n_lat_tok),
        grid=(b, B_KV_HEADS, n_t),
        in_specs=[
            pl.BlockSpec((None, tq, gw), lambda bi, g, i: (bi, i, QB0 // gw + g)),
            pl.BlockSpec((None, s, HEAD_DIM), lambda bi, g, i: (bi, 0, KB0 // HEAD_DIM + g)),
            pl.BlockSpec((None, s, HEAD_DIM), lambda bi, g, i: (bi, 0, VB0 // HEAD_DIM + g)),
            pl.BlockSpec((None, tq, gw), lambda bi, g, i: (bi, i, GB0 // gw + g)),
        ],
        out_specs=pl.BlockSpec((None, tq, gw), lambda bi, g, i: (bi, i, g)),
        out_shape=jax.ShapeDtypeStruct((b, s, B_Q_HEADS * HEAD_DIM), BF16),
        scratch_shapes=[
            pltpu.VMEM((REP * tq, LANES), F32),
            pltpu.VMEM((REP * tq, LANES), F32),
            pltpu.VMEM((REP * tq, HEAD_DIM), F32),
            pltpu.VMEM((2, REP * tq, TK_B), BF16),
            pltpu.VMEM((2, REP * tq, LANES), F32),
        ],
        compiler_params=_params(("parallel", "parallel", "parallel")),
        name="attn_dense",
    )(qkv, qkv, qkv, qkv)


def _residual(x, y, mod, w_post):
    return x + mod[2:3] * _rms(y, w_post)


def _attn_out_kernel(ya_ref, yb_ref, wa_ref, wb_ref, x_ref, mod_ref, np_ref, o_ref):
    y = _dot(ya_ref[...], wa_ref[...]) + _dot(yb_ref[...], wb_ref[...])
    o_ref[...] = _residual(x_ref[...], y, mod_ref[...], np_ref[...])


def _attn_out(ya, yb, wa, wb, xs, modl, norm_post, n_tiles):
    b, s, d = xs.shape
    n_lat = s // TM - 1
    row = lambda bi, i: (bi, i, 0)
    const = lambda bi, i: (0, 0)
    return pl.pallas_call(
        _attn_out_kernel,
        grid=(b, n_tiles),
        in_specs=[
            pl.BlockSpec((None, TM, ya.shape[-1]), row),
            pl.BlockSpec((None, TM, yb.shape[-1]), row),
            pl.BlockSpec(wa.shape, const),
            pl.BlockSpec(wb.shape, const),
            pl.BlockSpec((None, TM, d), row),
            pl.BlockSpec((None, None, 3, d), lambda bi, i: (bi, i // n_lat, 0, 0)),
            pl.BlockSpec((1, d), const),
        ],
        out_specs=pl.BlockSpec((None, TM, d), row),
        out_shape=jax.ShapeDtypeStruct((b, n_tiles * TM, d), F32),
        compiler_params=_params(("parallel", "parallel")),
        name="attn_out",
    )(ya, yb, wa, wb, xs, modl, norm_post)


def _softplus(t):
    return jnp.maximum(t, 0.0) + jnp.log(1.0 + jnp.exp(-jnp.abs(t)))


def _ssm_in_kernel(x_ref, xp_ref, xn_ref, mod_ref, np_ref, wz_ref, wx_ref, wdt_ref, wdtt_ref,
                   cw_ref, cb_ref, dtb_ref, dtbt_ref, z_ref, xbc_ref, dt_ref, dtt_ref, *, n_lat):
    i = pl.program_id(1)
    tm = TM
    mod = mod_ref[...]
    w_pre = np_ref[...]
    hb = _pre_norm(x_ref[...], mod, w_pre).astype(BF16)
    is_ctx = i >= n_lat
    has_prev = jnp.logical_and(i > 0, jnp.logical_not(is_ctx)).astype(F32)
    has_next = jnp.logical_and(i < n_lat - 1, jnp.logical_not(is_ctx)).astype(F32)
    hp = (_pre_norm(xp_ref[...], mod, w_pre) * has_prev).astype(BF16)
    hn = (_pre_norm(xn_ref[...], mod, w_pre) * has_next).astype(BF16)
    ext = jnp.concatenate([hp, hb, hn], axis=0)

    z_ref[...] = _silu(_dot(hb, wz_ref[...])).astype(BF16)
    dt_ref[...] = _softplus(_dot(hb, wdt_ref[...]) + dtb_ref[...])
    dtt_ref[...] = _softplus(_dot_nt(wdtt_ref[...], hb) + dtbt_ref[...])

    nblk = 512
    nt = tm // 8
    sub = lax.broadcasted_iota(jnp.int32, (8, nblk), 0)
    for j in range(wx_ref.shape[1] // nblk):
        cs = slice(j * nblk, (j + 1) * nblk)
        u3 = _dot(ext, wx_ref[:, cs]).reshape(nt + 2, 8, nblk)
        dn = pltpu.roll(u3, 1, 1)
        upw = pltpu.roll(u3, 7, 1)
        u_prev = jnp.where(sub == 0, dn[0:nt], dn[1:nt + 1])
        u_next = jnp.where(sub == 7, upw[2:nt + 2], upw[1:nt + 1])
        conv = (cb_ref[:, cs] + cw_ref[0:1, cs] * u_prev + cw_ref[1:2, cs] * u3[1:nt + 1]
                + cw_ref[2:3, cs] * u_next)
        xbc_ref[:, cs] = _silu(conv).reshape(tm, nblk).astype(BF16)


def _ssm_in(xs, modl, norm_pre, wz, wx, wdt, wdtt, cw, cb, dtb, dtbt):
    b, s, d = xs.shape
    n_t = s // TM
    n_lat = n_t - 1
    row = lambda bi, i: (bi, i, 0)
    const = lambda bi, i: (0, 0)
    r8 = TM // 8
    n8 = s // 8
    nh2 = wdt.shape[1]
    return pl.pallas_call(
        functools.partial(_ssm_in_kernel, n_lat=n_lat),
        grid=(b, n_t),
        in_specs=[
            pl.BlockSpec((None, TM, d), row),
            pl.BlockSpec((None, 8, d), lambda bi, i: (bi, jnp.maximum(i * r8 - 1, 0), 0)),
            pl.BlockSpec((None, 8, d), lambda bi, i: (bi, jnp.minimum((i + 1) * r8, n8 - 1), 0)),
            pl.BlockSpec((None, None, 3, d), lambda bi, i: (bi, i // n_lat, 0, 0)),
            pl.BlockSpec((1, d), const),
            pl.BlockSpec(wz.shape, const),
            pl.BlockSpec(wx.shape, const),
            pl.BlockSpec(wdt.shape, const),
            pl.BlockSpec(wdtt.shape, const),
            pl.BlockSpec(cw.shape, const),
            pl.BlockSpec(cb.shape, const),
            pl.BlockSpec(dtb.shape, const),
            pl.BlockSpec(dtbt.shape, const),
        ],
        out_specs=[
            pl.BlockSpec((None, TM, wz.shape[1]), row),
            pl.BlockSpec((None, TM, wx.shape[1]), row),
            pl.BlockSpec((None, TM, nh2), row),
            pl.BlockSpec((None, nh2, TM), lambda bi, i: (bi, 0, i)),
        ],
        out_shape=[
            jax.ShapeDtypeStruct((b, s, wz.shape[1]), BF16),
            jax.ShapeDtypeStruct((b, s, wx.shape[1]), BF16),
            jax.ShapeDtypeStruct((b, s, nh2), F32),
            jax.ShapeDtypeStruct((b, nh2, s), F32),
        ],
        compiler_params=_params(("parallel", "parallel")),
        name="ssm_in",
    )(xs, xs, xs, modl, norm_pre, wz, wx, wdt, wdtt, cw, cb, dtb, dtbt)


def _split3(t):
    hi = t.astype(BF16)
    r1 = t - hi.astype(F32)
    mid = r1.astype(BF16)
    lo = (r1 - mid.astype(F32)).astype(BF16)
    return hi, mid, lo


def _ones_dot_lhs(tri01, a):
    return _dot(jnp.concatenate([tri01] * 3, axis=1), jnp.concatenate(_split3(a), axis=0))


def _ones_dot_rhs(at, tri01):
    return _dot(jnp.concatenate(_split3(at), axis=1), jnp.concatenate([tri01] * 3, axis=0))


def _ssd_direction(xbc_ref, dt, dtt, a_row, a_col, h_sc, y_ref, reverse, hoff):
    q_len = SSM_CHUNK
    d_inner = SSM_HEADS * SSM_HEAD_DIM
    gw = SSM_REP * SSM_HEAD_DIM
    row = lax.broadcasted_iota(jnp.int32, (q_len, q_len), 0)
    col = lax.broadcasted_iota(jnp.int32, (q_len, q_len), 1)
    lower = row >= col
    upper = row <= col
    mask = upper if reverse else lower
    tri = jnp.where(mask, 1.0, 0.0).astype(BF16)
    tri_t = jnp.where(lower if reverse else upper, 1.0, 0.0).astype(BF16)

    a = dt * a_row
    at = dtt * a_col
    acum = _ones_dot_lhs(tri, a)
    acum_t = _ones_dot_rhs(at, tri_t)
    total_t = jnp.sum(at, axis=1, keepdims=True)
    w_t = jnp.exp(total_t - acum_t) * dtt
    etot = jnp.exp(jnp.sum(a, axis=0, keepdims=True))
    head_of_col = jnp.right_shift(lax.broadcasted_iota(jnp.int32, (q_len, gw), 1),
                                  SSM_HEAD_DIM.bit_length() - 1)

    for g in range(SSM_GROUPS):
        b_g = xbc_ref[:, d_inner + g * D_STATE:d_inner + (g + 1) * D_STATE]
        c_g = xbc_ref[:, d_inner + (SSM_GROUPS + g) * D_STATE:d_inner + (SSM_GROUPS + g + 1) * D_STATE]
        gsl = slice(g * gw, (g + 1) * gw)
        x_g = xbc_ref[:, gsl]
        cb = _dot_nt(c_g, b_g)
        c_f = c_g.astype(F32)
        b_t = b_g.astype(F32).T
        h_g = h_sc[g]
        h_b = h_g.astype(BF16)
        ms, ces, bws, bdx, bdh = [], [], [], [], []
        etot_row = None
        for r in range(SSM_REP):
            h = hoff + g * SSM_REP + r
            acb = jnp.broadcast_to(acum[:, h:h + 1], (q_len, q_len))
            decay = jnp.exp(jnp.where(mask, acb - acum_t[h:h + 1, :], NEG))
            ms.append((cb * decay * dtt[h:h + 1, :]).astype(BF16))
            ces.append((c_f * jnp.exp(acb)).astype(BF16))
            bws.append((b_t * w_t[h:h + 1, :]).astype(BF16))
            sel = head_of_col == r
            bdx.append(jnp.where(sel, x_g, jnp.zeros_like(x_g)))
            bdh.append(jnp.where(sel, h_b, jnp.zeros_like(h_b)))
            e_r = jnp.broadcast_to(etot[:, h:h + 1], (1, gw))
            etot_row = e_r if etot_row is None else jnp.where(head_of_col[0:1] == r, e_r, etot_row)
        bdx = jnp.concatenate(bdx, axis=0)
        bdh = jnp.concatenate(bdh, axis=0)
        lhs = jnp.concatenate([jnp.concatenate(ms, axis=1), jnp.concatenate(bws, axis=1)], axis=0)
        res = _dot(lhs, bdx)
        y = res[0:q_len] + _dot(jnp.concatenate(ces, axis=1), bdh)
        y_ref[:, gsl] = y.astype(BF16)
        h_sc[g] = h_g * etot_row + res[q_len:]


def _ssd_kernel(xf_ref, xb_ref, dtf_ref, dtb_ref, dttf_ref, dttb_ref, alog_ref, alogt_ref,
                yf_ref, yb_ref, hf_sc, hb_sc):
    @pl.when(pl.program_id(1) == 0)
    def _():
        hf_sc[...] = jnp.zeros(hf_sc.shape, F32)
        hb_sc[...] = jnp.zeros(hb_sc.shape, F32)

    a_row = -jnp.exp(alog_ref[...])
    a_col = -jnp.exp(alogt_ref[...])
    _ssd_direction(xf_ref, dtf_ref[...], dttf_ref[...], a_row, a_col, hf_sc, yf_ref, False, 0)
    _ssd_direction(xb_ref, dtb_ref[...], dttb_ref[...], a_row, a_col, hb_sc, yb_ref, True, SSM_HEADS)


def _ssd(xbc, dt, dtt, alog, alogt, n_lat_tok):
    b, s, cw = xbc.shape
    q_len = SSM_CHUNK
    n_c = s // q_len
    n_lat = n_lat_tok // q_len
    d_inner = SSM_HEADS * SSM_HEAD_DIM
    nh2 = dt.shape[-1]
    cf = lambda j: (j + n_lat) % n_c
    cbk = lambda j: n_c - 1 - j
    const = lambda bi, j: (0, 0)
    return pl.pallas_call(
        _ssd_kernel,
        grid=(b, n_c),
        in_specs=[
            pl.BlockSpec((None, q_len, cw), lambda bi, j: (bi, cf(j), 0)),
            pl.BlockSpec((None, q_len, cw), lambda bi, j: (bi, cbk(j), 0)),
            pl.BlockSpec((None, q_len, nh2), lambda bi, j: (bi, cf(j), 0)),
            pl.BlockSpec((None, q_len, nh2), lambda bi, j: (bi, cbk(j), 0)),
            pl.BlockSpec((None, nh2, q_len), lambda bi, j: (bi, 0, cf(j))),
            pl.BlockSpec((None, nh2, q_len), lambda bi, j: (bi, 0, cbk(j))),
            pl.BlockSpec(alog.shape, const),
            pl.BlockSpec(alogt.shape, const),
        ],
        out_specs=[
            pl.BlockSpec((None, q_len, d_inner), lambda bi, j: (bi, cf(j), 0)),
            pl.BlockSpec((None, q_len, d_inner), lambda bi, j: (bi, cbk(j), 0)),
        ],
        out_shape=[jax.ShapeDtypeStruct((b, s, d_inner), BF16)] * 2,
        scratch_shapes=[pltpu.VMEM((SSM_GROUPS, D_STATE, SSM_REP * SSM_HEAD_DIM), F32)] * 2,
        compiler_params=_params(("parallel", "arbitrary")),
        name="ssd_scan",
    )(xbc, xbc, dt, dt, dtt, dtt, alog, alogt)


def _ssm_out_kernel(yf_ref, yb_ref, xs_ref, z_ref, dsk_ref, nw_ref, w_ref, x_ref, mod_ref, np_ref, o_ref):
    y = yf_ref[...].astype(F32) + yb_ref[...].astype(F32) + dsk_ref[...] * xs_ref[...].astype(F32)
    gated = y * z_ref[...].astype(F32)
    gsz = gated.shape[1] // SSM_GROUPS
    parts = []
    for g in range(SSM_GROUPS):
        t = gated[:, g * gsz:(g + 1) * gsz]
        parts.append(t * lax.rsqrt(jnp.mean(t * t, axis=-1, keepdims=True) + EPS))
    gn = (jnp.concatenate(parts, axis=1) * nw_ref[...]).astype(BF16)
    o_ref[...] = _residual(x_ref[...], _dot(gn, w_ref[...]), mod_ref[...], np_ref[...])


def _ssm_out(yf, yb, xbc, z, dsk, nw, w, xs, modl, norm_post, n_tiles):
    b, s, d = xs.shape
    n_lat = s // TM - 1
    di = yf.shape[-1]
    row = lambda bi, i: (bi, i, 0)
    const = lambda bi, i: (0, 0)
    return pl.pallas_call(
        _ssm_out_kernel,
        grid=(b, n_tiles),
        in_specs=[
            pl.BlockSpec((None, TM, di), row),
            pl.BlockSpec((None, TM, di), row),
            pl.BlockSpec((None, TM, di), row),
            pl.BlockSpec((None, TM, di), row),
            pl.BlockSpec((1, di), const),
            pl.BlockSpec((1, di), const),
            pl.BlockSpec(w.shape, const),
            pl.BlockSpec((None, TM, d), row),
            pl.BlockSpec((None, None, 3, d), lambda bi, i: (bi, i // n_lat, 0, 0)),
            pl.BlockSpec((1, d), const),
        ],
        out_specs=pl.BlockSpec((None, TM, d), row),
        out_shape=jax.ShapeDtypeStruct((b, n_tiles * TM, d), F32),
        compiler_params=_params(("parallel", "parallel")),
        name="ssm_out",
    )(yf, yb, xbc, z, dsk, nw, w, xs, modl, norm_post)


def _rope_tables(n_lat_tok, n_ctx_tok):
    t = np.arange(n_lat_tok)
    n_freq = HEAD_DIM // 4
    inv = 1.0 / (ROPE_THETA ** (jnp.arange(n_freq, dtype=F32) / n_freq))
    rowp = jnp.asarray(t // GRID_W, F32)
    colp = jnp.asarray(t % GRID_W, F32)
    ang = jnp.concatenate([rowp[:, None] * inv, colp[:, None] * inv], axis=-1)
    cos, sin = jnp.cos(ang), jnp.sin(ang)
    cos2 = jnp.concatenate([cos, cos], axis=-1)
    sin2 = jnp.concatenate([-sin, sin], axis=-1)
    cos2 = jnp.concatenate([cos2, jnp.ones((n_ctx_tok, HEAD_DIM), F32)], axis=0)
    sin2 = jnp.concatenate([sin2, jnp.zeros((n_ctx_tok, HEAD_DIM), F32)], axis=0)
    return cos2, sin2


_DEINT = np.concatenate([np.arange(0, HEAD_DIM, 2), np.arange(1, HEAD_DIM, 2)])


def _attn_weight_columns():
    qa, ka, va, ga, qb, kb, vb, gb = 0, 1024, 1280, 1536, 2560, 3584, 3840, 4096

    def heads(start, n, perm):
        base = start + HEAD_DIM * np.arange(n)[:, None]
        return (base + (_DEINT if perm else np.arange(HEAD_DIM))[None, :]).reshape(-1)

    return np.concatenate([
        heads(qa, 8, True), heads(ga, 8, False), heads(qb, 8, True), heads(gb, 8, False),
        heads(ka, 2, True), heads(va, 2, False), heads(kb, 2, True), heads(vb, 2, False)])


_ATTN_COLS_IDX = _attn_weight_columns()


def kernel(x, c, ctx, c_ctx, w_ada, b_ada, norm_pre, norm_post, attn_w_in, attn_w_out, attn_sink,
           attn_q_norm, attn_k_norm, ssm_w_in, ssm_conv_w, ssm_conv_b, ssm_dt_bias, ssm_a_log, ssm_d,
           ssm_norm, ssm_w_out):
    bsz, n_lat_tok, d = x.shape
    n_ctx_tok = ctx.shape[1]
    depth = w_ada.shape[0]
    assert n_ctx_tok == TM and n_lat_tok % TM == 0 and bsz <= 7
    d_inner = SSM_HEADS * SSM_HEAD_DIM
    bc_w = 2 * SSM_GROUPS * D_STATE

    xs = jnp.concatenate([x, ctx], axis=1)
    cc = jnp.zeros((8, d), F32).at[:bsz].set(c).at[bsz].set(c_ctx)
    mod = _modulation(cc, w_ada, b_ada)
    mod = mod.reshape(depth, 8, 3, d)
    cos2, sin2 = _rope_tables(n_lat_tok, n_ctx_tok)

    for l in range(depth):
        last = l == depth - 1
        n_tiles = (n_lat_tok if last else n_lat_tok + n_ctx_tok) // TM
        modl = jnp.stack([mod[l, :bsz], jnp.broadcast_to(mod[l, bsz], (bsz, 3, d))], axis=1)
        npre = norm_pre[l].reshape(1, d)
        npost = norm_post[l].reshape(1, d)
        i = l // 2
        if l % 2 == 0:
            w = attn_w_in[i][:, _ATTN_COLS_IDX].astype(BF16)
            qn = attn_q_norm[i][_DEINT].reshape(1, HEAD_DIM)
            kn = attn_k_norm[i][_DEINT].reshape(1, HEAD_DIM)
            qkv = _attn_in(xs, modl, npre, w, cos2, sin2, qn, kn)
            sink_b = jnp.broadcast_to(attn_sink[i][:, None], (A_Q_HEADS, HEAD_DIM))
            ya = _attn_a(qkv, sink_b, n_lat_tok)
            yb = _attn_b(qkv, n_lat_tok)
            wo = attn_w_out[i].astype(BF16)
            aq = A_Q_HEADS * HEAD_DIM
            xs = _attn_out(ya, yb, wo[:aq], wo[aq:], xs, modl, npost, n_tiles)
        else:
            w = ssm_w_in[i]
            wz = w[:, :d_inner].astype(BF16)
            wx = w[:, d_inner:2 * d_inner + bc_w].astype(BF16)
            wdt = jnp.pad(w[:, 2 * d_inner + bc_w:], ((0, 0), (0, HPAD - 2 * SSM_HEADS))).astype(BF16)
            dtb = jnp.pad(ssm_dt_bias[i].reshape(1, -1), ((0, 0), (0, HPAD - 2 * SSM_HEADS)))
            z, xbc, dt, dtt = _ssm_in(xs, modl, npre, wz, wx, wdt, wdt.T, ssm_conv_w[i],
                                      ssm_conv_b[i].reshape(1, -1), dtb, dtb.reshape(-1, 1))
            alog = jnp.pad(ssm_a_log[i].reshape(1, -1), ((0, 0), (0, HPAD - 2 * SSM_HEADS)))
            yf, ybk = _ssd(xbc, dt, dtt, alog, alog.reshape(-1, 1), n_lat_tok)
            dsk = jnp.repeat(ssm_d[i], SSM_HEAD_DIM).reshape(1, d_inner)
            xs = _ssm_out(yf, ybk, xbc, z, dsk, ssm_norm[i].reshape(1, d_inner),
                          ssm_w_out[i].astype(BF16), xs, modl, npost, n_tiles)
    return xs
```

```python
import functools

import numpy as np
import jax
import jax.numpy as jnp
from jax import lax
from jax.experimental import pallas as pl
from jax.experimental.pallas import tpu as pltpu

F32 = jnp.float32
BF16 = jnp.bfloat16

EPS = 1e-6
GRID_W = 64
ROPE_THETA = 10000.0
HEAD_DIM = 128
A_Q_HEADS = 8
A_KV_HEADS = 2
B_Q_HEADS = 8
B_KV_HEADS = 2
REP = 4
WINDOW = 128
SSM_HEAD_DIM = 64
SSM_HEADS = 32
SSM_GROUPS = 8
SSM_REP = SSM_HEADS // SSM_GROUPS
D_STATE = 128
SSM_CHUNK = 128
HPAD = 128

V7X_VMEM_BYTES = 64 * 1024 * 1024
VMEM_LIMIT = V7X_VMEM_BYTES - 8 * 1024 * 1024

TM = 256
TQ_A = 128
TQ_B = 256
TK_B = 512
LANES = 128
LOG2E = 1.4426950408889634

QA0, GA0, QB0, GB0, KA0, VA0, KB0, VB0 = 0, 1024, 2048, 3072, 4096, 4352, 4608, 4864
ATTN_COLS = 5120

NEG = -1e30


def _params(sem, vmem=VMEM_LIMIT):
    return pltpu.CompilerParams(dimension_semantics=sem, vmem_limit_bytes=vmem)


def _silu(t):
    return t * (1.0 / (1.0 + jnp.exp2(t * (-LOG2E))))


def _rms(t, w):
    return t * lax.rsqrt(jnp.mean(t * t, axis=-1, keepdims=True) + EPS) * w


def _dot(a, b):
    return jnp.dot(a, b, preferred_element_type=F32)


def _dot_nt(a, b):
    return lax.dot_general(a, b, (((1,), (1,)), ((), ())), preferred_element_type=F32)


def _dot_hi(a, b):
    return jnp.dot(a, b, preferred_element_type=F32, precision=lax.Precision.HIGHEST)


def _mod_kernel(cc_ref, w_ref, b_ref, o_ref):
    o_ref[...] = _dot_hi(_silu(cc_ref[...]), w_ref[...]) + b_ref[...]


def _modulation(cc, w_ada, b_ada):
    depth, d, d3 = w_ada.shape
    return pl.pallas_call(
        _mod_kernel,
        grid=(depth,),
        in_specs=[
            pl.BlockSpec((8, d), lambda l: (0, 0)),
            pl.BlockSpec((None, d, d3), lambda l: (l, 0, 0)),
            pl.BlockSpec((None, 1, d3), lambda l: (l, 0, 0)),
        ],
        out_specs=pl.BlockSpec((None, 8, d3), lambda l: (l, 0, 0)),
        out_shape=jax.ShapeDtypeStruct((depth, 8, d3), F32),
        compiler_params=_params(("arbitrary",)),
        name="modulation",
    )(cc, w_ada, b_ada.reshape(depth, 1, d3))


def _pre_norm(x, mod, w):
    return _rms(x, w) * (1.0 + mod[1:2]) + mod[0:1]


def _attn_in_kernel(x_ref, mod_ref, np_ref, w_ref, cos_ref, sin_ref, qn_ref, kn_ref, o_ref):
    hb = _pre_norm(x_ref[...], mod_ref[...], np_ref[...]).astype(BF16)
    cos = cos_ref[...]
    sin = sin_ref[...]
    scale = HEAD_DIM ** -0.5

    def rope(t):
        return t * cos + pltpu.roll(t, HEAD_DIM // 2, 1) * sin

    nblk = 512
    for j in range(ATTN_COLS // nblk):
        c0 = j * nblk
        t = _dot(hb, w_ref[:, c0:c0 + nblk])
        for hh in range(nblk // HEAD_DIM):
            col = c0 + hh * HEAD_DIM
            th = t[:, hh * HEAD_DIM:(hh + 1) * HEAD_DIM]
            if col < GA0:
                th = rope(th) * (scale * LOG2E)
            elif col < QB0 or GB0 <= col < KA0:
                th = _silu(th)
            elif col < GB0:
                th = rope(_rms(th, qn_ref[...])) * (scale * LOG2E)
            elif col < VA0:
                th = rope(th)
            elif KB0 <= col < VB0:
                th = rope(_rms(th, kn_ref[...]))
            o_ref[:, col:col + HEAD_DIM] = th.astype(BF16)


def _attn_in(xs, modl, norm_pre, w, cos2, sin2, qn, kn):
    b, s, d = xs.shape
    n_t = s // TM
    n_lat = n_t - 1
    row = lambda bi, i: (bi, i, 0)
    return pl.pallas_call(
        _attn_in_kernel,
        grid=(b, n_t),
        in_specs=[
            pl.BlockSpec((None, TM, d), row),
            pl.BlockSpec((None, None, 3, d), lambda bi, i: (bi, i // n_lat, 0, 0)),
            pl.BlockSpec((1, d), lambda bi, i: (0, 0)),
            pl.BlockSpec((d, ATTN_COLS), lambda bi, i: (0, 0)),
            pl.BlockSpec((TM, HEAD_DIM), lambda bi, i: (i, 0)),
            pl.BlockSpec((TM, HEAD_DIM), lambda bi, i: (i, 0)),
            pl.BlockSpec((1, HEAD_DIM), lambda bi, i: (0, 0)),
            pl.BlockSpec((1, HEAD_DIM), lambda bi, i: (0, 0)),
        ],
        out_specs=pl.BlockSpec((None, TM, ATTN_COLS), row),
        out_shape=jax.ShapeDtypeStruct((b, s, ATTN_COLS), BF16),
        compiler_params=_params(("parallel", "parallel")),
        name="attn_in",
    )(xs, modl, norm_pre, w, cos2, sin2, qn, kn)


def _attn_a_kernel(q_ref, kp_ref, kc_ref, kn_ref, vp_ref, vc_ref, vn_ref, kx_ref, vx_ref,
                   g_ref, sink_ref, o_ref, *, n_lat):
    i = pl.program_id(1)
    tq = TQ_A
    rows = REP * tq
    r = lax.broadcasted_iota(jnp.int32, (rows, tq), 0) & (tq - 1)
    c = lax.broadcasted_iota(jnp.int32, (rows, tq), 1)
    is_lat = 2 * i < n_lat
    pen_cur = jnp.where(is_lat, 0.0, NEG)
    for a in range(2):
        blk = 2 * i + a
        rq = slice(a * tq, (a + 1) * tq)
        lim_prev = jnp.where(jnp.logical_and(blk > 0, is_lat), 0, tq)
        lim_next = jnp.where(jnp.logical_and(blk < n_lat - 1, is_lat), 0, tq)
        m_prev = (c - r) >= lim_prev
        m_next = (r - c) >= lim_next
        for g in range(A_KV_HEADS):
            gs = slice(g * HEAD_DIM, (g + 1) * HEAD_DIM)
            if a == 0:
                k_p, v_p = kp_ref[:, gs], vp_ref[:, gs]
                k_n, v_n = kc_ref[tq:2 * tq, gs], vc_ref[tq:2 * tq, gs]
            else:
                k_p, v_p = kc_ref[0:tq, gs], vc_ref[0:tq, gs]
                k_n, v_n = kn_ref[:, gs], vn_ref[:, gs]
            k_c, v_c = kc_ref[rq, gs], vc_ref[rq, gs]
            q = jnp.concatenate(
                [q_ref[rq, (g * REP + h) * HEAD_DIM:(g * REP + h + 1) * HEAD_DIM] for h in range(REP)], axis=0)
            sp = jnp.where(m_prev, _dot_nt(q, k_p), NEG)
            sc = _dot_nt(q, k_c) + pen_cur
            sn = jnp.where(m_next, _dot_nt(q, k_n), NEG)
            sx = _dot_nt(q, kx_ref[:, gs])
            sk = jnp.concatenate(
                [jnp.broadcast_to(sink_ref[g * REP + h:g * REP + h + 1, 0:1] * LOG2E, (tq, 1))
                 for h in range(REP)], axis=0)
            sx0, sx1 = sx[:, 0:tq], sx[:, tq:2 * tq]
            m_t = jnp.maximum(jnp.maximum(jnp.maximum(sp, sc), jnp.maximum(sn, sx0)), sx1)
            m = jnp.maximum(jnp.max(m_t, axis=1, keepdims=True), sk)
            pp = jnp.exp2(sp - m)
            pc = jnp.exp2(sc - m)
            pn = jnp.exp2(sn - m)
            px = jnp.exp2(sx - m)
            den_t = (pp + pc) + (pn + px[:, 0:tq]) + px[:, tq:2 * tq]
            den = jnp.sum(den_t, axis=1, keepdims=True) + jnp.exp2(sk - m)
            o = (_dot(pp.astype(BF16), v_p) + _dot(pc.astype(BF16), v_c)
                 + _dot(pn.astype(BF16), v_n) + _dot(px.astype(BF16), vx_ref[:, gs])) * (1.0 / den)
            for h in range(REP):
                cs = slice((g * REP + h) * HEAD_DIM, (g * REP + h + 1) * HEAD_DIM)
                o_ref[rq, cs] = (o[h * tq:(h + 1) * tq] * g_ref[rq, cs].astype(F32)).astype(BF16)


def _attn_a(qkv, sink_b, n_lat_tok):
    b, s, _ = qkv.shape
    tq = TQ_A
    n_blk = s // tq
    n_lat = n_lat_tok // tq
    kvw = A_KV_HEADS * HEAD_DIM
    ctx_len = s - n_lat_tok
    assert ctx_len == 2 * tq and n_lat_tok % ctx_len == 0
    qw = A_Q_HEADS * HEAD_DIM
    prev = lambda cb: (lambda bi, i: (bi, jnp.maximum(2 * i - 1, 0), cb))
    cur = lambda cb: (lambda bi, i: (bi, i, cb))
    nxt = lambda cb: (lambda bi, i: (bi, jnp.minimum(2 * i + 2, n_blk - 1), cb))
    ctx = lambda cb: (lambda bi, i: (bi, n_lat_tok // ctx_len, cb))
    kb, vb = KA0 // kvw, VA0 // kvw
    return pl.pallas_call(
        functools.partial(_attn_a_kernel, n_lat=n_lat),
        grid=(b, n_blk // 2),
        in_specs=[
            pl.BlockSpec((None, 2 * tq, qw), cur(QA0 // qw)),
            pl.BlockSpec((None, tq, kvw), prev(kb)),
            pl.BlockSpec((None, 2 * tq, kvw), cur(kb)),
            pl.BlockSpec((None, tq, kvw), nxt(kb)),
            pl.BlockSpec((None, tq, kvw), prev(vb)),
            pl.BlockSpec((None, 2 * tq, kvw), cur(vb)),
            pl.BlockSpec((None, tq, kvw), nxt(vb)),
            pl.BlockSpec((None, ctx_len, kvw), ctx(kb)),
            pl.BlockSpec((None, ctx_len, kvw), ctx(vb)),
            pl.BlockSpec((None, 2 * tq, qw), cur(GA0 // qw)),
            pl.BlockSpec((A_Q_HEADS, HEAD_DIM), lambda bi, i: (0, 0)),
        ],
        out_specs=pl.BlockSpec((None, 2 * tq, qw), lambda bi, i: (bi, i, 0)),
        out_shape=jax.ShapeDtypeStruct((b, s, qw), BF16),
        compiler_params=_params(("parallel", "parallel")),
        name="attn_window",
    )(qkv, qkv, qkv, qkv, qkv, qkv, qkv, qkv, qkv, qkv, sink_b)


def _attn_b_kernel(q_ref, k_ref, v_ref, g_ref, o_ref, m_sc, l_sc, acc_sc, p_sc, alpha_sc, *, n_lat, n_lat_tok):
    i = pl.program_id(2)
    tq, tk = TQ_B, TK_B
    n_ctx_tok = k_ref.shape[0] - n_lat_tok
    q = jnp.concatenate([q_ref[:, h * HEAD_DIM:(h + 1) * HEAD_DIM] for h in range(REP)], axis=0)
    m_sc[...] = jnp.full(m_sc.shape, NEG, F32)
    l_sc[...] = jnp.zeros(l_sc.shape, F32)
    acc_sc[...] = jnp.zeros(acc_sc.shape, F32)

    rb = 128

    def scores(k, slot):
        nk = k.shape[0]
        s = _dot_nt(q, k)
        for r0 in range(0, REP * tq, rb):
            rs = slice(r0, r0 + rb)
            s_b = s[rs]
            m_old = m_sc[rs]
            m_new = jnp.maximum(m_old, jnp.max(s_b, axis=1, keepdims=True))
            alpha = jnp.exp2(m_old - m_new)
            p = jnp.exp2(s_b - jnp.tile(m_new, (1, nk // LANES)))
            l_sc[rs] = alpha * l_sc[rs] + jnp.sum(p, axis=1, keepdims=True)
            m_sc[rs] = m_new
            alpha_sc[slot, rs] = alpha
            p_sc[slot, rs, 0:nk] = p.astype(BF16)

    def accumulate(slot, v):
        nk = v.shape[0]
        acc_sc[...] = alpha_sc[slot] * acc_sc[...] + _dot(p_sc[slot, :, 0:nk], v)

    def kv(ref, ci):
        if isinstance(ci, int):
            return ref[ci * tk:(ci + 1) * tk, :]
        return ref[pl.ds(pl.multiple_of(ci * tk, tk), tk), :]

    n_main = n_lat_tok // tk
    assert n_main % 2 == 0 and n_lat_tok % tk == 0
    k_ctx = k_ref[n_lat_tok:n_lat_tok + n_ctx_tok, :]
    v_ctx = v_ref[n_lat_tok:n_lat_tok + n_ctx_tok, :]

    @pl.when(i < n_lat)
    def _():
        scores(kv(k_ref, 0), 0)

        def pair(c1):
            scores(kv(k_ref, c1), 1)
            accumulate(0, kv(v_ref, c1 - 1))
            scores(kv(k_ref, c1 + 1), 0)
            accumulate(1, kv(v_ref, c1))

        def body(j, carry):
            pair(4 * j + 1)
            pair(4 * j + 3)
            return carry

        n_quads = (n_main - 2) // 4
        lax.fori_loop(0, n_quads, body, 0)
        for c1 in range(4 * n_quads + 1, n_main - 1, 2):
            pair(c1)
        scores(kv(k_ref, n_main - 1), 1)
        accumulate(0, kv(v_ref, n_main - 2))
        scores(k_ctx, 0)
        accumulate(1, kv(v_ref, n_main - 1))
        accumulate(0, v_ctx)

    @pl.when(i >= n_lat)
    def _():
        scores(k_ctx, 0)
        accumulate(0, v_ctx)

    o = acc_sc[...] * (1.0 / l_sc[...])
    for h in range(REP):
        cs = slice(h * HEAD_DIM, (h + 1) * HEAD_DIM)
        o_ref[:, cs] = (o[h * tq:(h + 1) * tq] * g_ref[:, cs].astype(F32)).astype(BF16)


def _attn_b(qkv, n_lat_tok):
    b, s, _ = qkv.shape
    tq = TQ_B
    n_t = s // tq
    n_lat = n_lat_tok // tq
    gw = REP * HEAD_DIM
    return pl.pallas_call(
        functools.partial(_attn_b_kernel, n_lat=n_lat, n_lat_tok=n_lat_tok),
        grid=(b, B_KV_HEADS, n_t),
        in_specs=[
            pl.BlockSpec((None, tq, gw), lambda bi, g, i: (bi, i, QB0 // gw + g)),
            pl.BlockSpec((None, s, HEAD_DIM), lambda bi, g, i: (bi, 0, KB0 // HEAD_DIM + g)),
            pl.BlockSpec((None, s, HEAD_DIM), lambda bi, g, i: (bi, 0, VB0 // HEAD_DIM + g)),
            pl.BlockSpec((None, tq, gw), lambda bi, g, i: (bi, i, GB0 // gw + g)),
        ],
        out_specs=pl.BlockSpec((None, tq, gw), lambda bi, g, i: (bi, i, g)),
        out_shape=jax.ShapeDtypeStruct((b, s, B_Q_HEADS * HEAD_DIM), BF16),
        scratch_shapes=[
            pltpu.VMEM((REP * tq, LANES), F32),
            pltpu.VMEM((REP * tq, LANES), F32),
            pltpu.VMEM((REP * tq, HEAD_DIM), F32),
            pltpu.VMEM((2, REP * tq, TK_B), BF16),
            pltpu.VMEM((2, REP * tq, LANES), F32),
        ],
        compiler_params=_params(("parallel", "parallel", "parallel")),
        name="attn_dense",
    )(qkv, qkv, qkv, qkv)


def _residual(x, y, mod, w_post):
    return x + mod[2:3] * _rms(y, w_post)


def _attn_out_kernel(ya_ref, yb_ref, wa_ref, wb_ref, x_ref, mod_ref, np_ref, o_ref):
    y = _dot(ya_ref[...], wa_ref[...]) + _dot(yb_ref[...], wb_ref[...])
    o_ref[...] = _residual(x_ref[...], y, mod_ref[...], np_ref[...])


def _attn_out(ya, yb, wa, wb, xs, modl, norm_post, n_tiles):
    b, s, d = xs.shape
    n_lat = s // TM - 1
    row = lambda bi, i: (bi, i, 0)
    const = lambda bi, i: (0, 0)
    return pl.pallas_call(
        _attn_out_kernel,
        grid=(b, n_tiles),
        in_specs=[
            pl.BlockSpec((None, TM, ya.shape[-1]), row),
            pl.BlockSpec((None, TM, yb.shape[-1]), row),
            pl.BlockSpec(wa.shape, const),
            pl.BlockSpec(wb.shape, const),
            pl.BlockSpec((None, TM, d), row),
            pl.BlockSpec((None, None, 3, d), lambda bi, i: (bi, i // n_lat, 0, 0)),
            pl.BlockSpec((1, d), const),
        ],
        out_specs=pl.BlockSpec((None, TM, d), row),
        out_shape=jax.ShapeDtypeStruct((b, n_tiles * TM, d), F32),
        compiler_params=_params(("parallel", "parallel")),
        name="attn_out",
    )(ya, yb, wa, wb, xs, modl, norm_post)


def _softplus(t):
    return jnp.maximum(t, 0.0) + jnp.log(1.0 + jnp.exp(-jnp.abs(t)))


def _ssm_in_kernel(x_ref, xp_ref, xn_ref, mod_ref, np_ref, wz_ref, wx_ref, wdt_ref, wdtt_ref,
                   cw_ref, cb_ref, dtb_ref, dtbt_ref, z_ref, xbc_ref, dt_ref, dtt_ref, *, n_lat):
    i = pl.program_id(1)
    tm = TM
    mod = mod_ref[...]
    w_pre = np_ref[...]
    hb = _pre_norm(x_ref[...], mod, w_pre).astype(BF16)
    is_ctx = i >= n_lat
    has_prev = jnp.logical_and(i > 0, jnp.logical_not(is_ctx)).astype(F32)
    has_next = jnp.logical_and(i < n_lat - 1, jnp.logical_not(is_ctx)).astype(F32)
    hp = (_pre_norm(xp_ref[...], mod, w_pre) * has_prev).astype(BF16)
    hn = (_pre_norm(xn_ref[...], mod, w_pre) * has_next).astype(BF16)
    ext = jnp.concatenate([hp, hb, hn], axis=0)

    z_ref[...] = _silu(_dot(hb, wz_ref[...])).astype(BF16)
    dt_ref[...] = _softplus(_dot(hb, wdt_ref[...]) + dtb_ref[...])
    dtt_ref[...] = _softplus(_dot_nt(wdtt_ref[...], hb) + dtbt_ref[...])

    nblk = 512
    nt = tm // 8
    sub = lax.broadcasted_iota(jnp.int32, (8, nblk), 0)
    for j in range(wx_ref.shape[1] // nblk):
        cs = slice(j * nblk, (j + 1) * nblk)
        u3 = _dot(ext, wx_ref[:, cs]).reshape(nt + 2, 8, nblk)
        dn = pltpu.roll(u3, 1, 1)
        upw = pltpu.roll(u3, 7, 1)
        u_prev = jnp.where(sub == 0, dn[0:nt], dn[1:nt + 1])
        u_next = jnp.where(sub == 7, upw[2:nt + 2], upw[1:nt + 1])
        conv = (cb_ref[:, cs] + cw_ref[0:1, cs] * u_prev + cw_ref[1:2, cs] * u3[1:nt + 1]
                + cw_ref[2:3, cs] * u_next)
        xbc_ref[:, cs] = _silu(conv).reshape(tm, nblk).astype(BF16)


def _ssm_in(xs, modl, norm_pre, wz, wx, wdt, wdtt, cw, cb, dtb, dtbt):
    b, s, d = xs.shape
    n_t = s // TM
    n_lat = n_t - 1
    row = lambda bi, i: (bi, i, 0)
    const = lambda bi, i: (0, 0)
    r8 = TM // 8
    n8 = s // 8
    nh2 = wdt.shape[1]
    return pl.pallas_call(
        functools.partial(_ssm_in_kernel, n_lat=n_lat),
        grid=(b, n_t),
        in_specs=[
            pl.BlockSpec((None, TM, d), row),
            pl.BlockSpec((None, 8, d), lambda bi, i: (bi, jnp.maximum(i * r8 - 1, 0), 0)),
            pl.BlockSpec((None, 8, d), lambda bi, i: (bi, jnp.minimum((i + 1) * r8, n8 - 1), 0)),
            pl.BlockSpec((None, None, 3, d), lambda bi, i: (bi, i // n_lat, 0, 0)),
            pl.BlockSpec((1, d), const),
            pl.BlockSpec(wz.shape, const),
            pl.BlockSpec(wx.shape, const),
            pl.BlockSpec(wdt.shape, const),
            pl.BlockSpec(wdtt.shape, const),
            pl.BlockSpec(cw.shape, const),
            pl.BlockSpec(cb.shape, const),
            pl.BlockSpec(dtb.shape, const),
            pl.BlockSpec(dtbt.shape, const),
        ],
        out_specs=[
            pl.BlockSpec((None, TM, wz.shape[1]), row),
            pl.BlockSpec((None, TM, wx.shape[1]), row),
            pl.BlockSpec((None, TM, nh2), row),
            pl.BlockSpec((None, nh2, TM), lambda bi, i: (bi, 0, i)),
        ],
        out_shape=[
            jax.ShapeDtypeStruct((b, s, wz.shape[1]), BF16),
            jax.ShapeDtypeStruct((b, s, wx.shape[1]), BF16),
            jax.ShapeDtypeStruct((b, s, nh2), F32),
            jax.ShapeDtypeStruct((b, nh2, s), F32),
        ],
        compiler_params=_params(("parallel", "parallel")),
        name="ssm_in",
    )(xs, xs, xs, modl, norm_pre, wz, wx, wdt, wdtt, cw, cb, dtb, dtbt)


def _split3(t):
    hi = t.astype(BF16)
    r1 = t - hi.astype(F32)
    mid = r1.astype(BF16)
    lo = (r1 - mid.astype(F32)).astype(BF16)
    return hi, mid, lo


def _ones_dot_lhs(tri01, a):
    return _dot(jnp.concatenate([tri01] * 3, axis=1), jnp.concatenate(_split3(a), axis=0))


def _ones_dot_rhs(at, tri01):
    return _dot(jnp.concatenate(_split3(at), axis=1), jnp.concatenate([tri01] * 3, axis=0))


def _ssd_direction(xbc_ref, dt, dtt, a_row, a_col, h_sc, y_ref, reverse, hoff):
    q_len = SSM_CHUNK
    d_inner = SSM_HEADS * SSM_HEAD_DIM
    gw = SSM_REP * SSM_HEAD_DIM
    row = lax.broadcasted_iota(jnp.int32, (q_len, q_len), 0)
    col = lax.broadcasted_iota(jnp.int32, (q_len, q_len), 1)
    lower = row >= col
    upper = row <= col
    mask = upper if reverse else lower
    tri = jnp.where(mask, 1.0, 0.0).astype(BF16)
    tri_t = jnp.where(lower if reverse else upper, 1.0, 0.0).astype(BF16)

    a = dt * a_row
    at = dtt * a_col
    acum = _ones_dot_lhs(tri, a)
    acum_t = _ones_dot_rhs(at, tri_t)
    total_t = jnp.sum(at, axis=1, keepdims=True)
    w_t = jnp.exp2(total_t - acum_t) * dtt
    etot = jnp.exp2(jnp.sum(a, axis=0, keepdims=True))
    head_of_col = jnp.right_shift(lax.broadcasted_iota(jnp.int32, (q_len, gw), 1),
                                  SSM_HEAD_DIM.bit_length() - 1)

    for g in range(SSM_GROUPS):
        b_g = xbc_ref[:, d_inner + g * D_STATE:d_inner + (g + 1) * D_STATE]
        c_g = xbc_ref[:, d_inner + (SSM_GROUPS + g) * D_STATE:d_inner + (SSM_GROUPS + g + 1) * D_STATE]
        gsl = slice(g * gw, (g + 1) * gw)
        x_g = xbc_ref[:, gsl]
        cb = _dot_nt(c_g, b_g)
        c_f = c_g.astype(F32)
        b_t = b_g.astype(F32).T
        h_g = h_sc[g]
        h_b = h_g.astype(BF16)
        ms, ces, bws, bdx, bdh = [], [], [], [], []
        etot_row = None
        for r in range(SSM_REP):
            h = hoff + g * SSM_REP + r
            acb = jnp.broadcast_to(acum[:, h:h + 1], (q_len, q_len))
            decay = jnp.exp2(jnp.where(mask, acb - acum_t[h:h + 1, :], NEG))
            ms.append((cb * decay * dtt[h:h + 1, :]).astype(BF16))
            ces.append((c_f * jnp.exp2(acb)).astype(BF16))
            bws.append((b_t * w_t[h:h + 1, :]).astype(BF16))
            sel = head_of_col == r
            bdx.append(jnp.where(sel, x_g, jnp.zeros_like(x_g)))
            bdh.append(jnp.where(sel, h_b, jnp.zeros_like(h_b)))
            e_r = jnp.broadcast_to(etot[:, h:h + 1], (1, gw))
            etot_row = e_r if etot_row is None else jnp.where(head_of_col[0:1] == r, e_r, etot_row)
        bdx = jnp.concatenate(bdx, axis=0)
        bdh = jnp.concatenate(bdh, axis=0)
        lhs = jnp.concatenate([jnp.concatenate(ms, axis=1), jnp.concatenate(bws, axis=1)], axis=0)
        res = _dot(lhs, bdx)
        y = res[0:q_len] + _dot(jnp.concatenate(ces, axis=1), bdh)
        y_ref[:, gsl] = y.astype(BF16)
        h_sc[g] = h_g * etot_row + res[q_len:]


def _ssd_kernel(xf_ref, xb_ref, dtf_ref, dtb_ref, dttf_ref, dttb_ref, alog_ref, alogt_ref,
                yf_ref, yb_ref, hf_sc, hb_sc):
    @pl.when(pl.program_id(1) == 0)
    def _():
        hf_sc[...] = jnp.zeros(hf_sc.shape, F32)
        hb_sc[...] = jnp.zeros(hb_sc.shape, F32)

    a_row = -jnp.exp(alog_ref[...]) * LOG2E
    a_col = -jnp.exp(alogt_ref[...]) * LOG2E
    _ssd_direction(xf_ref, dtf_ref[...], dttf_ref[...], a_row, a_col, hf_sc, yf_ref, False, 0)
    _ssd_direction(xb_ref, dtb_ref[...], dttb_ref[...], a_row, a_col, hb_sc, yb_ref, True, SSM_HEADS)


def _ssd(xbc, dt, dtt, alog, alogt, n_lat_tok):
    b, s, cw = xbc.shape
    q_len = SSM_CHUNK
    n_c = s // q_len
    n_lat = n_lat_tok // q_len
    d_inner = SSM_HEADS * SSM_HEAD_DIM
    nh2 = dt.shape[-1]
    cf = lambda j: (j + n_lat) % n_c
    cbk = lambda j: n_c - 1 - j
    const = lambda bi, j: (0, 0)
    return pl.pallas_call(
        _ssd_kernel,
        grid=(b, n_c),
        in_specs=[
            pl.BlockSpec((None, q_len, cw), lambda bi, j: (bi, cf(j), 0)),
            pl.BlockSpec((None, q_len, cw), lambda bi, j: (bi, cbk(j), 0)),
            pl.BlockSpec((None, q_len, nh2), lambda bi, j: (bi, cf(j), 0)),
            pl.BlockSpec((None, q_len, nh2), lambda bi, j: (bi, cbk(j), 0)),
            pl.BlockSpec((None, nh2, q_len), lambda bi, j: (bi, 0, cf(j))),
            pl.BlockSpec((None, nh2, q_len), lambda bi, j: (bi, 0, cbk(j))),
            pl.BlockSpec(alog.shape, const),
            pl.BlockSpec(alogt.shape, const),
        ],
        out_specs=[
            pl.BlockSpec((None, q_len, d_inner), lambda bi, j: (bi, cf(j), 0)),
            pl.BlockSpec((None, q_len, d_inner), lambda bi, j: (bi, cbk(j), 0)),
        ],
        out_shape=[jax.ShapeDtypeStruct((b, s, d_inner), BF16)] * 2,
        scratch_shapes=[pltpu.VMEM((SSM_GROUPS, D_STATE, SSM_REP * SSM_HEAD_DIM), F32)] * 2,
        compiler_params=_params(("parallel", "arbitrary")),
        name="ssd_scan",
    )(xbc, xbc, dt, dt, dtt, dtt, alog, alogt)


def _ssm_out_kernel(yf_ref, yb_ref, xs_ref, z_ref, dsk_ref, nw_ref, w_ref, x_ref, mod_ref, np_ref, o_ref):
    y = yf_ref[...].astype(F32) + yb_ref[...].astype(F32) + dsk_ref[...] * xs_ref[...].astype(F32)
    gated = y * z_ref[...].astype(F32)
    gsz = gated.shape[1] // SSM_GROUPS
    parts = []
    for g in range(SSM_GROUPS):
        t = gated[:, g * gsz:(g + 1) * gsz]
        parts.append(t * lax.rsqrt(jnp.mean(t * t, axis=-1, keepdims=True) + EPS))
    gn = (jnp.concatenate(parts, axis=1) * nw_ref[...]).astype(BF16)
    o_ref[...] = _residual(x_ref[...], _dot(gn, w_ref[...]), mod_ref[...], np_ref[...])


def _ssm_out(yf, yb, xbc, z, dsk, nw, w, xs, modl, norm_post, n_tiles):
    b, s, d = xs.shape
    n_lat = s // TM - 1
    di = yf.shape[-1]
    row = lambda bi, i: (bi, i, 0)
    const = lambda bi, i: (0, 0)
    return pl.pallas_call(
        _ssm_out_kernel,
        grid=(b, n_tiles),
        in_specs=[
            pl.BlockSpec((None, TM, di), row),
            pl.BlockSpec((None, TM, di), row),
            pl.BlockSpec((None, TM, di), row),
            pl.BlockSpec((None, TM, di), row),
            pl.BlockSpec((1, di), const),
            pl.BlockSpec((1, di), const),
            pl.BlockSpec(w.shape, const),
            pl.BlockSpec((None, TM, d), row),
            pl.BlockSpec((None, None, 3, d), lambda bi, i: (bi, i // n_lat, 0, 0)),
            pl.BlockSpec((1, d), const),
        ],
        out_specs=pl.BlockSpec((None, TM, d), row),
        out_shape=jax.ShapeDtypeStruct((b, n_tiles * TM, d), F32),
        compiler_params=_params(("parallel", "parallel")),
        name="ssm_out",
    )(yf, yb, xbc, z, dsk, nw, w, xs, modl, norm_post)


def _rope_tables(n_lat_tok, n_ctx_tok):
    t = np.arange(n_lat_tok)
    n_freq = HEAD_DIM // 4
    inv = 1.0 / (ROPE_THETA ** (jnp.arange(n_freq, dtype=F32) / n_freq))
    rowp = jnp.asarray(t // GRID_W, F32)
    colp = jnp.asarray(t % GRID_W, F32)
    ang = jnp.concatenate([rowp[:, None] * inv, colp[:, None] * inv], axis=-1)
    cos, sin = jnp.cos(ang), jnp.sin(ang)
    cos2 = jnp.concatenate([cos, cos], axis=-1)
    sin2 = jnp.concatenate([-sin, sin], axis=-1)
    cos2 = jnp.concatenate([cos2, jnp.ones((n_ctx_tok, HEAD_DIM), F32)], axis=0)
    sin2 = jnp.concatenate([sin2, jnp.zeros((n_ctx_tok, HEAD_DIM), F32)], axis=0)
    return cos2, sin2


_DEINT = np.concatenate([np.arange(0, HEAD_DIM, 2), np.arange(1, HEAD_DIM, 2)])


def _attn_weight_columns():
    qa, ka, va, ga, qb, kb, vb, gb = 0, 1024, 1280, 1536, 2560, 3584, 3840, 4096

    def heads(start, n, perm):
        base = start + HEAD_DIM * np.arange(n)[:, None]
        return (base + (_DEINT if perm else np.arange(HEAD_DIM))[None, :]).reshape(-1)

    return np.concatenate([
        heads(qa, 8, True), heads(ga, 8, False), heads(qb, 8, True), heads(gb, 8, False),
        heads(ka, 2, True), heads(va, 2, False), heads(kb, 2, True), heads(vb, 2, False)])


_ATTN_COLS_IDX = _attn_weight_columns()


def kernel(x, c, ctx, c_ctx, w_ada, b_ada, norm_pre, norm_post, attn_w_in, attn_w_out, attn_sink,
           attn_q_norm, attn_k_norm, ssm_w_in, ssm_conv_w, ssm_conv_b, ssm_dt_bias, ssm_a_log, ssm_d,
           ssm_norm, ssm_w_out):
    bsz, n_lat_tok, d = x.shape
    n_ctx_tok = ctx.shape[1]
    depth = w_ada.shape[0]
    assert n_ctx_tok == TM and n_lat_tok % TM == 0 and bsz <= 7
    d_inner = SSM_HEADS * SSM_HEAD_DIM
    bc_w = 2 * SSM_GROUPS * D_STATE

    xs = jnp.concatenate([x, ctx], axis=1)
    cc = jnp.zeros((8, d), F32).at[:bsz].set(c).at[bsz].set(c_ctx)
    mod = _modulation(cc, w_ada, b_ada)
    mod = mod.reshape(depth, 8, 3, d)
    cos2, sin2 = _rope_tables(n_lat_tok, n_ctx_tok)

    for l in range(depth):
        last = l == depth - 1
        n_tiles = (n_lat_tok if last else n_lat_tok + n_ctx_tok) // TM
        modl = jnp.stack([mod[l, :bsz], jnp.broadcast_to(mod[l, bsz], (bsz, 3, d))], axis=1)
        npre = norm_pre[l].reshape(1, d)
        npost = norm_post[l].reshape(1, d)
        i = l // 2
        if l % 2 == 0:
            w = attn_w_in[i][:, _ATTN_COLS_IDX].astype(BF16)
            qn = attn_q_norm[i][_DEINT].reshape(1, HEAD_DIM)
            kn = attn_k_norm[i][_DEINT].reshape(1, HEAD_DIM)
            qkv = _attn_in(xs, modl, npre, w, cos2, sin2, qn, kn)
            sink_b = jnp.broadcast_to(attn_sink[i][:, None], (A_Q_HEADS, HEAD_DIM))
            ya = _attn_a(qkv, sink_b, n_lat_tok)
            yb = _attn_b(qkv, n_lat_tok)
            wo = attn_w_out[i].astype(BF16)
            aq = A_Q_HEADS * HEAD_DIM
            xs = _attn_out(ya, yb, wo[:aq], wo[aq:], xs, modl, npost, n_tiles)
        else:
            w = ssm_w_in[i]
            wz = w[:, :d_inner].astype(BF16)
            wx = w[:, d_inner:2 * d_inner + bc_w].astype(BF16)
            wdt = jnp.pad(w[:, 2 * d_inner + bc_w:], ((0, 0), (0, HPAD - 2 * SSM_HEADS))).astype(BF16)
            dtb = jnp.pad(ssm_dt_bias[i].reshape(1, -1), ((0, 0), (0, HPAD - 2 * SSM_HEADS)))
            z, xbc, dt, dtt = _ssm_in(xs, modl, npre, wz, wx, wdt, wdt.T, ssm_conv_w[i],
                                      ssm_conv_b[i].reshape(1, -1), dtb, dtb.reshape(-1, 1))
            alog = jnp.pad(ssm_a_log[i].reshape(1, -1), ((0, 0), (0, HPAD - 2 * SSM_HEADS)))
            yf, ybk = _ssd(xbc, dt, dtt, alog, alog.reshape(-1, 1), n_lat_tok)
            dsk = jnp.repeat(ssm_d[i], SSM_HEAD_DIM).reshape(1, d_inner)
            xs = _ssm_out(yf, ybk, xbc, z, dsk, ssm_norm[i].reshape(1, d_inner),
                          ssm_w_out[i].astype(BF16), xs, modl, npost, n_tiles)
    return xs
```

```python
import functools

import numpy as np
import jax
import jax.numpy as jnp
from jax import lax
from jax.experimental import pallas as pl
from jax.experimental.pallas import tpu as pltpu

F32 = jnp.float32
BF16 = jnp.bfloat16

EPS = 1e-6
GRID_W = 64
ROPE_THETA = 10000.0
HEAD_DIM = 128
A_Q_HEADS = 8
A_KV_HEADS = 2
B_Q_HEADS = 8
B_KV_HEADS = 2
REP = 4
WINDOW = 128
SSM_HEAD_DIM = 64
SSM_HEADS = 32
SSM_GROUPS = 8
SSM_REP = SSM_HEADS // SSM_GROUPS
D_STATE = 128
SSM_CHUNK = 128
HPAD = 128

V7X_VMEM_BYTES = 64 * 1024 * 1024
VMEM_LIMIT = V7X_VMEM_BYTES - 8 * 1024 * 1024

TM = 256
CONV_COLS = 512
TQ_A = 128
TQ_B = 512
TK_B = 512
LANES = 128
LOG2E = 1.4426950408889634

QA0, GA0, QB0, GB0, KA0, VA0, KB0, VB0 = 0, 1024, 2048, 3072, 4096, 4352, 4608, 4864
ATTN_COLS = 5120

NEG = -1e30


def _params(sem, vmem=VMEM_LIMIT):
    return pltpu.CompilerParams(dimension_semantics=sem, vmem_limit_bytes=vmem)


def _silu(t):
    return t * (1.0 / (1.0 + jnp.exp2(t * (-LOG2E))))


def _rms(t, w):
    return t * lax.rsqrt(jnp.mean(t * t, axis=-1, keepdims=True) + EPS) * w


def _dot(a, b):
    return jnp.dot(a, b, preferred_element_type=F32)


def _dot_nt(a, b):
    return lax.dot_general(a, b, (((1,), (1,)), ((), ())), preferred_element_type=F32)


def _dot_hi(a, b):
    return jnp.dot(a, b, preferred_element_type=F32, precision=lax.Precision.HIGHEST)


def _mod_kernel(cc_ref, w_ref, b_ref, o_ref):
    o_ref[...] = _dot_hi(_silu(cc_ref[...]), w_ref[...]) + b_ref[...]


def _modulation(cc, w_ada, b_ada):
    depth, d, d3 = w_ada.shape
    return pl.pallas_call(
        _mod_kernel,
        grid=(depth,),
        in_specs=[
            pl.BlockSpec((8, d), lambda l: (0, 0)),
            pl.BlockSpec((None, d, d3), lambda l: (l, 0, 0)),
            pl.BlockSpec((None, 1, d3), lambda l: (l, 0, 0)),
        ],
        out_specs=pl.BlockSpec((None, 8, d3), lambda l: (l, 0, 0)),
        out_shape=jax.ShapeDtypeStruct((depth, 8, d3), F32),
        compiler_params=_params(("arbitrary",)),
        name="modulation",
    )(cc, w_ada, b_ada.reshape(depth, 1, d3))


def _pre_norm(x, mod, w):
    return _rms(x, w) * (1.0 + mod[1:2]) + mod[0:1]


def _attn_in_kernel(x_ref, mod_ref, np_ref, w_ref, cos_ref, sin_ref, qn_ref, kn_ref, o_ref):
    hb = _pre_norm(x_ref[...], mod_ref[...], np_ref[...]).astype(BF16)
    cos = cos_ref[...]
    sin = sin_ref[...]
    scale = HEAD_DIM ** -0.5

    def rope(t):
        return t * cos + pltpu.roll(t, HEAD_DIM // 2, 1) * sin

    nblk = 512
    for j in range(ATTN_COLS // nblk):
        c0 = j * nblk
        t = _dot(hb, w_ref[:, c0:c0 + nblk])
        for hh in range(nblk // HEAD_DIM):
            col = c0 + hh * HEAD_DIM
            th = t[:, hh * HEAD_DIM:(hh + 1) * HEAD_DIM]
            if col < GA0:
                th = rope(th) * (scale * LOG2E)
            elif col < QB0 or GB0 <= col < KA0:
                th = _silu(th)
            elif col < GB0:
                th = rope(_rms(th, qn_ref[...])) * (scale * LOG2E)
            elif col < VA0:
                th = rope(th)
            elif KB0 <= col < VB0:
                th = rope(_rms(th, kn_ref[...]))
            o_ref[:, col:col + HEAD_DIM] = th.astype(BF16)


def _attn_in(xs, modl, norm_pre, w, cos2, sin2, qn, kn):
    b, s, d = xs.shape
    n_t = s // TM
    n_lat = n_t - 1
    row = lambda bi, i: (bi, i, 0)
    return pl.pallas_call(
        _attn_in_kernel,
        grid=(b, n_t),
        in_specs=[
            pl.BlockSpec((None, TM, d), row),
            pl.BlockSpec((None, None, 3, d), lambda bi, i: (bi, i // n_lat, 0, 0)),
            pl.BlockSpec((1, d), lambda bi, i: (0, 0)),
            pl.BlockSpec((d, ATTN_COLS), lambda bi, i: (0, 0)),
            pl.BlockSpec((TM, HEAD_DIM), lambda bi, i: (i, 0)),
            pl.BlockSpec((TM, HEAD_DIM), lambda bi, i: (i, 0)),
            pl.BlockSpec((1, HEAD_DIM), lambda bi, i: (0, 0)),
            pl.BlockSpec((1, HEAD_DIM), lambda bi, i: (0, 0)),
        ],
        out_specs=pl.BlockSpec((None, TM, ATTN_COLS), row),
        out_shape=jax.ShapeDtypeStruct((b, s, ATTN_COLS), BF16),
        compiler_params=_params(("parallel", "parallel")),
        name="attn_in",
    )(xs, modl, norm_pre, w, cos2, sin2, qn, kn)


def _attn_a_kernel(q_ref, kp_ref, kc_ref, kn_ref, vp_ref, vc_ref, vn_ref, kx_ref, vx_ref,
                   g_ref, sink_ref, o_ref, *, n_lat):
    i = pl.program_id(1)
    tq = TQ_A
    rows = REP * tq
    r = lax.broadcasted_iota(jnp.int32, (rows, tq), 0) & (tq - 1)
    c = lax.broadcasted_iota(jnp.int32, (rows, tq), 1)
    is_lat = 2 * i < n_lat
    pen_cur = jnp.where(is_lat, 0.0, NEG)
    for a in range(2):
        blk = 2 * i + a
        rq = slice(a * tq, (a + 1) * tq)
        lim_prev = jnp.where(jnp.logical_and(blk > 0, is_lat), 0, tq)
        lim_next = jnp.where(jnp.logical_and(blk < n_lat - 1, is_lat), 0, tq)
        m_prev = (c - r) >= lim_prev
        m_next = (r - c) >= lim_next
        for g in range(A_KV_HEADS):
            gs = slice(g * HEAD_DIM, (g + 1) * HEAD_DIM)
            if a == 0:
                k_p, v_p = kp_ref[:, gs], vp_ref[:, gs]
                k_n, v_n = kc_ref[tq:2 * tq, gs], vc_ref[tq:2 * tq, gs]
            else:
                k_p, v_p = kc_ref[0:tq, gs], vc_ref[0:tq, gs]
                k_n, v_n = kn_ref[:, gs], vn_ref[:, gs]
            k_c, v_c = kc_ref[rq, gs], vc_ref[rq, gs]
            q = jnp.concatenate(
                [q_ref[rq, (g * REP + h) * HEAD_DIM:(g * REP + h + 1) * HEAD_DIM] for h in range(REP)], axis=0)
            sp = jnp.where(m_prev, _dot_nt(q, k_p), NEG)
            sc = _dot_nt(q, k_c) + pen_cur
            sn = jnp.where(m_next, _dot_nt(q, k_n), NEG)
            sx = _dot_nt(q, kx_ref[:, gs])
            sk = jnp.concatenate(
                [jnp.broadcast_to(sink_ref[g * REP + h:g * REP + h + 1, 0:1] * LOG2E, (tq, 1))
                 for h in range(REP)], axis=0)
            sx0, sx1 = sx[:, 0:tq], sx[:, tq:2 * tq]
            m_t = jnp.maximum(jnp.maximum(jnp.maximum(sp, sc), jnp.maximum(sn, sx0)), sx1)
            m = jnp.maximum(jnp.max(m_t, axis=1, keepdims=True), sk)
            pp = jnp.exp2(sp - m)
            pc = jnp.exp2(sc - m)
            pn = jnp.exp2(sn - m)
            px = jnp.exp2(sx - m)
            den_t = (pp + pc) + (pn + px[:, 0:tq]) + px[:, tq:2 * tq]
            den = jnp.sum(den_t, axis=1, keepdims=True) + jnp.exp2(sk - m)
            o = (_dot(pp.astype(BF16), v_p) + _dot(pc.astype(BF16), v_c)
                 + _dot(pn.astype(BF16), v_n) + _dot(px.astype(BF16), vx_ref[:, gs])) * (1.0 / den)
            for h in range(REP):
                cs = slice((g * REP + h) * HEAD_DIM, (g * REP + h + 1) * HEAD_DIM)
                o_ref[rq, cs] = (o[h * tq:(h + 1) * tq] * g_ref[rq, cs].astype(F32)).astype(BF16)


def _attn_a(qkv, sink_b, n_lat_tok):
    b, s, _ = qkv.shape
    tq = TQ_A
    n_blk = s // tq
    n_lat = n_lat_tok // tq
    kvw = A_KV_HEADS * HEAD_DIM
    ctx_len = s - n_lat_tok
    assert ctx_len == 2 * tq and n_lat_tok % ctx_len == 0
    qw = A_Q_HEADS * HEAD_DIM
    prev = lambda cb: (lambda bi, i: (bi, jnp.maximum(2 * i - 1, 0), cb))
    cur = lambda cb: (lambda bi, i: (bi, i, cb))
    nxt = lambda cb: (lambda bi, i: (bi, jnp.minimum(2 * i + 2, n_blk - 1), cb))
    ctx = lambda cb: (lambda bi, i: (bi, n_lat_tok // ctx_len, cb))
    kb, vb = KA0 // kvw, VA0 // kvw
    return pl.pallas_call(
        functools.partial(_attn_a_kernel, n_lat=n_lat),
        grid=(b, n_blk // 2),
        in_specs=[
            pl.BlockSpec((None, 2 * tq, qw), cur(QA0 // qw)),
            pl.BlockSpec((None, tq, kvw), prev(kb)),
            pl.BlockSpec((None, 2 * tq, kvw), cur(kb)),
            pl.BlockSpec((None, tq, kvw), nxt(kb)),
            pl.BlockSpec((None, tq, kvw), prev(vb)),
            pl.BlockSpec((None, 2 * tq, kvw), cur(vb)),
            pl.BlockSpec((None, tq, kvw), nxt(vb)),
            pl.BlockSpec((None, ctx_len, kvw), ctx(kb)),
            pl.BlockSpec((None, ctx_len, kvw), ctx(vb)),
            pl.BlockSpec((None, 2 * tq, qw), cur(GA0 // qw)),
            pl.BlockSpec((A_Q_HEADS, HEAD_DIM), lambda bi, i: (0, 0)),
        ],
        out_specs=pl.BlockSpec((None, 2 * tq, qw), lambda bi, i: (bi, i, 0)),
        out_shape=jax.ShapeDtypeStruct((b, s, qw), BF16),
        compiler_params=_params(("parallel", "parallel")),
        name="attn_window",
    )(qkv, qkv, qkv, qkv, qkv, qkv, qkv, qkv, qkv, qkv, sink_b)


def _attn_b_kernel(q_ref, k_ref, v_ref, g_ref, o_ref, m_sc, l_sc, acc_sc, p_sc, alpha_sc, *, n_lat_tok):
    tq, tk = TQ_B, TK_B
    n_ctx_tok = k_ref.shape[0] - n_lat_tok
    q = jnp.concatenate([q_ref[:, h * HEAD_DIM:(h + 1) * HEAD_DIM] for h in range(REP)], axis=0)
    m_sc[...] = jnp.full(m_sc.shape, NEG, F32)
    l_sc[...] = jnp.zeros(l_sc.shape, F32)
    acc_sc[...] = jnp.zeros(acc_sc.shape, F32)

    rb = 128

    def scores(k, slot):
        nk = k.shape[0]
        s = _dot_nt(q, k)
        for r0 in range(0, REP * tq, rb):
            rs = slice(r0, r0 + rb)
            s_b = s[rs]
            m_old = m_sc[rs]
            m_new = jnp.maximum(m_old, jnp.max(s_b, axis=1, keepdims=True))
            alpha = jnp.exp2(m_old - m_new)
            p = jnp.exp2(s_b - jnp.tile(m_new, (1, nk // LANES)))
            l_sc[rs] = alpha * l_sc[rs] + jnp.sum(p, axis=1, keepdims=True)
            m_sc[rs] = m_new
            alpha_sc[slot, rs] = alpha
            p_sc[slot, rs, 0:nk] = p.astype(BF16)

    def accumulate(slot, v):
        nk = v.shape[0]
        acc_sc[...] = alpha_sc[slot] * acc_sc[...] + _dot(p_sc[slot, :, 0:nk], v)

    def kv(ref, ci):
        if isinstance(ci, int):
            return ref[ci * tk:(ci + 1) * tk, :]
        return ref[pl.ds(pl.multiple_of(ci * tk, tk), tk), :]

    n_main = n_lat_tok // tk
    assert n_main % 2 == 0 and n_lat_tok % tk == 0
    scores(kv(k_ref, 0), 0)

    def pair(c1):
        scores(kv(k_ref, c1), 1)
        accumulate(0, kv(v_ref, c1 - 1))
        scores(kv(k_ref, c1 + 1), 0)
        accumulate(1, kv(v_ref, c1))

    def body(j, carry):
        pair(4 * j + 1)
        pair(4 * j + 3)
        return carry

    n_quads = (n_main - 2) // 4
    lax.fori_loop(0, n_quads, body, 0)
    for c1 in range(4 * n_quads + 1, n_main - 1, 2):
        pair(c1)
    scores(kv(k_ref, n_main - 1), 1)
    accumulate(0, kv(v_ref, n_main - 2))
    scores(k_ref[n_lat_tok:n_lat_tok + n_ctx_tok, :], 0)
    accumulate(1, kv(v_ref, n_main - 1))
    accumulate(0, v_ref[n_lat_tok:n_lat_tok + n_ctx_tok, :])

    o = acc_sc[...] * (1.0 / l_sc[...])
    for h in range(REP):
        cs = slice(h * HEAD_DIM, (h + 1) * HEAD_DIM)
        o_ref[:, cs] = (o[h * tq:(h + 1) * tq] * g_ref[:, cs].astype(F32)).astype(BF16)


def _attn_b_ctx_kernel(q_ref, k_ref, v_ref, g_ref, y_hbm_ref, o_ref):
    del y_hbm_ref
    n = q_ref.shape[0]
    q = jnp.concatenate([q_ref[:, h * HEAD_DIM:(h + 1) * HEAD_DIM] for h in range(REP)], axis=0)
    s = _dot_nt(q, k_ref[...])
    p = jnp.exp2(s - jnp.max(s, axis=1, keepdims=True))
    o = _dot(p.astype(BF16), v_ref[...]) * (1.0 / jnp.sum(p, axis=1, keepdims=True))
    for h in range(REP):
        cs = slice(h * HEAD_DIM, (h + 1) * HEAD_DIM)
        o_ref[:, cs] = (o[h * n:(h + 1) * n] * g_ref[:, cs].astype(F32)).astype(BF16)


def _attn_b(qkv, n_lat_tok):
    b, s, _ = qkv.shape
    tq = TQ_B
    gw = REP * HEAD_DIM
    n_ctx_tok = s - n_lat_tok
    assert n_lat_tok % tq == 0 and n_lat_tok % n_ctx_tok == 0
    y_lat = pl.pallas_call(
        functools.partial(_attn_b_kernel, n_lat_tok=n_lat_tok),
        grid=(b, B_KV_HEADS, n_lat_tok // tq),
        in_specs=[
            pl.BlockSpec((None, tq, gw), lambda bi, g, i: (bi, i, QB0 // gw + g)),
            pl.BlockSpec((None, s, HEAD_DIM), lambda bi, g, i: (bi, 0, KB0 // HEAD_DIM + g)),
            pl.BlockSpec((None, s, HEAD_DIM), lambda bi, g, i: (bi, 0, VB0 // HEAD_DIM + g)),
            pl.BlockSpec((None, tq, gw), lambda bi, g, i: (bi, i, GB0 // gw + g)),
        ],
        out_specs=pl.BlockSpec((None, tq, gw), lambda bi, g, i: (bi, i, g)),
        out_shape=jax.ShapeDtypeStruct((b, s, B_Q_HEADS * HEAD_DIM), BF16),
        scratch_shapes=[
            pltpu.VMEM((REP * tq, LANES), F32),
            pltpu.VMEM((REP * tq, LANES), F32),
            pltpu.VMEM((REP * tq, HEAD_DIM), F32),
            pltpu.VMEM((2, REP * tq, TK_B), BF16),
            pltpu.VMEM((2, REP * tq, LANES), F32),
        ],
        compiler_params=_params(("parallel", "parallel", "parallel")),
        name="attn_dense",
    )(qkv, qkv, qkv, qkv)
    cblk = n_lat_tok // n_ctx_tok
    return pl.pallas_call(
        _attn_b_ctx_kernel,
        grid=(b, B_KV_HEADS),
        in_specs=[
            pl.BlockSpec((None, n_ctx_tok, gw), lambda bi, g: (bi, cblk, QB0 // gw + g)),
            pl.BlockSpec((None, n_ctx_tok, HEAD_DIM), lambda bi, g: (bi, cblk, KB0 // HEAD_DIM + g)),
            pl.BlockSpec((None, n_ctx_tok, HEAD_DIM), lambda bi, g: (bi, cblk, VB0 // HEAD_DIM + g)),
            pl.BlockSpec((None, n_ctx_tok, gw), lambda bi, g: (bi, cblk, GB0 // gw + g)),
            pl.BlockSpec(memory_space=pl.ANY),
        ],
        out_specs=pl.BlockSpec((None, n_ctx_tok, gw), lambda bi, g: (bi, cblk, g)),
        out_shape=jax.ShapeDtypeStruct(y_lat.shape, BF16),
        input_output_aliases={4: 0},
        compiler_params=_params(("parallel", "parallel")),
        name="attn_dense_ctx",
    )(qkv, qkv, qkv, qkv, y_lat)


def _residual(x, y, mod, w_post):
    return x + mod[2:3] * _rms(y, w_post)


def _attn_out_kernel(ya_ref, yb_ref, wa_ref, wb_ref, x_ref, mod_ref, np_ref, o_ref):
    y = _dot(ya_ref[...], wa_ref[...]) + _dot(yb_ref[...], wb_ref[...])
    o_ref[...] = _residual(x_ref[...], y, mod_ref[...], np_ref[...])


def _attn_out(ya, yb, wa, wb, xs, modl, norm_post, n_tiles):
    b, s, d = xs.shape
    n_lat = s // TM - 1
    row = lambda bi, i: (bi, i, 0)
    const = lambda bi, i: (0, 0)
    return pl.pallas_call(
        _attn_out_kernel,
        grid=(b, n_tiles),
        in_specs=[
            pl.BlockSpec((None, TM, ya.shape[-1]), row),
            pl.BlockSpec((None, TM, yb.shape[-1]), row),
            pl.BlockSpec(wa.shape, const),
            pl.BlockSpec(wb.shape, const),
            pl.BlockSpec((None, TM, d), row),
            pl.BlockSpec((None, None, 3, d), lambda bi, i: (bi, i // n_lat, 0, 0)),
            pl.BlockSpec((1, d), const),
        ],
        out_specs=pl.BlockSpec((None, TM, d), row),
        out_shape=jax.ShapeDtypeStruct((b, n_tiles * TM, d), F32),
        compiler_params=_params(("parallel", "parallel")),
        name="attn_out",
    )(ya, yb, wa, wb, xs, modl, norm_post)


def _softplus(t):
    return jnp.maximum(t, 0.0) + jnp.log(1.0 + jnp.exp(-jnp.abs(t)))


def _ssm_in_kernel(x_ref, xp_ref, xn_ref, mod_ref, np_ref, wz_ref, wx_ref, wdt_ref, wdtt_ref,
                   cw_ref, cb_ref, dtb_ref, dtbt_ref, z_ref, xbc_ref, dt_ref, dtt_ref, *, n_lat):
    i = pl.program_id(1)
    tm = TM
    mod = mod_ref[...]
    w_pre = np_ref[...]
    h = _pre_norm(x_ref[...], mod, w_pre)
    hb = h.astype(BF16)
    is_ctx = i >= n_lat
    has_prev = jnp.logical_and(i > 0, jnp.logical_not(is_ctx)).astype(F32)
    has_next = jnp.logical_and(i < n_lat - 1, jnp.logical_not(is_ctx)).astype(F32)
    hp = _pre_norm(xp_ref[...], mod, w_pre) * has_prev
    hn = _pre_norm(xn_ref[...], mod, w_pre) * has_next
    ext = jnp.concatenate([hp, h, hn], axis=0).astype(BF16)

    z_ref[...] = _silu(_dot(hb, wz_ref[...])).astype(BF16)
    dt_ref[...] = _softplus(_dot(hb, wdt_ref[...]) + dtb_ref[...])
    dtt_ref[...] = _softplus(_dot_nt(wdtt_ref[...], hb) + dtbt_ref[...])

    nblk = CONV_COLS
    nt = tm // 8
    sub = lax.broadcasted_iota(jnp.int32, (8, nblk), 0)
    for j in range(wx_ref.shape[1] // nblk):
        cs = slice(j * nblk, (j + 1) * nblk)
        u3 = _dot(ext, wx_ref[:, cs]).reshape(nt + 2, 8, nblk)
        dn = pltpu.roll(u3, 1, 1)
        upw = pltpu.roll(u3, 7, 1)
        u_prev = jnp.where(sub == 0, dn[0:nt], dn[1:nt + 1])
        u_next = jnp.where(sub == 7, upw[2:nt + 2], upw[1:nt + 1])
        conv = (cb_ref[:, cs] + cw_ref[0:1, cs] * u_prev + cw_ref[1:2, cs] * u3[1:nt + 1]
                + cw_ref[2:3, cs] * u_next)
        xbc_ref[:, cs] = _silu(conv).reshape(tm, nblk).astype(BF16)


def _ssm_in(xs, modl, norm_pre, wz, wx, wdt, wdtt, cw, cb, dtb, dtbt):
    b, s, d = xs.shape
    n_t = s // TM
    n_lat = n_t - 1
    row = lambda bi, i: (bi, i, 0)
    const = lambda bi, i: (0, 0)
    r8 = TM // 8
    n8 = s // 8
    nh2 = wdt.shape[1]
    return pl.pallas_call(
        functools.partial(_ssm_in_kernel, n_lat=n_lat),
        grid=(b, n_t),
        in_specs=[
            pl.BlockSpec((None, TM, d), row),
            pl.BlockSpec((None, 8, d), lambda bi, i: (bi, jnp.maximum(i * r8 - 1, 0), 0)),
            pl.BlockSpec((None, 8, d), lambda bi, i: (bi, jnp.minimum((i + 1) * r8, n8 - 1), 0)),
            pl.BlockSpec((None, None, 3, d), lambda bi, i: (bi, i // n_lat, 0, 0)),
            pl.BlockSpec((1, d), const),
            pl.BlockSpec(wz.shape, const),
            pl.BlockSpec(wx.shape, const),
            pl.BlockSpec(wdt.shape, const),
            pl.BlockSpec(wdtt.shape, const),
            pl.BlockSpec(cw.shape, const),
            pl.BlockSpec(cb.shape, const),
            pl.BlockSpec(dtb.shape, const),
            pl.BlockSpec(dtbt.shape, const),
        ],
        out_specs=[
            pl.BlockSpec((None, TM, wz.shape[1]), row),
            pl.BlockSpec((None, TM, wx.shape[1]), row),
            pl.BlockSpec((None, TM, nh2), row),
            pl.BlockSpec((None, nh2, TM), lambda bi, i: (bi, 0, i)),
        ],
        out_shape=[
            jax.ShapeDtypeStruct((b, s, wz.shape[1]), BF16),
            jax.ShapeDtypeStruct((b, s, wx.shape[1]), BF16),
            jax.ShapeDtypeStruct((b, s, nh2), F32),
            jax.ShapeDtypeStruct((b, nh2, s), F32),
        ],
        compiler_params=_params(("parallel", "parallel")),
        name="ssm_in",
    )(xs, xs, xs, modl, norm_pre, wz, wx, wdt, wdtt, cw, cb, dtb, dtbt)


def _split3(t):
    hi = t.astype(BF16)
    r1 = t - hi.astype(F32)
    mid = r1.astype(BF16)
    lo = (r1 - mid.astype(F32)).astype(BF16)
    return hi, mid, lo


def _ones_dot_lhs(tri01, a):
    return _dot(jnp.concatenate([tri01] * 3, axis=1), jnp.concatenate(_split3(a), axis=0))


def _ones_dot_rhs(at, tri01):
    return _dot(jnp.concatenate(_split3(at), axis=1), jnp.concatenate([tri01] * 3, axis=0))


def _ssd_direction(xbc_ref, dt, dtt, a_row, a_col, h_sc, y_ref, reverse, hoff):
    q_len = SSM_CHUNK
    d_inner = SSM_HEADS * SSM_HEAD_DIM
    gw = SSM_REP * SSM_HEAD_DIM
    row = lax.broadcasted_iota(jnp.int32, (q_len, q_len), 0)
    col = lax.broadcasted_iota(jnp.int32, (q_len, q_len), 1)
    lower = row >= col
    upper = row <= col
    mask = upper if reverse else lower
    tri = jnp.where(mask, 1.0, 0.0).astype(BF16)
    tri_t = jnp.where(lower if reverse else upper, 1.0, 0.0).astype(BF16)

    a = dt * a_row
    at = dtt * a_col
    acum = _ones_dot_lhs(tri, a)
    acum_t = _ones_dot_rhs(at, tri_t)
    total_t = jnp.sum(at, axis=1, keepdims=True)
    w_t = jnp.exp2(total_t - acum_t) * dtt
    etot = jnp.exp2(jnp.sum(a, axis=0, keepdims=True))
    head_of_col = jnp.right_shift(lax.broadcasted_iota(jnp.int32, (q_len, gw), 1),
                                  SSM_HEAD_DIM.bit_length() - 1)

    def group(g):
        b_g = xbc_ref[:, d_inner + g * D_STATE:d_inner + (g + 1) * D_STATE]
        c_g = xbc_ref[:, d_inner + (SSM_GROUPS + g) * D_STATE:d_inner + (SSM_GROUPS + g + 1) * D_STATE]
        gsl = slice(g * gw, (g + 1) * gw)
        x_g = xbc_ref[:, gsl]
        cb = _dot_nt(c_g, b_g)
        c_f = c_g.astype(F32)
        b_t = b_g.astype(F32).T
        h_g = h_sc[g]
        h_b = h_g.astype(BF16)
        ms, ces, bws, bdx, bdh = [], [], [], [], []
        etot_row = None
        for r in range(SSM_REP):
            h = hoff + g * SSM_REP + r
            acb = jnp.broadcast_to(acum[:, h:h + 1], (q_len, q_len))
            decay = jnp.exp2(jnp.where(mask, acb - acum_t[h:h + 1, :], NEG))
            ms.append((cb * decay * dtt[h:h + 1, :]).astype(BF16))
            ces.append((c_f * jnp.exp2(acb)).astype(BF16))
            bws.append((b_t * w_t[h:h + 1, :]).astype(BF16))
            sel = head_of_col == r
            bdx.append(jnp.where(sel, x_g, jnp.zeros_like(x_g)))
            bdh.append(jnp.where(sel, h_b, jnp.zeros_like(h_b)))
            e_r = jnp.broadcast_to(etot[:, h:h + 1], (1, gw))
            etot_row = e_r if etot_row is None else jnp.where(head_of_col[0:1] == r, e_r, etot_row)
        bdx = jnp.concatenate(bdx, axis=0)
        bdh = jnp.concatenate(bdh, axis=0)
        lhs = jnp.concatenate([jnp.concatenate(ms, axis=1), jnp.concatenate(bws, axis=1)], axis=0)
        res = _dot(lhs, bdx)
        y = res[0:q_len] + _dot(jnp.concatenate(ces, axis=1), bdh)
        y_ref[:, gsl] = y.astype(BF16)
        h_sc[g] = h_g * etot_row + res[q_len:]

    return group


def _ssd_kernel(xf_ref, xb_ref, dtf_ref, dtb_ref, dttf_ref, dttb_ref, alog_ref, alogt_ref,
                yf_ref, yb_ref, hf_sc, hb_sc):
    @pl.when(pl.program_id(1) == 0)
    def _():
        hf_sc[...] = jnp.zeros(hf_sc.shape, F32)
        hb_sc[...] = jnp.zeros(hb_sc.shape, F32)

    a_row = -jnp.exp(alog_ref[...]) * LOG2E
    a_col = -jnp.exp(alogt_ref[...]) * LOG2E
    fwd = _ssd_direction(xf_ref, dtf_ref[...], dttf_ref[...], a_row, a_col, hf_sc, yf_ref, False, 0)
    for g in range(SSM_GROUPS):
        fwd(g)
    bwd = _ssd_direction(xb_ref, dtb_ref[...], dttb_ref[...], a_row, a_col, hb_sc, yb_ref, True, SSM_HEADS)
    for g in range(SSM_GROUPS):
        bwd(g)


def _ssd(xbc, dt, dtt, alog, alogt, n_lat_tok):
    b, s, cw = xbc.shape
    q_len = SSM_CHUNK
    n_c = s // q_len
    n_lat = n_lat_tok // q_len
    d_inner = SSM_HEADS * SSM_HEAD_DIM
    nh2 = dt.shape[-1]
    cf = lambda j: (j + n_lat) % n_c
    cbk = lambda j: n_c - 1 - j
    const = lambda bi, j: (0, 0)
    return pl.pallas_call(
        _ssd_kernel,
        grid=(b, n_c),
        in_specs=[
            pl.BlockSpec((None, q_len, cw), lambda bi, j: (bi, cf(j), 0)),
            pl.BlockSpec((None, q_len, cw), lambda bi, j: (bi, cbk(j), 0)),
            pl.BlockSpec((None, q_len, nh2), lambda bi, j: (bi, cf(j), 0)),
            pl.BlockSpec((None, q_len, nh2), lambda bi, j: (bi, cbk(j), 0)),
            pl.BlockSpec((None, nh2, q_len), lambda bi, j: (bi, 0, cf(j))),
            pl.BlockSpec((None, nh2, q_len), lambda bi, j: (bi, 0, cbk(j))),
            pl.BlockSpec(alog.shape, const),
            pl.BlockSpec(alogt.shape, const),
        ],
        out_specs=[
            pl.BlockSpec((None, q_len, d_inner), lambda bi, j: (bi, cf(j), 0)),
            pl.BlockSpec((None, q_len, d_inner), lambda bi, j: (bi, cbk(j), 0)),
        ],
        out_shape=[jax.ShapeDtypeStruct((b, s, d_inner), BF16)] * 2,
        scratch_shapes=[pltpu.VMEM((SSM_GROUPS, D_STATE, SSM_REP * SSM_HEAD_DIM), F32)] * 2,
        compiler_params=_params(("parallel", "arbitrary")),
        name="ssd_scan",
    )(xbc, xbc, dt, dt, dtt, dtt, alog, alogt)


def _ssm_out_kernel(yf_ref, yb_ref, xs_ref, z_ref, dsk_ref, nw_ref, w_ref, x_ref, mod_ref, np_ref, o_ref):
    y = yf_ref[...].astype(F32) + yb_ref[...].astype(F32) + dsk_ref[...] * xs_ref[...].astype(F32)
    gated = y * z_ref[...].astype(F32)
    gsz = gated.shape[1] // SSM_GROUPS
    parts = []
    for g in range(SSM_GROUPS):
        t = gated[:, g * gsz:(g + 1) * gsz]
        parts.append(t * lax.rsqrt(jnp.mean(t * t, axis=-1, keepdims=True) + EPS))
    gn = (jnp.concatenate(parts, axis=1) * nw_ref[...]).astype(BF16)
    o_ref[...] = _residual(x_ref[...], _dot(gn, w_ref[...]), mod_ref[...], np_ref[...])


def _ssm_out(yf, yb, xbc, z, dsk, nw, w, xs, modl, norm_post, n_tiles):
    b, s, d = xs.shape
    n_lat = s // TM - 1
    di = yf.shape[-1]
    row = lambda bi, i: (bi, i, 0)
    const = lambda bi, i: (0, 0)
    return pl.pallas_call(
        _ssm_out_kernel,
        grid=(b, n_tiles),
        in_specs=[
            pl.BlockSpec((None, TM, di), row),
            pl.BlockSpec((None, TM, di), row),
            pl.BlockSpec((None, TM, di), row),
            pl.BlockSpec((None, TM, di), row),
            pl.BlockSpec((1, di), const),
            pl.BlockSpec((1, di), const),
            pl.BlockSpec(w.shape, const),
            pl.BlockSpec((None, TM, d), row),
            pl.BlockSpec((None, None, 3, d), lambda bi, i: (bi, i // n_lat, 0, 0)),
            pl.BlockSpec((1, d), const),
        ],
        out_specs=pl.BlockSpec((None, TM, d), row),
        out_shape=jax.ShapeDtypeStruct((b, n_tiles * TM, d), F32),
        compiler_params=_params(("parallel", "parallel")),
        name="ssm_out",
    )(yf, yb, xbc, z, dsk, nw, w, xs, modl, norm_post)


def _rope_tables(n_lat_tok, n_ctx_tok):
    t = np.arange(n_lat_tok)
    n_freq = HEAD_DIM // 4
    inv = 1.0 / (ROPE_THETA ** (jnp.arange(n_freq, dtype=F32) / n_freq))
    rowp = jnp.asarray(t // GRID_W, F32)
    colp = jnp.asarray(t % GRID_W, F32)
    ang = jnp.concatenate([rowp[:, None] * inv, colp[:, None] * inv], axis=-1)
    cos, sin = jnp.cos(ang), jnp.sin(ang)
    cos2 = jnp.concatenate([cos, cos], axis=-1)
    sin2 = jnp.concatenate([-sin, sin], axis=-1)
    cos2 = jnp.concatenate([cos2, jnp.ones((n_ctx_tok, HEAD_DIM), F32)], axis=0)
    sin2 = jnp.concatenate([sin2, jnp.zeros((n_ctx_tok, HEAD_DIM), F32)], axis=0)
    return cos2, sin2


_DEINT = np.concatenate([np.arange(0, HEAD_DIM, 2), np.arange(1, HEAD_DIM, 2)])


def _attn_weight_columns():
    qa, ka, va, ga, qb, kb, vb, gb = 0, 1024, 1280, 1536, 2560, 3584, 3840, 4096

    def heads(start, n, perm):
        base = start + HEAD_DIM * np.arange(n)[:, None]
        return (base + (_DEINT if perm else np.arange(HEAD_DIM))[None, :]).reshape(-1)

    return np.concatenate([
        heads(qa, 8, True), heads(ga, 8, False), heads(qb, 8, True), heads(gb, 8, False),
        heads(ka, 2, True), heads(va, 2, False), heads(kb, 2, True), heads(vb, 2, False)])


_ATTN_COLS_IDX = _attn_weight_columns()


def kernel(x, c, ctx, c_ctx, w_ada, b_ada, norm_pre, norm_post, attn_w_in, attn_w_out, attn_sink,
           attn_q_norm, attn_k_norm, ssm_w_in, ssm_conv_w, ssm_conv_b, ssm_dt_bias, ssm_a_log, ssm_d,
           ssm_norm, ssm_w_out):
    bsz, n_lat_tok, d = x.shape
    n_ctx_tok = ctx.shape[1]
    depth = w_ada.shape[0]
    assert n_ctx_tok == TM and n_lat_tok % TM == 0 and bsz <= 7
    d_inner = SSM_HEADS * SSM_HEAD_DIM
    bc_w = 2 * SSM_GROUPS * D_STATE

    xs = jnp.concatenate([x, ctx], axis=1)
    cc = jnp.zeros((8, d), F32).at[:bsz].set(c).at[bsz].set(c_ctx)
    mod = _modulation(cc, w_ada, b_ada)
    mod = mod.reshape(depth, 8, 3, d)
    cos2, sin2 = _rope_tables(n_lat_tok, n_ctx_tok)

    for l in range(depth):
        last = l == depth - 1
        n_tiles = (n_lat_tok if last else n_lat_tok + n_ctx_tok) // TM
        modl = jnp.stack([mod[l, :bsz], jnp.broadcast_to(mod[l, bsz], (bsz, 3, d))], axis=1)
        npre = norm_pre[l].reshape(1, d)
        npost = norm_post[l].reshape(1, d)
        i = l // 2
        if l % 2 == 0:
            w = attn_w_in[i][:, _ATTN_COLS_IDX].astype(BF16)
            qn = attn_q_norm[i][_DEINT].reshape(1, HEAD_DIM)
            kn = attn_k_norm[i][_DEINT].reshape(1, HEAD_DIM)
            qkv = _attn_in(xs, modl, npre, w, cos2, sin2, qn, kn)
            sink_b = jnp.broadcast_to(attn_sink[i][:, None], (A_Q_HEADS, HEAD_DIM))
            ya = _attn_a(qkv, sink_b, n_lat_tok)
            yb = _attn_b(qkv, n_lat_tok)
            wo = attn_w_out[i].astype(BF16)
            aq = A_Q_HEADS * HEAD_DIM
            xs = _attn_out(ya, yb, wo[:aq], wo[aq:], xs, modl, npost, n_tiles)
        else:
            w = ssm_w_in[i]
            wz = w[:, :d_inner].astype(BF16)
            wx = w[:, d_inner:2 * d_inner + bc_w].astype(BF16)
            wdt = jnp.pad(w[:, 2 * d_inner + bc_w:], ((0, 0), (0, HPAD - 2 * SSM_HEADS))).astype(BF16)
            dtb = jnp.pad(ssm_dt_bias[i].reshape(1, -1), ((0, 0), (0, HPAD - 2 * SSM_HEADS)))
            z, xbc, dt, dtt = _ssm_in(xs, modl, npre, wz, wx, wdt, wdt.T, ssm_conv_w[i],
                                      ssm_conv_b[i].reshape(1, -1), dtb, dtb.reshape(-1, 1))
            alog = jnp.pad(ssm_a_log[i].reshape(1, -1), ((0, 0), (0, HPAD - 2 * SSM_HEADS)))
            yf, ybk = _ssd(xbc, dt, dtt, alog, alog.reshape(-1, 1), n_lat_tok)
            dsk = jnp.repeat(ssm_d[i], SSM_HEAD_DIM).reshape(1, d_inner)
            xs = _ssm_out(yf, ybk, xbc, z, dsk, ssm_norm[i].reshape(1, d_inner),
                          ssm_w_out[i].astype(BF16), xs, modl, npost, n_tiles)
    return xs
```

```python
import functools
from typing import NamedTuple

import numpy as np
import jax
import jax.numpy as jnp
from jax import lax
from jax.experimental import pallas as pl
from jax.experimental.pallas import tpu as pltpu

F32 = jnp.float32
BF16 = jnp.bfloat16

EPS = 1e-6
GRID_W = 64
ROPE_THETA = 10000.0
HEAD_DIM = 128
A_Q_HEADS = 8
A_KV_HEADS = 2
B_Q_HEADS = 8
B_KV_HEADS = 2
REP = 4
WINDOW = 128
SSM_HEAD_DIM = 64
SSM_HEADS = 32
SSM_GROUPS = 8
SSM_REP = SSM_HEADS // SSM_GROUPS
D_STATE = 128
SSM_CHUNK = 128
HPAD = 128

V7X_VMEM_BYTES = 64 * 1024 * 1024
VMEM_LIMIT = V7X_VMEM_BYTES - 8 * 1024 * 1024

TM = 256
TM_OUT = 512
CONV_COLS = 512
TQ_A = 128
NB_A = 4
TQ_B = 512
TK_B = 512
LANES = 128
LOG2E = 1.4426950408889634

QA0, GA0, QB0, GB0, KA0, VA0, KB0, VB0 = 0, 1024, 2048, 3072, 4096, 4352, 4608, 4864
ATTN_COLS = 5120

NEG = -1e30


def _params(sem, vmem=VMEM_LIMIT):
    return pltpu.CompilerParams(dimension_semantics=sem, vmem_limit_bytes=vmem)


def _silu(t):
    return t * (1.0 / (1.0 + jnp.exp2(t * (-LOG2E))))


def _rms(t, w):
    return t * lax.rsqrt(jnp.mean(t * t, axis=-1, keepdims=True) + EPS) * w


def _dot(a, b):
    return jnp.dot(a, b, preferred_element_type=F32)


def _dot_nt(a, b):
    return lax.dot_general(a, b, (((1,), (1,)), ((), ())), preferred_element_type=F32)


def _dot_hi(a, b):
    return jnp.dot(a, b, preferred_element_type=F32, precision=lax.Precision.HIGHEST)


class _Part(NamedTuple):
    src: jax.Array
    off: int
    tm: int
    steps: int
    mod_row: int
    out_off: int


def _stream_parts(lat, lat_off, cx, cx_off, n_lat_tok, n_ctx_tok, tm):
    return (_Part(lat, lat_off, tm, n_lat_tok // tm, 0, 0), _Part(cx, cx_off, n_ctx_tok, 1, 1, n_lat_tok))


def _retile(parts, tm):
    lat, cx = parts
    return (lat._replace(tm=tm, steps=lat.tm * lat.steps // tm), cx)


def _rows(off, tm):
    return lambda bi, i: (bi, off // tm + i, 0)


def _per_part(parts, call):
    outs = None
    for p in parts:
        outs = call(p, outs)
    return outs


def _alias_tail(kernel_fn, n_in, prev):
    if prev is None:
        return [], [], {}, kernel_fn
    prev = list(prev) if isinstance(prev, (list, tuple)) else [prev]
    n = len(prev)

    def fn(*refs):
        return kernel_fn(*refs[:n_in], *refs[n_in + n:])

    return [pl.BlockSpec(memory_space=pl.ANY)] * n, prev, {n_in + k: k for k in range(n)}, fn


def _mod_kernel(cc_ref, w_ref, b_ref, o_ref):
    o_ref[...] = _dot_hi(_silu(cc_ref[...]), w_ref[...]) + b_ref[...]


def _modulation(cc, w_ada, b_ada):
    depth, d, d3 = w_ada.shape
    return pl.pallas_call(
        _mod_kernel,
        grid=(depth,),
        in_specs=[
            pl.BlockSpec((8, d), lambda l: (0, 0)),
            pl.BlockSpec((None, d, d3), lambda l: (l, 0, 0)),
            pl.BlockSpec((None, 1, d3), lambda l: (l, 0, 0)),
        ],
        out_specs=pl.BlockSpec((None, 8, d3), lambda l: (l, 0, 0)),
        out_shape=jax.ShapeDtypeStruct((depth, 8, d3), F32),
        compiler_params=_params(("arbitrary",)),
        name="modulation",
    )(cc, w_ada, b_ada.reshape(depth, 1, d3))


def _pre_norm(x, mod, w):
    return _rms(x, w) * (1.0 + mod[1:2]) + mod[0:1]


def _attn_in_kernel(x_ref, mod_ref, np_ref, w_ref, cos_ref, sin_ref, qn_ref, kn_ref, o_ref):
    hb = _pre_norm(x_ref[...], mod_ref[...], np_ref[...]).astype(BF16)
    cos = cos_ref[...]
    sin = sin_ref[...]
    scale = HEAD_DIM ** -0.5

    def rope(t):
        return t * cos + pltpu.roll(t, HEAD_DIM // 2, 1) * sin

    nblk = 512
    for j in range(ATTN_COLS // nblk):
        c0 = j * nblk
        t = _dot(hb, w_ref[:, c0:c0 + nblk])
        for hh in range(nblk // HEAD_DIM):
            col = c0 + hh * HEAD_DIM
            th = t[:, hh * HEAD_DIM:(hh + 1) * HEAD_DIM]
            if col < GA0:
                th = rope(th) * (scale * LOG2E)
            elif col < QB0 or GB0 <= col < KA0:
                th = _silu(th)
            elif col < GB0:
                th = rope(_rms(th, qn_ref[...])) * (scale * LOG2E)
            elif col < VA0:
                th = rope(th)
            elif KB0 <= col < VB0:
                th = rope(_rms(th, kn_ref[...]))
            o_ref[:, col:col + HEAD_DIM] = th.astype(BF16)


def _attn_in(parts, s, modl, norm_pre, w, cos2, sin2, qn, kn):
    b, d = modl.shape[0], modl.shape[-1]
    const = lambda bi, i: (0, 0)

    def call(p, prev):
        tail_specs, tail_args, aliases, fn = _alias_tail(_attn_in_kernel, 8, prev)
        table = pl.BlockSpec((p.tm, HEAD_DIM), lambda bi, i: (p.out_off // p.tm + i, 0))
        return pl.pallas_call(
            fn,
            grid=(b, p.steps),
            in_specs=[
                pl.BlockSpec((None, p.tm, d), _rows(p.off, p.tm)),
                pl.BlockSpec((None, None, 3, d), lambda bi, i: (bi, p.mod_row, 0, 0)),
                pl.BlockSpec((1, d), const),
                pl.BlockSpec((d, ATTN_COLS), const),
                table,
                table,
                pl.BlockSpec((1, HEAD_DIM), const),
                pl.BlockSpec((1, HEAD_DIM), const),
            ] + tail_specs,
            out_specs=pl.BlockSpec((None, p.tm, ATTN_COLS), _rows(p.out_off, p.tm)),
            out_shape=jax.ShapeDtypeStruct((b, s, ATTN_COLS), BF16),
            input_output_aliases=aliases,
            compiler_params=_params(("parallel", "parallel")),
            name="attn_in",
        )(p.src, modl, norm_pre, w, cos2, sin2, qn, kn, *tail_args)

    return _per_part(parts, call)


def _sink_column(sink_ref, g, n):
    return jnp.concatenate(
        [jnp.broadcast_to(sink_ref[g * REP + h:g * REP + h + 1, 0:1] * LOG2E, (n, 1)) for h in range(REP)], axis=0)


def _attn_a_kernel(q_ref, kp_ref, kc_ref, kn_ref, vp_ref, vc_ref, vn_ref, kx_ref, vx_ref,
                   g_ref, sink_ref, o_ref, *, n_lat):
    i = pl.program_id(1)
    tq = TQ_A
    rows = REP * tq
    r = lax.broadcasted_iota(jnp.int32, (rows, tq), 0) & (tq - 1)
    c = lax.broadcasted_iota(jnp.int32, (rows, tq), 1)
    for a in range(NB_A):
        blk = NB_A * i + a
        rq = slice(a * tq, (a + 1) * tq)
        m_prev = (c - r) >= jnp.where(blk > 0, 0, tq)
        m_next = (r - c) >= jnp.where(blk < n_lat - 1, 0, tq)
        for g in range(A_KV_HEADS):
            gs = slice(g * HEAD_DIM, (g + 1) * HEAD_DIM)
            if a == 0:
                k_p, v_p = kp_ref[:, gs], vp_ref[:, gs]
            else:
                k_p, v_p = kc_ref[(a - 1) * tq:a * tq, gs], vc_ref[(a - 1) * tq:a * tq, gs]
            if a == NB_A - 1:
                k_n, v_n = kn_ref[:, gs], vn_ref[:, gs]
            else:
                k_n, v_n = kc_ref[(a + 1) * tq:(a + 2) * tq, gs], vc_ref[(a + 1) * tq:(a + 2) * tq, gs]
            k_c, v_c = kc_ref[rq, gs], vc_ref[rq, gs]
            q = jnp.concatenate(
                [q_ref[rq, (g * REP + h) * HEAD_DIM:(g * REP + h + 1) * HEAD_DIM] for h in range(REP)], axis=0)
            sp = jnp.where(m_prev, _dot_nt(q, k_p), NEG)
            sc = _dot_nt(q, k_c)
            sn = jnp.where(m_next, _dot_nt(q, k_n), NEG)
            sx = _dot_nt(q, kx_ref[:, gs])
            sk = _sink_column(sink_ref, g, tq)
            sx0, sx1 = sx[:, 0:tq], sx[:, tq:2 * tq]
            m_t = jnp.maximum(jnp.maximum(jnp.maximum(sp, sc), jnp.maximum(sn, sx0)), sx1)
            m = jnp.maximum(jnp.max(m_t, axis=1, keepdims=True), sk)
            pp = jnp.exp2(sp - m)
            pc = jnp.exp2(sc - m)
            pn = jnp.exp2(sn - m)
            px = jnp.exp2(sx - m)
            den_t = (pp + pc) + (pn + px[:, 0:tq]) + px[:, tq:2 * tq]
            den = jnp.sum(den_t, axis=1, keepdims=True) + jnp.exp2(sk - m)
            o = (_dot(pp.astype(BF16), v_p) + _dot(pc.astype(BF16), v_c)
                 + _dot(pn.astype(BF16), v_n) + _dot(px.astype(BF16), vx_ref[:, gs])) * (1.0 / den)
            for h in range(REP):
                cs = slice((g * REP + h) * HEAD_DIM, (g * REP + h + 1) * HEAD_DIM)
                o_ref[rq, cs] = (o[h * tq:(h + 1) * tq] * g_ref[rq, cs].astype(F32)).astype(BF16)


def _attn_a_ctx_kernel(q_ref, kx_ref, vx_ref, g_ref, sink_ref, y_hbm_ref, o_ref):
    del y_hbm_ref
    n = q_ref.shape[0]
    for g in range(A_KV_HEADS):
        gs = slice(g * HEAD_DIM, (g + 1) * HEAD_DIM)
        q = jnp.concatenate(
            [q_ref[:, (g * REP + h) * HEAD_DIM:(g * REP + h + 1) * HEAD_DIM] for h in range(REP)], axis=0)
        sx = _dot_nt(q, kx_ref[:, gs])
        sk = _sink_column(sink_ref, g, n)
        m = jnp.maximum(jnp.max(sx, axis=1, keepdims=True), sk)
        px = jnp.exp2(sx - m)
        den = jnp.sum(px, axis=1, keepdims=True) + jnp.exp2(sk - m)
        o = _dot(px.astype(BF16), vx_ref[:, gs]) * (1.0 / den)
        for h in range(REP):
            cs = slice((g * REP + h) * HEAD_DIM, (g * REP + h + 1) * HEAD_DIM)
            o_ref[:, cs] = (o[h * n:(h + 1) * n] * g_ref[:, cs].astype(F32)).astype(BF16)


def _attn_a(qkv, sink_b, n_lat_tok):
    b, s, _ = qkv.shape
    tq = TQ_A
    n_blk = s // tq
    n_lat = n_lat_tok // tq
    kvw = A_KV_HEADS * HEAD_DIM
    ctx_len = s - n_lat_tok
    step = NB_A * tq
    assert ctx_len == 2 * tq and n_lat_tok % ctx_len == 0 and n_lat_tok % step == 0
    qw = A_Q_HEADS * HEAD_DIM
    prev = lambda cb: (lambda bi, i: (bi, jnp.maximum(NB_A * i - 1, 0), cb))
    cur = lambda cb: (lambda bi, i: (bi, i, cb))
    nxt = lambda cb: (lambda bi, i: (bi, jnp.minimum(NB_A * (i + 1), n_blk - 1), cb))
    cblk = n_lat_tok // ctx_len
    ctx = lambda cb: (lambda bi, i: (bi, cblk, cb))
    kb, vb = KA0 // kvw, VA0 // kvw
    sink_spec = pl.BlockSpec((A_Q_HEADS, HEAD_DIM), lambda bi, i: (0, 0))
    y_lat = pl.pallas_call(
        functools.partial(_attn_a_kernel, n_lat=n_lat),
        grid=(b, n_lat_tok // step),
        in_specs=[
            pl.BlockSpec((None, step, qw), cur(QA0 // qw)),
            pl.BlockSpec((None, tq, kvw), prev(kb)),
            pl.BlockSpec((None, step, kvw), cur(kb)),
            pl.BlockSpec((None, tq, kvw), nxt(kb)),
            pl.BlockSpec((None, tq, kvw), prev(vb)),
            pl.BlockSpec((None, step, kvw), cur(vb)),
            pl.BlockSpec((None, tq, kvw), nxt(vb)),
            pl.BlockSpec((None, ctx_len, kvw), ctx(kb)),
            pl.BlockSpec((None, ctx_len, kvw), ctx(vb)),
            pl.BlockSpec((None, step, qw), cur(GA0 // qw)),
            sink_spec,
        ],
        out_specs=pl.BlockSpec((None, step, qw), lambda bi, i: (bi, i, 0)),
        out_shape=jax.ShapeDtypeStruct((b, s, qw), BF16),
        compiler_params=_params(("parallel", "parallel")),
        name="attn_window",
    )(qkv, qkv, qkv, qkv, qkv, qkv, qkv, qkv, qkv, qkv, sink_b)
    return pl.pallas_call(
        _attn_a_ctx_kernel,
        grid=(b, 1),
        in_specs=[
            pl.BlockSpec((None, ctx_len, qw), ctx(QA0 // qw)),
            pl.BlockSpec((None, ctx_len, kvw), ctx(kb)),
            pl.BlockSpec((None, ctx_len, kvw), ctx(vb)),
            pl.BlockSpec((None, ctx_len, qw), ctx(GA0 // qw)),
            sink_spec,
            pl.BlockSpec(memory_space=pl.ANY),
        ],
        out_specs=pl.BlockSpec((None, ctx_len, qw), lambda bi, i: (bi, cblk, 0)),
        out_shape=jax.ShapeDtypeStruct(y_lat.shape, BF16),
        input_output_aliases={5: 0},
        compiler_params=_params(("parallel", "parallel")),
        name="attn_window_ctx",
    )(qkv, qkv, qkv, qkv, sink_b, y_lat)


def _attn_b_kernel(q_ref, k_ref, v_ref, g_ref, o_ref, m_sc, l_sc, acc_sc, p_sc, alpha_sc, *, n_lat_tok):
    tq, tk = TQ_B, TK_B
    n_ctx_tok = k_ref.shape[0] - n_lat_tok
    q = jnp.concatenate([q_ref[:, h * HEAD_DIM:(h + 1) * HEAD_DIM] for h in range(REP)], axis=0)
    m_sc[...] = jnp.full(m_sc.shape, NEG, F32)
    l_sc[...] = jnp.zeros(l_sc.shape, F32)
    acc_sc[...] = jnp.zeros(acc_sc.shape, F32)

    rb = 128

    def scores(k, slot):
        nk = k.shape[0]
        s = _dot_nt(q, k)
        for r0 in range(0, REP * tq, rb):
            rs = slice(r0, r0 + rb)
            s_b = s[rs]
            m_old = m_sc[rs]
            m_new = jnp.maximum(m_old, jnp.max(s_b, axis=1, keepdims=True))
            alpha = jnp.exp2(m_old - m_new)
            p = jnp.exp2(s_b - jnp.tile(m_new, (1, nk // LANES)))
            l_sc[rs] = alpha * l_sc[rs] + jnp.sum(p, axis=1, keepdims=True)
            m_sc[rs] = m_new
            alpha_sc[slot, rs] = alpha
            p_sc[slot, rs, 0:nk] = p.astype(BF16)

    def accumulate(slot, v):
        nk = v.shape[0]
        acc_sc[...] = alpha_sc[slot] * acc_sc[...] + _dot(p_sc[slot, :, 0:nk], v)

    def kv(ref, ci):
        if isinstance(ci, int):
            return ref[ci * tk:(ci + 1) * tk, :]
        return ref[pl.ds(pl.multiple_of(ci * tk, tk), tk), :]

    n_main = n_lat_tok // tk
    assert n_main % 2 == 0 and n_lat_tok % tk == 0
    scores(kv(k_ref, 0), 0)

    def pair(c1):
        scores(kv(k_ref, c1), 1)
        accumulate(0, kv(v_ref, c1 - 1))
        scores(kv(k_ref, c1 + 1), 0)
        accumulate(1, kv(v_ref, c1))

    def body(j, carry):
        pair(4 * j + 1)
        pair(4 * j + 3)
        return carry

    n_quads = (n_main - 2) // 4
    lax.fori_loop(0, n_quads, body, 0)
    for c1 in range(4 * n_quads + 1, n_main - 1, 2):
        pair(c1)
    scores(kv(k_ref, n_main - 1), 1)
    accumulate(0, kv(v_ref, n_main - 2))
    scores(k_ref[n_lat_tok:n_lat_tok + n_ctx_tok, :], 0)
    accumulate(1, kv(v_ref, n_main - 1))
    accumulate(0, v_ref[n_lat_tok:n_lat_tok + n_ctx_tok, :])

    o = acc_sc[...] * (1.0 / l_sc[...])
    for h in range(REP):
        cs = slice(h * HEAD_DIM, (h + 1) * HEAD_DIM)
        o_ref[:, cs] = (o[h * tq:(h + 1) * tq] * g_ref[:, cs].astype(F32)).astype(BF16)


def _attn_b_ctx_kernel(q_ref, k_ref, v_ref, g_ref, y_hbm_ref, o_ref):
    del y_hbm_ref
    n = q_ref.shape[0]
    q = jnp.concatenate([q_ref[:, h * HEAD_DIM:(h + 1) * HEAD_DIM] for h in range(REP)], axis=0)
    s = _dot_nt(q, k_ref[...])
    p = jnp.exp2(s - jnp.max(s, axis=1, keepdims=True))
    o = _dot(p.astype(BF16), v_ref[...]) * (1.0 / jnp.sum(p, axis=1, keepdims=True))
    for h in range(REP):
        cs = slice(h * HEAD_DIM, (h + 1) * HEAD_DIM)
        o_ref[:, cs] = (o[h * n:(h + 1) * n] * g_ref[:, cs].astype(F32)).astype(BF16)


def _attn_b(qkv, n_lat_tok):
    b, s, _ = qkv.shape
    tq = TQ_B
    gw = REP * HEAD_DIM
    n_ctx_tok = s - n_lat_tok
    assert n_lat_tok % tq == 0 and n_lat_tok % n_ctx_tok == 0
    y_lat = pl.pallas_call(
        functools.partial(_attn_b_kernel, n_lat_tok=n_lat_tok),
        grid=(b, B_KV_HEADS, n_lat_tok // tq),
        in_specs=[
            pl.BlockSpec((None, tq, gw), lambda bi, g, i: (bi, i, QB0 // gw + g)),
            pl.BlockSpec((None, s, HEAD_DIM), lambda bi, g, i: (bi, 0, KB0 // HEAD_DIM + g)),
            pl.BlockSpec((None, s, HEAD_DIM), lambda bi, g, i: (bi, 0, VB0 // HEAD_DIM + g)),
            pl.BlockSpec((None, tq, gw), lambda bi, g, i: (bi, i, GB0 // gw + g)),
        ],
        out_specs=pl.BlockSpec((None, tq, gw), lambda bi, g, i: (bi, i, g)),
        out_shape=jax.ShapeDtypeStruct((b, s, B_Q_HEADS * HEAD_DIM), BF16),
        scratch_shapes=[
            pltpu.VMEM((REP * tq, LANES), F32),
            pltpu.VMEM((REP * tq, LANES), F32),
            pltpu.VMEM((REP * tq, HEAD_DIM), F32),
            pltpu.VMEM((2, REP * tq, TK_B), BF16),
            pltpu.VMEM((2, REP * tq, LANES), F32),
        ],
        compiler_params=_params(("parallel", "parallel", "parallel")),
        name="attn_dense",
    )(qkv, qkv, qkv, qkv)
    cblk = n_lat_tok // n_ctx_tok
    return pl.pallas_call(
        _attn_b_ctx_kernel,
        grid=(b, B_KV_HEADS),
        in_specs=[
            pl.BlockSpec((None, n_ctx_tok, gw), lambda bi, g: (bi, cblk, QB0 // gw + g)),
            pl.BlockSpec((None, n_ctx_tok, HEAD_DIM), lambda bi, g: (bi, cblk, KB0 // HEAD_DIM + g)),
            pl.BlockSpec((None, n_ctx_tok, HEAD_DIM), lambda bi, g: (bi, cblk, VB0 // HEAD_DIM + g)),
            pl.BlockSpec((None, n_ctx_tok, gw), lambda bi, g: (bi, cblk, GB0 // gw + g)),
            pl.BlockSpec(memory_space=pl.ANY),
        ],
        out_specs=pl.BlockSpec((None, n_ctx_tok, gw), lambda bi, g: (bi, cblk, g)),
        out_shape=jax.ShapeDtypeStruct(y_lat.shape, BF16),
        input_output_aliases={4: 0},
        compiler_params=_params(("parallel", "parallel")),
        name="attn_dense_ctx",
    )(qkv, qkv, qkv, qkv, y_lat)


def _residual(x, y, mod, w_post):
    return x + mod[2:3] * _rms(y, w_post)


def _attn_out_kernel(ya_ref, yb_ref, wa_ref, wb_ref, x_ref, mod_ref, np_ref, o_ref):
    y = _dot(ya_ref[...], wa_ref[...]) + _dot(yb_ref[...], wb_ref[...])
    o_ref[...] = _residual(x_ref[...], y, mod_ref[...], np_ref[...])


def _attn_out(parts, s_out, ya, yb, wa, wb, modl, norm_post):
    b, d = modl.shape[0], modl.shape[-1]
    const = lambda bi, i: (0, 0)

    def call(p, prev):
        tail_specs, tail_args, aliases, fn = _alias_tail(_attn_out_kernel, 7, prev)
        return pl.pallas_call(
            fn,
            grid=(b, p.steps),
            in_specs=[
                pl.BlockSpec((None, p.tm, ya.shape[-1]), _rows(p.out_off, p.tm)),
                pl.BlockSpec((None, p.tm, yb.shape[-1]), _rows(p.out_off, p.tm)),
                pl.BlockSpec(wa.shape, const),
                pl.BlockSpec(wb.shape, const),
                pl.BlockSpec((None, p.tm, d), _rows(p.off, p.tm)),
                pl.BlockSpec((None, None, 3, d), lambda bi, i: (bi, p.mod_row, 0, 0)),
                pl.BlockSpec((1, d), const),
            ] + tail_specs,
            out_specs=pl.BlockSpec((None, p.tm, d), _rows(p.out_off, p.tm)),
            out_shape=jax.ShapeDtypeStruct((b, s_out, d), F32),
            input_output_aliases=aliases,
            compiler_params=_params(("parallel", "parallel")),
            name="attn_out",
        )(ya, yb, wa, wb, p.src, modl, norm_post, *tail_args)

    return _per_part(parts, call)


def _softplus(t):
    return jnp.maximum(t, 0.0) + jnp.log(1.0 + jnp.exp(-jnp.abs(t)))


def _ssm_in_kernel(x_ref, xp_ref, xn_ref, mod_ref, np_ref, wz_ref, wx_ref, wdt_ref, wdtt_ref,
                   cw_ref, cb_ref, dtb_ref, dtbt_ref, z_ref, xbc_ref, dt_ref, dtt_ref, *, n_steps):
    i = pl.program_id(1)
    tm = x_ref.shape[0]
    mod = mod_ref[...]
    w_pre = np_ref[...]
    h = _pre_norm(x_ref[...], mod, w_pre)
    hb = h.astype(BF16)
    has_prev = (i > 0).astype(F32)
    has_next = (i < n_steps - 1).astype(F32)
    hp = _pre_norm(xp_ref[...], mod, w_pre) * has_prev
    hn = _pre_norm(xn_ref[...], mod, w_pre) * has_next
    ext = jnp.concatenate([hp, h, hn], axis=0).astype(BF16)

    z_ref[...] = _silu(_dot(hb, wz_ref[...])).astype(BF16)
    dt_ref[...] = _softplus(_dot(hb, wdt_ref[...]) + dtb_ref[...])
    dtt_ref[...] = _softplus(_dot_nt(wdtt_ref[...], hb) + dtbt_ref[...])

    nblk = CONV_COLS
    nt = tm // 8
    sub = lax.broadcasted_iota(jnp.int32, (8, nblk), 0)
    for j in range(wx_ref.shape[1] // nblk):
        cs = slice(j * nblk, (j + 1) * nblk)
        u3 = _dot(ext, wx_ref[:, cs]).reshape(nt + 2, 8, nblk)
        dn = pltpu.roll(u3, 1, 1)
        upw = pltpu.roll(u3, 7, 1)
        u_prev = jnp.where(sub == 0, dn[0:nt], dn[1:nt + 1])
        u_next = jnp.where(sub == 7, upw[2:nt + 2], upw[1:nt + 1])
        conv = (cb_ref[:, cs] + cw_ref[0:1, cs] * u_prev + cw_ref[1:2, cs] * u3[1:nt + 1]
                + cw_ref[2:3, cs] * u_next)
        xbc_ref[:, cs] = _silu(conv).reshape(tm, nblk).astype(BF16)


def _ssm_in(parts, s, modl, norm_pre, wz, wx, wdt, wdtt, cw, cb, dtb, dtbt):
    b, d = modl.shape[0], modl.shape[-1]
    const = lambda bi, i: (0, 0)
    nh2 = wdt.shape[1]

    def call(p, prev):
        tail_specs, tail_args, aliases, fn = _alias_tail(
            functools.partial(_ssm_in_kernel, n_steps=p.steps), 13, prev)
        r8 = p.tm // 8
        base8 = p.off // 8
        last8 = p.src.shape[1] // 8 - 1
        col = p.out_off // p.tm
        return pl.pallas_call(
            fn,
            grid=(b, p.steps),
            in_specs=[
                pl.BlockSpec((None, p.tm, d), _rows(p.off, p.tm)),
                pl.BlockSpec((None, 8, d), lambda bi, i: (bi, jnp.maximum(base8 + i * r8 - 1, 0), 0)),
                pl.BlockSpec((None, 8, d), lambda bi, i: (bi, jnp.minimum(base8 + (i + 1) * r8, last8), 0)),
                pl.BlockSpec((None, None, 3, d), lambda bi, i: (bi, p.mod_row, 0, 0)),
                pl.BlockSpec((1, d), const),
                pl.BlockSpec(wz.shape, const),
                pl.BlockSpec(wx.shape, const),
                pl.BlockSpec(wdt.shape, const),
                pl.BlockSpec(wdtt.shape, const),
                pl.BlockSpec(cw.shape, const),
                pl.BlockSpec(cb.shape, const),
                pl.BlockSpec(dtb.shape, const),
                pl.BlockSpec(dtbt.shape, const),
            ] + tail_specs,
            out_specs=[
                pl.BlockSpec((None, p.tm, wz.shape[1]), _rows(p.out_off, p.tm)),
                pl.BlockSpec((None, p.tm, wx.shape[1]), _rows(p.out_off, p.tm)),
                pl.BlockSpec((None, p.tm, nh2), _rows(p.out_off, p.tm)),
                pl.BlockSpec((None, nh2, p.tm), lambda bi, i: (bi, 0, col + i)),
            ],
            out_shape=[
                jax.ShapeDtypeStruct((b, s, wz.shape[1]), BF16),
                jax.ShapeDtypeStruct((b, s, wx.shape[1]), BF16),
                jax.ShapeDtypeStruct((b, s, nh2), F32),
                jax.ShapeDtypeStruct((b, nh2, s), F32),
            ],
            input_output_aliases=aliases,
            compiler_params=_params(("parallel", "parallel")),
            name="ssm_in",
        )(p.src, p.src, p.src, modl, norm_pre, wz, wx, wdt, wdtt, cw, cb, dtb, dtbt, *tail_args)

    return _per_part(parts, call)


def _split3(t):
    hi = t.astype(BF16)
    r1 = t - hi.astype(F32)
    mid = r1.astype(BF16)
    lo = (r1 - mid.astype(F32)).astype(BF16)
    return hi, mid, lo


def _ones_dot_lhs(tri01, a):
    return _dot(jnp.concatenate([tri01] * 3, axis=1), jnp.concatenate(_split3(a), axis=0))


def _ones_dot_rhs(at, tri01):
    return _dot(jnp.concatenate(_split3(at), axis=1), jnp.concatenate([tri01] * 3, axis=0))


def _ssd_direction(xbc_ref, dt, dtt, a_row, a_col, h_sc, y_ref, reverse, hoff):
    q_len = SSM_CHUNK
    d_inner = SSM_HEADS * SSM_HEAD_DIM
    gw = SSM_REP * SSM_HEAD_DIM
    row = lax.broadcasted_iota(jnp.int32, (q_len, q_len), 0)
    col = lax.broadcasted_iota(jnp.int32, (q_len, q_len), 1)
    lower = row >= col
    upper = row <= col
    mask = upper if reverse else lower
    tri = jnp.where(mask, 1.0, 0.0).astype(BF16)
    tri_t = jnp.where(lower if reverse else upper, 1.0, 0.0).astype(BF16)

    a = dt * a_row
    at = dtt * a_col
    acum = _ones_dot_lhs(tri, a)
    acum_t = _ones_dot_rhs(at, tri_t)
    total_t = jnp.sum(at, axis=1, keepdims=True)
    acum_t = acum_t - jnp.log2(dtt)
    w_t = jnp.exp2(total_t - acum_t)
    etot = jnp.exp2(jnp.sum(a, axis=0, keepdims=True))
    head_of_col = jnp.right_shift(lax.broadcasted_iota(jnp.int32, (q_len, gw), 1),
                                  SSM_HEAD_DIM.bit_length() - 1)

    def group(g):
        b_g = xbc_ref[:, d_inner + g * D_STATE:d_inner + (g + 1) * D_STATE]
        c_g = xbc_ref[:, d_inner + (SSM_GROUPS + g) * D_STATE:d_inner + (SSM_GROUPS + g + 1) * D_STATE]
        gsl = slice(g * gw, (g + 1) * gw)
        x_g = xbc_ref[:, gsl]
        cb = _dot_nt(c_g, b_g)
        c_f = c_g.astype(F32)
        b_t = b_g.astype(F32).T
        h_g = h_sc[g]
        h_b = h_g.astype(BF16)
        ms, ces, bws, bdx, bdh = [], [], [], [], []
        etot_row = None
        for r in range(SSM_REP):
            h = hoff + g * SSM_REP + r
            acb = jnp.broadcast_to(acum[:, h:h + 1], (q_len, q_len))
            decay = jnp.exp2(jnp.where(mask, acb - acum_t[h:h + 1, :], NEG))
            ms.append((cb * decay).astype(BF16))
            ces.append((c_f * jnp.exp2(acb)).astype(BF16))
            bws.append((b_t * w_t[h:h + 1, :]).astype(BF16))
            sel = head_of_col == r
            bdx.append(jnp.where(sel, x_g, jnp.zeros_like(x_g)))
            bdh.append(jnp.where(sel, h_b, jnp.zeros_like(h_b)))
            e_r = jnp.broadcast_to(etot[:, h:h + 1], (1, gw))
            etot_row = e_r if etot_row is None else jnp.where(head_of_col[0:1] == r, e_r, etot_row)
        bdx = jnp.concatenate(bdx, axis=0)
        bdh = jnp.concatenate(bdh, axis=0)
        lhs = jnp.concatenate([jnp.concatenate(ms, axis=1), jnp.concatenate(bws, axis=1)], axis=0)
        res = _dot(lhs, bdx)
        y = res[0:q_len] + _dot(jnp.concatenate(ces, axis=1), bdh)
        y_ref[:, gsl] = y.astype(BF16)
        h_sc[g] = h_g * etot_row + res[q_len:]

    return group


def _ssd_kernel(xf_ref, xb_ref, dtf_ref, dtb_ref, dttf_ref, dttb_ref, alog_ref, alogt_ref,
                yf_ref, yb_ref, hf_sc, hb_sc):
    @pl.when(pl.program_id(1) == 0)
    def _():
        hf_sc[...] = jnp.zeros(hf_sc.shape, F32)
        hb_sc[...] = jnp.zeros(hb_sc.shape, F32)

    a_row = -jnp.exp(alog_ref[...]) * LOG2E
    a_col = -jnp.exp(alogt_ref[...]) * LOG2E
    fwd = _ssd_direction(xf_ref, dtf_ref[...], dttf_ref[...], a_row, a_col, hf_sc, yf_ref, False, 0)
    for g in range(SSM_GROUPS):
        fwd(g)
    bwd = _ssd_direction(xb_ref, dtb_ref[...], dttb_ref[...], a_row, a_col, hb_sc, yb_ref, True, SSM_HEADS)
    for g in range(SSM_GROUPS):
        bwd(g)


def _ssd(xbc, dt, dtt, alog, alogt, n_lat_tok):
    b, s, cw = xbc.shape
    q_len = SSM_CHUNK
    n_c = s // q_len
    n_lat = n_lat_tok // q_len
    d_inner = SSM_HEADS * SSM_HEAD_DIM
    nh2 = dt.shape[-1]
    cf = lambda j: (j + n_lat) % n_c
    cbk = lambda j: n_c - 1 - j
    const = lambda bi, j: (0, 0)
    return pl.pallas_call(
        _ssd_kernel,
        grid=(b, n_c),
        in_specs=[
            pl.BlockSpec((None, q_len, cw), lambda bi, j: (bi, cf(j), 0)),
            pl.BlockSpec((None, q_len, cw), lambda bi, j: (bi, cbk(j), 0)),
            pl.BlockSpec((None, q_len, nh2), lambda bi, j: (bi, cf(j), 0)),
            pl.BlockSpec((None, q_len, nh2), lambda bi, j: (bi, cbk(j), 0)),
            pl.BlockSpec((None, nh2, q_len), lambda bi, j: (bi, 0, cf(j))),
            pl.BlockSpec((None, nh2, q_len), lambda bi, j: (bi, 0, cbk(j))),
            pl.BlockSpec(alog.shape, const),
            pl.BlockSpec(alogt.shape, const),
        ],
        out_specs=[
            pl.BlockSpec((None, q_len, d_inner), lambda bi, j: (bi, cf(j), 0)),
            pl.BlockSpec((None, q_len, d_inner), lambda bi, j: (bi, cbk(j), 0)),
        ],
        out_shape=[jax.ShapeDtypeStruct((b, s, d_inner), BF16)] * 2,
        scratch_shapes=[pltpu.VMEM((SSM_GROUPS, D_STATE, SSM_REP * SSM_HEAD_DIM), F32)] * 2,
        compiler_params=_params(("parallel", "arbitrary")),
        name="ssd_scan",
    )(xbc, xbc, dt, dt, dtt, dtt, alog, alogt)


def _ssm_out_kernel(yf_ref, yb_ref, xs_ref, z_ref, dsk_ref, nw_ref, w_ref, x_ref, mod_ref, np_ref, o_ref):
    y = yf_ref[...].astype(F32) + yb_ref[...].astype(F32) + dsk_ref[...] * xs_ref[...].astype(F32)
    gated = y * z_ref[...].astype(F32)
    gsz = gated.shape[1] // SSM_GROUPS
    parts = []
    for g in range(SSM_GROUPS):
        t = gated[:, g * gsz:(g + 1) * gsz]
        parts.append(t * lax.rsqrt(jnp.mean(t * t, axis=-1, keepdims=True) + EPS))
    gn = (jnp.concatenate(parts, axis=1) * nw_ref[...]).astype(BF16)
    o_ref[...] = _residual(x_ref[...], _dot(gn, w_ref[...]), mod_ref[...], np_ref[...])


def _ssm_out(parts, s_out, yf, yb, xbc, z, dsk, nw, w, modl, norm_post):
    b, d = modl.shape[0], modl.shape[-1]
    di = yf.shape[-1]
    const = lambda bi, i: (0, 0)

    def call(p, prev):
        tail_specs, tail_args, aliases, fn = _alias_tail(_ssm_out_kernel, 10, prev)
        wide = pl.BlockSpec((None, p.tm, di), _rows(p.out_off, p.tm))
        return pl.pallas_call(
            fn,
            grid=(b, p.steps),
            in_specs=[
                wide, wide, wide, wide,
                pl.BlockSpec((1, di), const),
                pl.BlockSpec((1, di), const),
                pl.BlockSpec(w.shape, const),
                pl.BlockSpec((None, p.tm, d), _rows(p.off, p.tm)),
                pl.BlockSpec((None, None, 3, d), lambda bi, i: (bi, p.mod_row, 0, 0)),
                pl.BlockSpec((1, d), const),
            ] + tail_specs,
            out_specs=pl.BlockSpec((None, p.tm, d), _rows(p.out_off, p.tm)),
            out_shape=jax.ShapeDtypeStruct((b, s_out, d), F32),
            input_output_aliases=aliases,
            compiler_params=_params(("parallel", "parallel")),
            name="ssm_out",
        )(yf, yb, xbc, z, dsk, nw, w, p.src, modl, norm_post, *tail_args)

    return _per_part(parts, call)


def _rope_tables(n_lat_tok, n_ctx_tok):
    t = np.arange(n_lat_tok)
    n_freq = HEAD_DIM // 4
    inv = 1.0 / (ROPE_THETA ** (jnp.arange(n_freq, dtype=F32) / n_freq))
    rowp = jnp.asarray(t // GRID_W, F32)
    colp = jnp.asarray(t % GRID_W, F32)
    ang = jnp.concatenate([rowp[:, None] * inv, colp[:, None] * inv], axis=-1)
    cos, sin = jnp.cos(ang), jnp.sin(ang)
    cos2 = jnp.concatenate([cos, cos], axis=-1)
    sin2 = jnp.concatenate([-sin, sin], axis=-1)
    cos2 = jnp.concatenate([cos2, jnp.ones((n_ctx_tok, HEAD_DIM), F32)], axis=0)
    sin2 = jnp.concatenate([sin2, jnp.zeros((n_ctx_tok, HEAD_DIM), F32)], axis=0)
    return cos2, sin2


_DEINT = np.concatenate([np.arange(0, HEAD_DIM, 2), np.arange(1, HEAD_DIM, 2)])


def _attn_weight_columns():
    qa, ka, va, ga, qb, kb, vb, gb = 0, 1024, 1280, 1536, 2560, 3584, 3840, 4096

    def heads(start, n, perm):
        base = start + HEAD_DIM * np.arange(n)[:, None]
        return (base + (_DEINT if perm else np.arange(HEAD_DIM))[None, :]).reshape(-1)

    return np.concatenate([
        heads(qa, 8, True), heads(ga, 8, False), heads(qb, 8, True), heads(gb, 8, False),
        heads(ka, 2, True), heads(va, 2, False), heads(kb, 2, True), heads(vb, 2, False)])


_ATTN_COLS_IDX = _attn_weight_columns()


def kernel(x, c, ctx, c_ctx, w_ada, b_ada, norm_pre, norm_post, attn_w_in, attn_w_out, attn_sink,
           attn_q_norm, attn_k_norm, ssm_w_in, ssm_conv_w, ssm_conv_b, ssm_dt_bias, ssm_a_log, ssm_d,
           ssm_norm, ssm_w_out):
    bsz, n_lat_tok, d = x.shape
    n_ctx_tok = ctx.shape[1]
    depth = w_ada.shape[0]
    assert n_lat_tok % TM == 0 and n_lat_tok % n_ctx_tok == 0 and n_ctx_tok % LANES == 0 and bsz <= 7
    s = n_lat_tok + n_ctx_tok
    d_inner = SSM_HEADS * SSM_HEAD_DIM
    bc_w = 2 * SSM_GROUPS * D_STATE

    parts = _stream_parts(x, 0, ctx, 0, n_lat_tok, n_ctx_tok, TM)
    cc = jnp.zeros((8, d), F32).at[:bsz].set(c).at[bsz].set(c_ctx)
    mod = _modulation(cc, w_ada, b_ada)
    mod = mod.reshape(depth, 8, 3, d)
    cos2, sin2 = _rope_tables(n_lat_tok, n_ctx_tok)

    for l in range(depth):
        last = l == depth - 1
        out_parts = _retile(parts, TM_OUT)
        out_parts = out_parts[:1] if last else out_parts
        s_out = n_lat_tok if last else s
        modl = jnp.stack([mod[l, :bsz], jnp.broadcast_to(mod[l, bsz], (bsz, 3, d))], axis=1)
        npre = norm_pre[l].reshape(1, d)
        npost = norm_post[l].reshape(1, d)
        i = l // 2
        if l % 2 == 0:
            w = attn_w_in[i][:, _ATTN_COLS_IDX].astype(BF16)
            qn = attn_q_norm[i][_DEINT].reshape(1, HEAD_DIM)
            kn = attn_k_norm[i][_DEINT].reshape(1, HEAD_DIM)
            qkv = _attn_in(parts, s, modl, npre, w, cos2, sin2, qn, kn)
            sink_b = jnp.broadcast_to(attn_sink[i][:, None], (A_Q_HEADS, HEAD_DIM))
            ya = _attn_a(qkv, sink_b, n_lat_tok)
            yb = _attn_b(qkv, n_lat_tok)
            wo = attn_w_out[i].astype(BF16)
            aq = A_Q_HEADS * HEAD_DIM
            xs = _attn_out(out_parts, s_out, ya, yb, wo[:aq], wo[aq:], modl, npost)
        else:
            w = ssm_w_in[i]
            wz = w[:, :d_inner].astype(BF16)
            wx = w[:, d_inner:2 * d_inner + bc_w].astype(BF16)
            wdt = jnp.pad(w[:, 2 * d_inner + bc_w:], ((0, 0), (0, HPAD - 2 * SSM_HEADS))).astype(BF16)
            dtb = jnp.pad(ssm_dt_bias[i].reshape(1, -1), ((0, 0), (0, HPAD - 2 * SSM_HEADS)))
            z, xbc, dt, dtt = _ssm_in(parts, s, modl, npre, wz, wx, wdt, wdt.T, ssm_conv_w[i],
                                      ssm_conv_b[i].reshape(1, -1), dtb, dtb.reshape(-1, 1))
            alog = jnp.pad(ssm_a_log[i].reshape(1, -1), ((0, 0), (0, HPAD - 2 * SSM_HEADS)))
            yf, ybk = _ssd(xbc, dt, dtt, alog, alog.reshape(-1, 1), n_lat_tok)
            dsk = jnp.repeat(ssm_d[i], SSM_HEAD_DIM).reshape(1, d_inner)
            xs = _ssm_out(out_parts, s_out, yf, ybk, xbc, z, dsk, ssm_norm[i].reshape(1, d_inner),
                          ssm_w_out[i].astype(BF16), modl, npost)
        parts = _stream_parts(xs, 0, xs, n_lat_tok, n_lat_tok, n_ctx_tok, TM)
    return xs
```

```python
import functools
from typing import NamedTuple

import numpy as np
import jax
import jax.numpy as jnp
from jax import lax
from jax.experimental import pallas as pl
from jax.experimental.pallas import tpu as pltpu

F32 = jnp.float32
BF16 = jnp.bfloat16

EPS = 1e-6
GRID_W = 64
ROPE_THETA = 10000.0
HEAD_DIM = 128
A_Q_HEADS = 8
A_KV_HEADS = 2
B_Q_HEADS = 8
B_KV_HEADS = 2
REP = 4
WINDOW = 128
SSM_HEAD_DIM = 64
SSM_HEADS = 32
SSM_GROUPS = 8
SSM_REP = SSM_HEADS // SSM_GROUPS
D_STATE = 128
SSM_CHUNK = 128
SSD_SUB = 2
HPAD = 128

V7X_VMEM_BYTES = 64 * 1024 * 1024
VMEM_LIMIT = V7X_VMEM_BYTES - 8 * 1024 * 1024

TM = 256
TM_OUT = 512
CONV_COLS = 512
TQ_A = 128
NB_A = 4
TQ_B = 512
TK_B = 512
LANES = 128
LOG2E = 1.4426950408889634

QA0, GA0, QB0, GB0, KA0, VA0, KB0, VB0 = 0, 1024, 2048, 3072, 4096, 4352, 4608, 4864
ATTN_COLS = 5120

NEG = -1e30


def _params(sem, vmem=VMEM_LIMIT):
    return pltpu.CompilerParams(dimension_semantics=sem, vmem_limit_bytes=vmem)


def _silu(t):
    return t * (1.0 / (1.0 + jnp.exp2(t * (-LOG2E))))


def _rms(t, w):
    return t * lax.rsqrt(jnp.mean(t * t, axis=-1, keepdims=True) + EPS) * w


def _dot(a, b):
    return jnp.dot(a, b, preferred_element_type=F32)


def _dot_nt(a, b):
    return lax.dot_general(a, b, (((1,), (1,)), ((), ())), preferred_element_type=F32)


def _dot_hi(a, b):
    return jnp.dot(a, b, preferred_element_type=F32, precision=lax.Precision.HIGHEST)


class _Part(NamedTuple):
    src: jax.Array
    off: int
    tm: int
    steps: int
    mod_row: int
    out_off: int


def _stream_parts(lat, lat_off, cx, cx_off, n_lat_tok, n_ctx_tok, tm):
    return (_Part(lat, lat_off, tm, n_lat_tok // tm, 0, 0), _Part(cx, cx_off, n_ctx_tok, 1, 1, n_lat_tok))


def _retile(parts, tm):
    lat, cx = parts
    return (lat._replace(tm=tm, steps=lat.tm * lat.steps // tm), cx)


def _rows(off, tm):
    return lambda bi, i: (bi, off // tm + i, 0)


def _per_part(parts, call):
    outs = None
    for p in parts:
        outs = call(p, outs)
    return outs


def _alias_tail(kernel_fn, n_in, prev):
    if prev is None:
        return [], [], {}, kernel_fn
    prev = list(prev) if isinstance(prev, (list, tuple)) else [prev]
    n = len(prev)

    def fn(*refs):
        return kernel_fn(*refs[:n_in], *refs[n_in + n:])

    return [pl.BlockSpec(memory_space=pl.ANY)] * n, prev, {n_in + k: k for k in range(n)}, fn


def _mod_kernel(cc_ref, w_ref, b_ref, o_ref):
    o_ref[...] = _dot_hi(_silu(cc_ref[...]), w_ref[...]) + b_ref[...]


def _modulation(cc, w_ada, b_ada):
    depth, d, d3 = w_ada.shape
    return pl.pallas_call(
        _mod_kernel,
        grid=(depth,),
        in_specs=[
            pl.BlockSpec((8, d), lambda l: (0, 0)),
            pl.BlockSpec((None, d, d3), lambda l: (l, 0, 0)),
            pl.BlockSpec((None, 1, d3), lambda l: (l, 0, 0)),
        ],
        out_specs=pl.BlockSpec((None, 8, d3), lambda l: (l, 0, 0)),
        out_shape=jax.ShapeDtypeStruct((depth, 8, d3), F32),
        compiler_params=_params(("arbitrary",)),
        name="modulation",
    )(cc, w_ada, b_ada.reshape(depth, 1, d3))


def _pre_norm(x, mod, w):
    return _rms(x, w) * (1.0 + mod[1:2]) + mod[0:1]


def _attn_in_kernel(x_ref, mod_ref, np_ref, w_ref, cos_ref, sin_ref, qn_ref, kn_ref, o_ref):
    hb = _pre_norm(x_ref[...], mod_ref[...], np_ref[...]).astype(BF16)
    cos = cos_ref[...]
    sin = sin_ref[...]
    scale = HEAD_DIM ** -0.5

    def rope(t):
        return t * cos + pltpu.roll(t, HEAD_DIM // 2, 1) * sin

    nblk = 512
    for j in range(ATTN_COLS // nblk):
        c0 = j * nblk
        t = _dot(hb, w_ref[:, c0:c0 + nblk])
        for hh in range(nblk // HEAD_DIM):
            col = c0 + hh * HEAD_DIM
            th = t[:, hh * HEAD_DIM:(hh + 1) * HEAD_DIM]
            if col < GA0:
                th = rope(th) * (scale * LOG2E)
            elif col < QB0 or GB0 <= col < KA0:
                th = _silu(th)
            elif col < GB0:
                th = rope(_rms(th, qn_ref[...])) * (scale * LOG2E)
            elif col < VA0:
                th = rope(th)
            elif KB0 <= col < VB0:
                th = rope(_rms(th, kn_ref[...]))
            o_ref[:, col:col + HEAD_DIM] = th.astype(BF16)


def _attn_in(parts, s, modl, norm_pre, w, cos2, sin2, qn, kn):
    b, d = modl.shape[0], modl.shape[-1]
    const = lambda bi, i: (0, 0)

    def call(p, prev):
        tail_specs, tail_args, aliases, fn = _alias_tail(_attn_in_kernel, 8, prev)
        table = pl.BlockSpec((p.tm, HEAD_DIM), lambda bi, i: (p.out_off // p.tm + i, 0))
        return pl.pallas_call(
            fn,
            grid=(b, p.steps),
            in_specs=[
                pl.BlockSpec((None, p.tm, d), _rows(p.off, p.tm)),
                pl.BlockSpec((None, None, 3, d), lambda bi, i: (bi, p.mod_row, 0, 0)),
                pl.BlockSpec((1, d), const),
                pl.BlockSpec((d, ATTN_COLS), const),
                table,
                table,
                pl.BlockSpec((1, HEAD_DIM), const),
                pl.BlockSpec((1, HEAD_DIM), const),
            ] + tail_specs,
            out_specs=pl.BlockSpec((None, p.tm, ATTN_COLS), _rows(p.out_off, p.tm)),
            out_shape=jax.ShapeDtypeStruct((b, s, ATTN_COLS), BF16),
            input_output_aliases=aliases,
            compiler_params=_params(("parallel", "parallel")),
            name="attn_in",
        )(p.src, modl, norm_pre, w, cos2, sin2, qn, kn, *tail_args)

    return _per_part(parts, call)


def _sink_column(sink_ref, g, n):
    return jnp.concatenate(
        [jnp.broadcast_to(sink_ref[g * REP + h:g * REP + h + 1, 0:1] * LOG2E, (n, 1)) for h in range(REP)], axis=0)


def _attn_a_kernel(q_ref, kp_ref, kc_ref, kn_ref, vp_ref, vc_ref, vn_ref, kx_ref, vx_ref,
                   g_ref, sink_ref, o_ref, *, n_lat):
    i = pl.program_id(1)
    tq = TQ_A
    rows = REP * tq
    r = lax.broadcasted_iota(jnp.int32, (rows, tq), 0) & (tq - 1)
    c = lax.broadcasted_iota(jnp.int32, (rows, tq), 1)
    for a in range(NB_A):
        blk = NB_A * i + a
        rq = slice(a * tq, (a + 1) * tq)
        m_prev = (c - r) >= jnp.where(blk > 0, 0, tq)
        m_next = (r - c) >= jnp.where(blk < n_lat - 1, 0, tq)
        for g in range(A_KV_HEADS):
            gs = slice(g * HEAD_DIM, (g + 1) * HEAD_DIM)
            if a == 0:
                k_p, v_p = kp_ref[:, gs], vp_ref[:, gs]
            else:
                k_p, v_p = kc_ref[(a - 1) * tq:a * tq, gs], vc_ref[(a - 1) * tq:a * tq, gs]
            if a == NB_A - 1:
                k_n, v_n = kn_ref[:, gs], vn_ref[:, gs]
            else:
                k_n, v_n = kc_ref[(a + 1) * tq:(a + 2) * tq, gs], vc_ref[(a + 1) * tq:(a + 2) * tq, gs]
            k_c, v_c = kc_ref[rq, gs], vc_ref[rq, gs]
            q = jnp.concatenate(
                [q_ref[rq, (g * REP + h) * HEAD_DIM:(g * REP + h + 1) * HEAD_DIM] for h in range(REP)], axis=0)
            sp = jnp.where(m_prev, _dot_nt(q, k_p), NEG)
            sc = _dot_nt(q, k_c)
            sn = jnp.where(m_next, _dot_nt(q, k_n), NEG)
            sx = _dot_nt(q, kx_ref[:, gs])
            sk = _sink_column(sink_ref, g, tq)
            sx0, sx1 = sx[:, 0:tq], sx[:, tq:2 * tq]
            m_t = jnp.maximum(jnp.maximum(jnp.maximum(sp, sc), jnp.maximum(sn, sx0)), sx1)
            m = jnp.maximum(jnp.max(m_t, axis=1, keepdims=True), sk)
            pp = jnp.exp2(sp - m)
            pc = jnp.exp2(sc - m)
            pn = jnp.exp2(sn - m)
            px = jnp.exp2(sx - m)
            den_t = (pp + pc) + (pn + px[:, 0:tq]) + px[:, tq:2 * tq]
            den = jnp.sum(den_t, axis=1, keepdims=True) + jnp.exp2(sk - m)
            o = (_dot(pp.astype(BF16), v_p) + _dot(pc.astype(BF16), v_c)
                 + _dot(pn.astype(BF16), v_n) + _dot(px.astype(BF16), vx_ref[:, gs])) * (1.0 / den)
            for h in range(REP):
                cs = slice((g * REP + h) * HEAD_DIM, (g * REP + h + 1) * HEAD_DIM)
                o_ref[rq, cs] = (o[h * tq:(h + 1) * tq] * g_ref[rq, cs].astype(F32)).astype(BF16)


def _attn_a_ctx_kernel(q_ref, kx_ref, vx_ref, g_ref, sink_ref, y_hbm_ref, o_ref):
    del y_hbm_ref
    n = q_ref.shape[0]
    for g in range(A_KV_HEADS):
        gs = slice(g * HEAD_DIM, (g + 1) * HEAD_DIM)
        q = jnp.concatenate(
            [q_ref[:, (g * REP + h) * HEAD_DIM:(g * REP + h + 1) * HEAD_DIM] for h in range(REP)], axis=0)
        sx = _dot_nt(q, kx_ref[:, gs])
        sk = _sink_column(sink_ref, g, n)
        m = jnp.maximum(jnp.max(sx, axis=1, keepdims=True), sk)
        px = jnp.exp2(sx - m)
        den = jnp.sum(px, axis=1, keepdims=True) + jnp.exp2(sk - m)
        o = _dot(px.astype(BF16), vx_ref[:, gs]) * (1.0 / den)
        for h in range(REP):
            cs = slice((g * REP + h) * HEAD_DIM, (g * REP + h + 1) * HEAD_DIM)
            o_ref[:, cs] = (o[h * n:(h + 1) * n] * g_ref[:, cs].astype(F32)).astype(BF16)


def _attn_a(qkv, sink_b, n_lat_tok):
    b, s, _ = qkv.shape
    tq = TQ_A
    n_blk = s // tq
    n_lat = n_lat_tok // tq
    kvw = A_KV_HEADS * HEAD_DIM
    ctx_len = s - n_lat_tok
    step = NB_A * tq
    assert ctx_len == 2 * tq and n_lat_tok % ctx_len == 0 and n_lat_tok % step == 0
    qw = A_Q_HEADS * HEAD_DIM
    prev = lambda cb: (lambda bi, i: (bi, jnp.maximum(NB_A * i - 1, 0), cb))
    cur = lambda cb: (lambda bi, i: (bi, i, cb))
    nxt = lambda cb: (lambda bi, i: (bi, jnp.minimum(NB_A * (i + 1), n_blk - 1), cb))
    cblk = n_lat_tok // ctx_len
    ctx = lambda cb: (lambda bi, i: (bi, cblk, cb))
    kb, vb = KA0 // kvw, VA0 // kvw
    sink_spec = pl.BlockSpec((A_Q_HEADS, HEAD_DIM), lambda bi, i: (0, 0))
    y_lat = pl.pallas_call(
        functools.partial(_attn_a_kernel, n_lat=n_lat),
        grid=(b, n_lat_tok // step),
        in_specs=[
            pl.BlockSpec((None, step, qw), cur(QA0 // qw)),
            pl.BlockSpec((None, tq, kvw), prev(kb)),
            pl.BlockSpec((None, step, kvw), cur(kb)),
            pl.BlockSpec((None, tq, kvw), nxt(kb)),
            pl.BlockSpec((None, tq, kvw), prev(vb)),
            pl.BlockSpec((None, step, kvw), cur(vb)),
            pl.BlockSpec((None, tq, kvw), nxt(vb)),
            pl.BlockSpec((None, ctx_len, kvw), ctx(kb)),
            pl.BlockSpec((None, ctx_len, kvw), ctx(vb)),
            pl.BlockSpec((None, step, qw), cur(GA0 // qw)),
            sink_spec,
        ],
        out_specs=pl.BlockSpec((None, step, qw), lambda bi, i: (bi, i, 0)),
        out_shape=jax.ShapeDtypeStruct((b, s, qw), BF16),
        compiler_params=_params(("parallel", "parallel")),
        name="attn_window",
    )(qkv, qkv, qkv, qkv, qkv, qkv, qkv, qkv, qkv, qkv, sink_b)
    return pl.pallas_call(
        _attn_a_ctx_kernel,
        grid=(b, 1),
        in_specs=[
            pl.BlockSpec((None, ctx_len, qw), ctx(QA0 // qw)),
            pl.BlockSpec((None, ctx_len, kvw), ctx(kb)),
            pl.BlockSpec((None, ctx_len, kvw), ctx(vb)),
            pl.BlockSpec((None, ctx_len, qw), ctx(GA0 // qw)),
            sink_spec,
            pl.BlockSpec(memory_space=pl.ANY),
        ],
        out_specs=pl.BlockSpec((None, ctx_len, qw), lambda bi, i: (bi, cblk, 0)),
        out_shape=jax.ShapeDtypeStruct(y_lat.shape, BF16),
        input_output_aliases={5: 0},
        compiler_params=_params(("parallel", "parallel")),
        name="attn_window_ctx",
    )(qkv, qkv, qkv, qkv, sink_b, y_lat)


def _attn_b_kernel(q_ref, k_ref, v_ref, g_ref, o_ref, m_sc, l_sc, acc_sc, p_sc, alpha_sc, *, n_lat_tok):
    tq, tk = TQ_B, TK_B
    n_ctx_tok = k_ref.shape[0] - n_lat_tok
    q = jnp.concatenate([q_ref[:, h * HEAD_DIM:(h + 1) * HEAD_DIM] for h in range(REP)], axis=0)
    m_sc[...] = jnp.full(m_sc.shape, NEG, F32)
    l_sc[...] = jnp.zeros(l_sc.shape, F32)
    acc_sc[...] = jnp.zeros(acc_sc.shape, F32)

    rb = 128

    def scores(k, slot):
        nk = k.shape[0]
        s = _dot_nt(q, k)
        for r0 in range(0, REP * tq, rb):
            rs = slice(r0, r0 + rb)
            s_b = s[rs]
            m_old = m_sc[rs]
            m_new = jnp.maximum(m_old, jnp.max(s_b, axis=1, keepdims=True))
            alpha = jnp.exp2(m_old - m_new)
            p = jnp.exp2(s_b - jnp.tile(m_new, (1, nk // LANES)))
            l_sc[rs] = alpha * l_sc[rs] + jnp.sum(p, axis=1, keepdims=True)
            m_sc[rs] = m_new
            alpha_sc[slot, rs] = alpha
            p_sc[slot, rs, 0:nk] = p.astype(BF16)

    def accumulate(slot, v):
        nk = v.shape[0]
        acc_sc[...] = alpha_sc[slot] * acc_sc[...] + _dot(p_sc[slot, :, 0:nk], v)

    def kv(ref, ci):
        if isinstance(ci, int):
            return ref[ci * tk:(ci + 1) * tk, :]
        return ref[pl.ds(pl.multiple_of(ci * tk, tk), tk), :]

    n_main = n_lat_tok // tk
    assert n_main % 2 == 0 and n_lat_tok % tk == 0
    scores(kv(k_ref, 0), 0)

    def pair(c1):
        scores(kv(k_ref, c1), 1)
        accumulate(0, kv(v_ref, c1 - 1))
        scores(kv(k_ref, c1 + 1), 0)
        accumulate(1, kv(v_ref, c1))

    def body(j, carry):
        pair(4 * j + 1)
        pair(4 * j + 3)
        return carry

    n_quads = (n_main - 2) // 4
    lax.fori_loop(0, n_quads, body, 0)
    for c1 in range(4 * n_quads + 1, n_main - 1, 2):
        pair(c1)
    scores(kv(k_ref, n_main - 1), 1)
    accumulate(0, kv(v_ref, n_main - 2))
    scores(k_ref[n_lat_tok:n_lat_tok + n_ctx_tok, :], 0)
    accumulate(1, kv(v_ref, n_main - 1))
    accumulate(0, v_ref[n_lat_tok:n_lat_tok + n_ctx_tok, :])

    o = acc_sc[...] * (1.0 / l_sc[...])
    for h in range(REP):
        cs = slice(h * HEAD_DIM, (h + 1) * HEAD_DIM)
        o_ref[:, cs] = (o[h * tq:(h + 1) * tq] * g_ref[:, cs].astype(F32)).astype(BF16)


def _attn_b_ctx_kernel(q_ref, k_ref, v_ref, g_ref, y_hbm_ref, o_ref):
    del y_hbm_ref
    n = q_ref.shape[0]
    q = jnp.concatenate([q_ref[:, h * HEAD_DIM:(h + 1) * HEAD_DIM] for h in range(REP)], axis=0)
    s = _dot_nt(q, k_ref[...])
    p = jnp.exp2(s - jnp.max(s, axis=1, keepdims=True))
    o = _dot(p.astype(BF16), v_ref[...]) * (1.0 / jnp.sum(p, axis=1, keepdims=True))
    for h in range(REP):
        cs = slice(h * HEAD_DIM, (h + 1) * HEAD_DIM)
        o_ref[:, cs] = (o[h * n:(h + 1) * n] * g_ref[:, cs].astype(F32)).astype(BF16)


def _attn_b(qkv, n_lat_tok):
    b, s, _ = qkv.shape
    tq = TQ_B
    gw = REP * HEAD_DIM
    n_ctx_tok = s - n_lat_tok
    assert n_lat_tok % tq == 0 and n_lat_tok % n_ctx_tok == 0
    y_lat = pl.pallas_call(
        functools.partial(_attn_b_kernel, n_lat_tok=n_lat_tok),
        grid=(b, B_KV_HEADS, n_lat_tok // tq),
        in_specs=[
            pl.BlockSpec((None, tq, gw), lambda bi, g, i: (bi, i, QB0 // gw + g)),
            pl.BlockSpec((None, s, HEAD_DIM), lambda bi, g, i: (bi, 0, KB0 // HEAD_DIM + g)),
            pl.BlockSpec((None, s, HEAD_DIM), lambda bi, g, i: (bi, 0, VB0 // HEAD_DIM + g)),
            pl.BlockSpec((None, tq, gw), lambda bi, g, i: (bi, i, GB0 // gw + g)),
        ],
        out_specs=pl.BlockSpec((None, tq, gw), lambda bi, g, i: (bi, i, g)),
        out_shape=jax.ShapeDtypeStruct((b, s, B_Q_HEADS * HEAD_DIM), BF16),
        scratch_shapes=[
            pltpu.VMEM((REP * tq, LANES), F32),
            pltpu.VMEM((REP * tq, LANES), F32),
            pltpu.VMEM((REP * tq, HEAD_DIM), F32),
            pltpu.VMEM((2, REP * tq, TK_B), BF16),
            pltpu.VMEM((2, REP * tq, LANES), F32),
        ],
        compiler_params=_params(("parallel", "parallel", "parallel")),
        name="attn_dense",
    )(qkv, qkv, qkv, qkv)
    cblk = n_lat_tok // n_ctx_tok
    return pl.pallas_call(
        _attn_b_ctx_kernel,
        grid=(b, B_KV_HEADS),
        in_specs=[
            pl.BlockSpec((None, n_ctx_tok, gw), lambda bi, g: (bi, cblk, QB0 // gw + g)),
            pl.BlockSpec((None, n_ctx_tok, HEAD_DIM), lambda bi, g: (bi, cblk, KB0 // HEAD_DIM + g)),
            pl.BlockSpec((None, n_ctx_tok, HEAD_DIM), lambda bi, g: (bi, cblk, VB0 // HEAD_DIM + g)),
            pl.BlockSpec((None, n_ctx_tok, gw), lambda bi, g: (bi, cblk, GB0 // gw + g)),
            pl.BlockSpec(memory_space=pl.ANY),
        ],
        out_specs=pl.BlockSpec((None, n_ctx_tok, gw), lambda bi, g: (bi, cblk, g)),
        out_shape=jax.ShapeDtypeStruct(y_lat.shape, BF16),
        input_output_aliases={4: 0},
        compiler_params=_params(("parallel", "parallel")),
        name="attn_dense_ctx",
    )(qkv, qkv, qkv, qkv, y_lat)


def _residual(x, y, mod, w_post):
    return x + mod[2:3] * _rms(y, w_post)


def _attn_out_kernel(ya_ref, yb_ref, wa_ref, wb_ref, x_ref, mod_ref, np_ref, o_ref):
    y = _dot(ya_ref[...], wa_ref[...]) + _dot(yb_ref[...], wb_ref[...])
    o_ref[...] = _residual(x_ref[...], y, mod_ref[...], np_ref[...])


def _attn_out(parts, s_out, ya, yb, wa, wb, modl, norm_post):
    b, d = modl.shape[0], modl.shape[-1]
    const = lambda bi, i: (0, 0)

    def call(p, prev):
        tail_specs, tail_args, aliases, fn = _alias_tail(_attn_out_kernel, 7, prev)
        return pl.pallas_call(
            fn,
            grid=(b, p.steps),
            in_specs=[
                pl.BlockSpec((None, p.tm, ya.shape[-1]), _rows(p.out_off, p.tm)),
                pl.BlockSpec((None, p.tm, yb.shape[-1]), _rows(p.out_off, p.tm)),
                pl.BlockSpec(wa.shape, const),
                pl.BlockSpec(wb.shape, const),
                pl.BlockSpec((None, p.tm, d), _rows(p.off, p.tm)),
                pl.BlockSpec((None, None, 3, d), lambda bi, i: (bi, p.mod_row, 0, 0)),
                pl.BlockSpec((1, d), const),
            ] + tail_specs,
            out_specs=pl.BlockSpec((None, p.tm, d), _rows(p.out_off, p.tm)),
            out_shape=jax.ShapeDtypeStruct((b, s_out, d), F32),
            input_output_aliases=aliases,
            compiler_params=_params(("parallel", "parallel")),
            name="attn_out",
        )(ya, yb, wa, wb, p.src, modl, norm_post, *tail_args)

    return _per_part(parts, call)


def _softplus(t):
    return jnp.maximum(t, 0.0) + jnp.log(1.0 + jnp.exp(-jnp.abs(t)))


def _ssm_in_kernel(x_ref, xp_ref, xn_ref, mod_ref, np_ref, wz_ref, wx_ref, wdt_ref, wdtt_ref,
                   cw_ref, cb_ref, dtb_ref, dtbt_ref, z_ref, xbc_ref, dt_ref, dtt_ref, *, n_steps):
    i = pl.program_id(1)
    tm = x_ref.shape[0]
    mod = mod_ref[...]
    w_pre = np_ref[...]
    h = _pre_norm(x_ref[...], mod, w_pre)
    hb = h.astype(BF16)
    has_prev = (i > 0).astype(F32)
    has_next = (i < n_steps - 1).astype(F32)
    hp = _pre_norm(xp_ref[...], mod, w_pre) * has_prev
    hn = _pre_norm(xn_ref[...], mod, w_pre) * has_next
    ext = jnp.concatenate([hp, h, hn], axis=0).astype(BF16)

    z_ref[...] = _silu(_dot(hb, wz_ref[...])).astype(BF16)
    dt_ref[...] = _softplus(_dot(hb, wdt_ref[...]) + dtb_ref[...])
    dtt_ref[...] = _softplus(_dot_nt(wdtt_ref[...], hb) + dtbt_ref[...])

    nblk = CONV_COLS
    nt = tm // 8
    sub = lax.broadcasted_iota(jnp.int32, (8, nblk), 0)
    for j in range(wx_ref.shape[1] // nblk):
        cs = slice(j * nblk, (j + 1) * nblk)
        u3 = _dot(ext, wx_ref[:, cs]).reshape(nt + 2, 8, nblk)
        dn = pltpu.roll(u3, 1, 1)
        upw = pltpu.roll(u3, 7, 1)
        u_prev = jnp.where(sub == 0, dn[0:nt], dn[1:nt + 1])
        u_next = jnp.where(sub == 7, upw[2:nt + 2], upw[1:nt + 1])
        conv = (cb_ref[:, cs] + cw_ref[0:1, cs] * u_prev + cw_ref[1:2, cs] * u3[1:nt + 1]
                + cw_ref[2:3, cs] * u_next)
        xbc_ref[:, cs] = _silu(conv).reshape(tm, nblk).astype(BF16)


def _ssm_in(parts, s, modl, norm_pre, wz, wx, wdt, wdtt, cw, cb, dtb, dtbt):
    b, d = modl.shape[0], modl.shape[-1]
    const = lambda bi, i: (0, 0)
    nh2 = wdt.shape[1]

    def call(p, prev):
        tail_specs, tail_args, aliases, fn = _alias_tail(
            functools.partial(_ssm_in_kernel, n_steps=p.steps), 13, prev)
        r8 = p.tm // 8
        base8 = p.off // 8
        last8 = p.src.shape[1] // 8 - 1
        col = p.out_off // p.tm
        return pl.pallas_call(
            fn,
            grid=(b, p.steps),
            in_specs=[
                pl.BlockSpec((None, p.tm, d), _rows(p.off, p.tm)),
                pl.BlockSpec((None, 8, d), lambda bi, i: (bi, jnp.maximum(base8 + i * r8 - 1, 0), 0)),
                pl.BlockSpec((None, 8, d), lambda bi, i: (bi, jnp.minimum(base8 + (i + 1) * r8, last8), 0)),
                pl.BlockSpec((None, None, 3, d), lambda bi, i: (bi, p.mod_row, 0, 0)),
                pl.BlockSpec((1, d), const),
                pl.BlockSpec(wz.shape, const),
                pl.BlockSpec(wx.shape, const),
                pl.BlockSpec(wdt.shape, const),
                pl.BlockSpec(wdtt.shape, const),
                pl.BlockSpec(cw.shape, const),
                pl.BlockSpec(cb.shape, const),
                pl.BlockSpec(dtb.shape, const),
                pl.BlockSpec(dtbt.shape, const),
            ] + tail_specs,
            out_specs=[
                pl.BlockSpec((None, p.tm, wz.shape[1]), _rows(p.out_off, p.tm)),
                pl.BlockSpec((None, p.tm, wx.shape[1]), _rows(p.out_off, p.tm)),
                pl.BlockSpec((None, p.tm, nh2), _rows(p.out_off, p.tm)),
                pl.BlockSpec((None, nh2, p.tm), lambda bi, i: (bi, 0, col + i)),
            ],
            out_shape=[
                jax.ShapeDtypeStruct((b, s, wz.shape[1]), BF16),
                jax.ShapeDtypeStruct((b, s, wx.shape[1]), BF16),
                jax.ShapeDtypeStruct((b, s, nh2), F32),
                jax.ShapeDtypeStruct((b, nh2, s), F32),
            ],
            input_output_aliases=aliases,
            compiler_params=_params(("parallel", "parallel")),
            name="ssm_in",
        )(p.src, p.src, p.src, modl, norm_pre, wz, wx, wdt, wdtt, cw, cb, dtb, dtbt, *tail_args)

    return _per_part(parts, call)


def _split3(t):
    hi = t.astype(BF16)
    r1 = t - hi.astype(F32)
    mid = r1.astype(BF16)
    lo = (r1 - mid.astype(F32)).astype(BF16)
    return hi, mid, lo


def _ones_dot_lhs(tri01, a):
    return _dot(jnp.concatenate([tri01] * 3, axis=1), jnp.concatenate(_split3(a), axis=0))


def _ones_dot_rhs(at, tri01):
    return _dot(jnp.concatenate(_split3(at), axis=1), jnp.concatenate([tri01] * 3, axis=0))


def _ssd_direction(xbc_ref, rs, dt, dtt, a_row, a_col, h_sc, y_ref, reverse, hoff):
    q_len = SSM_CHUNK
    d_inner = SSM_HEADS * SSM_HEAD_DIM
    gw = SSM_REP * SSM_HEAD_DIM
    row = lax.broadcasted_iota(jnp.int32, (q_len, q_len), 0)
    col = lax.broadcasted_iota(jnp.int32, (q_len, q_len), 1)
    lower = row >= col
    upper = row <= col
    mask = upper if reverse else lower
    tri = jnp.where(mask, 1.0, 0.0).astype(BF16)
    tri_t = jnp.where(lower if reverse else upper, 1.0, 0.0).astype(BF16)

    a = dt * a_row
    at = dtt * a_col
    acum = _ones_dot_lhs(tri, a)
    acum_t = _ones_dot_rhs(at, tri_t)
    total_t = jnp.sum(at, axis=1, keepdims=True)
    acum_t = acum_t - jnp.log2(dtt)
    w_t = jnp.exp2(total_t - acum_t)
    etot = jnp.exp2(jnp.sum(a, axis=0, keepdims=True))
    head_of_col = jnp.right_shift(lax.broadcasted_iota(jnp.int32, (q_len, gw), 1),
                                  SSM_HEAD_DIM.bit_length() - 1)

    def group(g):
        b_g = xbc_ref[rs, d_inner + g * D_STATE:d_inner + (g + 1) * D_STATE]
        c_g = xbc_ref[rs, d_inner + (SSM_GROUPS + g) * D_STATE:d_inner + (SSM_GROUPS + g + 1) * D_STATE]
        gsl = slice(g * gw, (g + 1) * gw)
        x_g = xbc_ref[rs, gsl]
        cb = _dot_nt(c_g, b_g)
        c_f = c_g.astype(F32)
        b_t = b_g.astype(F32).T
        h_g = h_sc[g]
        h_b = h_g.astype(BF16)
        ms, ces, bws, bdx, bdh = [], [], [], [], []
        etot_row = None
        for r in range(SSM_REP):
            h = hoff + g * SSM_REP + r
            acb = jnp.broadcast_to(acum[:, h:h + 1], (q_len, q_len))
            decay = jnp.exp2(jnp.where(mask, acb - acum_t[h:h + 1, :], NEG))
            ms.append((cb * decay).astype(BF16))
            ces.append((c_f * jnp.exp2(acb)).astype(BF16))
            bws.append((b_t * w_t[h:h + 1, :]).astype(BF16))
            sel = head_of_col == r
            bdx.append(jnp.where(sel, x_g, jnp.zeros_like(x_g)))
            bdh.append(jnp.where(sel, h_b, jnp.zeros_like(h_b)))
            e_r = jnp.broadcast_to(etot[:, h:h + 1], (1, gw))
            etot_row = e_r if etot_row is None else jnp.where(head_of_col[0:1] == r, e_r, etot_row)
        bdx = jnp.concatenate(bdx, axis=0)
        bdh = jnp.concatenate(bdh, axis=0)
        lhs = jnp.concatenate([jnp.concatenate(ms, axis=1), jnp.concatenate(bws, axis=1)], axis=0)
        res = _dot(lhs, bdx)
        y = res[0:q_len] + _dot(jnp.concatenate(ces, axis=1), bdh)
        y_ref[rs, gsl] = y.astype(BF16)
        h_sc[g] = h_g * etot_row + res[q_len:]

    return group


def _ssd_kernel(xf_ref, xb_ref, dtf_ref, dtb_ref, dttf_ref, dttb_ref, alog_ref, alogt_ref,
                yf_ref, yb_ref, hf_sc, hb_sc):
    @pl.when(pl.program_id(1) == 0)
    def _():
        hf_sc[...] = jnp.zeros(hf_sc.shape, F32)
        hb_sc[...] = jnp.zeros(hb_sc.shape, F32)

    a_row = -jnp.exp(alog_ref[...]) * LOG2E
    a_col = -jnp.exp(alogt_ref[...]) * LOG2E
    for sub in range(SSD_SUB):
        rs = slice(sub * SSM_CHUNK, (sub + 1) * SSM_CHUNK)
        fwd = _ssd_direction(xf_ref, rs, dtf_ref[rs, :], dttf_ref[:, rs], a_row, a_col, hf_sc, yf_ref, False, 0)
        for g in range(SSM_GROUPS):
            fwd(g)
    for sub in reversed(range(SSD_SUB)):
        rs = slice(sub * SSM_CHUNK, (sub + 1) * SSM_CHUNK)
        bwd = _ssd_direction(xb_ref, rs, dtb_ref[rs, :], dttb_ref[:, rs], a_row, a_col, hb_sc, yb_ref, True,
                             SSM_HEADS)
        for g in range(SSM_GROUPS):
            bwd(g)


def _ssd(xbc, dt, dtt, alog, alogt, n_lat_tok):
    b, s, cw = xbc.shape
    q_len = SSD_SUB * SSM_CHUNK
    assert n_lat_tok % q_len == 0 and (s - n_lat_tok) % q_len == 0
    n_c = s // q_len
    n_lat = n_lat_tok // q_len
    d_inner = SSM_HEADS * SSM_HEAD_DIM
    nh2 = dt.shape[-1]
    cf = lambda j: (j + n_lat) % n_c
    cbk = lambda j: n_c - 1 - j
    const = lambda bi, j: (0, 0)
    return pl.pallas_call(
        _ssd_kernel,
        grid=(b, n_c),
        in_specs=[
            pl.BlockSpec((None, q_len, cw), lambda bi, j: (bi, cf(j), 0)),
            pl.BlockSpec((None, q_len, cw), lambda bi, j: (bi, cbk(j), 0)),
            pl.BlockSpec((None, q_len, nh2), lambda bi, j: (bi, cf(j), 0)),
            pl.BlockSpec((None, q_len, nh2), lambda bi, j: (bi, cbk(j), 0)),
            pl.BlockSpec((None, nh2, q_len), lambda bi, j: (bi, 0, cf(j))),
            pl.BlockSpec((None, nh2, q_len), lambda bi, j: (bi, 0, cbk(j))),
            pl.BlockSpec(alog.shape, const),
            pl.BlockSpec(alogt.shape, const),
        ],
        out_specs=[
            pl.BlockSpec((None, q_len, d_inner), lambda bi, j: (bi, cf(j), 0)),
            pl.BlockSpec((None, q_len, d_inner), lambda bi, j: (bi, cbk(j), 0)),
        ],
        out_shape=[jax.ShapeDtypeStruct((b, s, d_inner), BF16)] * 2,
        scratch_shapes=[pltpu.VMEM((SSM_GROUPS, D_STATE, SSM_REP * SSM_HEAD_DIM), F32)] * 2,
        compiler_params=_params(("parallel", "arbitrary")),
        name="ssd_scan",
    )(xbc, xbc, dt, dt, dtt, dtt, alog, alogt)


def _ssm_out_kernel(yf_ref, yb_ref, xs_ref, z_ref, dsk_ref, nw_ref, w_ref, x_ref, mod_ref, np_ref, o_ref):
    y = yf_ref[...].astype(F32) + yb_ref[...].astype(F32) + dsk_ref[...] * xs_ref[...].astype(F32)
    gated = y * z_ref[...].astype(F32)
    gsz = gated.shape[1] // SSM_GROUPS
    parts = []
    for g in range(SSM_GROUPS):
        t = gated[:, g * gsz:(g + 1) * gsz]
        parts.append(t * lax.rsqrt(jnp.mean(t * t, axis=-1, keepdims=True) + EPS))
    gn = (jnp.concatenate(parts, axis=1) * nw_ref[...]).astype(BF16)
    o_ref[...] = _residual(x_ref[...], _dot(gn, w_ref[...]), mod_ref[...], np_ref[...])


def _ssm_out(parts, s_out, yf, yb, xbc, z, dsk, nw, w, modl, norm_post):
    b, d = modl.shape[0], modl.shape[-1]
    di = yf.shape[-1]
    const = lambda bi, i: (0, 0)

    def call(p, prev):
        tail_specs, tail_args, aliases, fn = _alias_tail(_ssm_out_kernel, 10, prev)
        wide = pl.BlockSpec((None, p.tm, di), _rows(p.out_off, p.tm))
        return pl.pallas_call(
            fn,
            grid=(b, p.steps),
            in_specs=[
                wide, wide, wide, wide,
                pl.BlockSpec((1, di), const),
                pl.BlockSpec((1, di), const),
                pl.BlockSpec(w.shape, const),
                pl.BlockSpec((None, p.tm, d), _rows(p.off, p.tm)),
                pl.BlockSpec((None, None, 3, d), lambda bi, i: (bi, p.mod_row, 0, 0)),
                pl.BlockSpec((1, d), const),
            ] + tail_specs,
            out_specs=pl.BlockSpec((None, p.tm, d), _rows(p.out_off, p.tm)),
            out_shape=jax.ShapeDtypeStruct((b, s_out, d), F32),
            input_output_aliases=aliases,
            compiler_params=_params(("parallel", "parallel")),
            name="ssm_out",
        )(yf, yb, xbc, z, dsk, nw, w, p.src, modl, norm_post, *tail_args)

    return _per_part(parts, call)


def _rope_tables(n_lat_tok, n_ctx_tok):
    t = np.arange(n_lat_tok)
    n_freq = HEAD_DIM // 4
    inv = 1.0 / (ROPE_THETA ** (jnp.arange(n_freq, dtype=F32) / n_freq))
    rowp = jnp.asarray(t // GRID_W, F32)
    colp = jnp.asarray(t % GRID_W, F32)
    ang = jnp.concatenate([rowp[:, None] * inv, colp[:, None] * inv], axis=-1)
    cos, sin = jnp.cos(ang), jnp.sin(ang)
    cos2 = jnp.concatenate([cos, cos], axis=-1)
    sin2 = jnp.concatenate([-sin, sin], axis=-1)
    cos2 = jnp.concatenate([cos2, jnp.ones((n_ctx_tok, HEAD_DIM), F32)], axis=0)
    sin2 = jnp.concatenate([sin2, jnp.zeros((n_ctx_tok, HEAD_DIM), F32)], axis=0)
    return cos2, sin2


_DEINT = np.concatenate([np.arange(0, HEAD_DIM, 2), np.arange(1, HEAD_DIM, 2)])


def _attn_weight_columns():
    qa, ka, va, ga, qb, kb, vb, gb = 0, 1024, 1280, 1536, 2560, 3584, 3840, 4096

    def heads(start, n, perm):
        base = start + HEAD_DIM * np.arange(n)[:, None]
        return (base + (_DEINT if perm else np.arange(HEAD_DIM))[None, :]).reshape(-1)

    return np.concatenate([
        heads(qa, 8, True), heads(ga, 8, False), heads(qb, 8, True), heads(gb, 8, False),
        heads(ka, 2, True), heads(va, 2, False), heads(kb, 2, True), heads(vb, 2, False)])


_ATTN_COLS_IDX = _attn_weight_columns()


def kernel(x, c, ctx, c_ctx, w_ada, b_ada, norm_pre, norm_post, attn_w_in, attn_w_out, attn_sink,
           attn_q_norm, attn_k_norm, ssm_w_in, ssm_conv_w, ssm_conv_b, ssm_dt_bias, ssm_a_log, ssm_d,
           ssm_norm, ssm_w_out):
    bsz, n_lat_tok, d = x.shape
    n_ctx_tok = ctx.shape[1]
    depth = w_ada.shape[0]
    assert n_lat_tok % TM == 0 and n_lat_tok % n_ctx_tok == 0 and n_ctx_tok % LANES == 0 and bsz <= 7
    s = n_lat_tok + n_ctx_tok
    d_inner = SSM_HEADS * SSM_HEAD_DIM
    bc_w = 2 * SSM_GROUPS * D_STATE

    parts = _stream_parts(x, 0, ctx, 0, n_lat_tok, n_ctx_tok, TM)
    cc = jnp.zeros((8, d), F32).at[:bsz].set(c).at[bsz].set(c_ctx)
    mod = _modulation(cc, w_ada, b_ada)
    mod = mod.reshape(depth, 8, 3, d)
    cos2, sin2 = _rope_tables(n_lat_tok, n_ctx_tok)

    for l in range(depth):
        last = l == depth - 1
        out_parts = _retile(parts, TM_OUT)
        out_parts = out_parts[:1] if last else out_parts
        s_out = n_lat_tok if last else s
        modl = jnp.stack([mod[l, :bsz], jnp.broadcast_to(mod[l, bsz], (bsz, 3, d))], axis=1)
        npre = norm_pre[l].reshape(1, d)
        npost = norm_post[l].reshape(1, d)
        i = l // 2
        if l % 2 == 0:
            w = attn_w_in[i][:, _ATTN_COLS_IDX].astype(BF16)
            qn = attn_q_norm[i][_DEINT].reshape(1, HEAD_DIM)
            kn = attn_k_norm[i][_DEINT].reshape(1, HEAD_DIM)
            qkv = _attn_in(parts, s, modl, npre, w, cos2, sin2, qn, kn)
            sink_b = jnp.broadcast_to(attn_sink[i][:, None], (A_Q_HEADS, HEAD_DIM))
            ya = _attn_a(qkv, sink_b, n_lat_tok)
            yb = _attn_b(qkv, n_lat_tok)
            wo = attn_w_out[i].astype(BF16)
            aq = A_Q_HEADS * HEAD_DIM
            xs = _attn_out(out_parts, s_out, ya, yb, wo[:aq], wo[aq:], modl, npost)
        else:
            w = ssm_w_in[i]
            wz = w[:, :d_inner].astype(BF16)
            wx = w[:, d_inner:2 * d_inner + bc_w].astype(BF16)
            wdt = jnp.pad(w[:, 2 * d_inner + bc_w:], ((0, 0), (0, HPAD - 2 * SSM_HEADS))).astype(BF16)
            dtb = jnp.pad(ssm_dt_bias[i].reshape(1, -1), ((0, 0), (0, HPAD - 2 * SSM_HEADS)))
            z, xbc, dt, dtt = _ssm_in(parts, s, modl, npre, wz, wx, wdt, wdt.T, ssm_conv_w[i],
                                      ssm_conv_b[i].reshape(1, -1), dtb, dtb.reshape(-1, 1))
            alog = jnp.pad(ssm_a_log[i].reshape(1, -1), ((0, 0), (0, HPAD - 2 * SSM_HEADS)))
            yf, ybk = _ssd(xbc, dt, dtt, alog, alog.reshape(-1, 1), n_lat_tok)
            dsk = jnp.repeat(ssm_d[i], SSM_HEAD_DIM).reshape(1, d_inner)
            xs = _ssm_out(out_parts, s_out, yf, ybk, xbc, z, dsk, ssm_norm[i].reshape(1, d_inner),
                          ssm_w_out[i].astype(BF16), modl, npost)
        parts = _stream_parts(xs, 0, xs, n_lat_tok, n_lat_tok, n_ctx_tok, TM)
    return xs
```

```python
import functools
from typing import NamedTuple

import numpy as np
import jax
import jax.numpy as jnp
from jax import lax
from jax.experimental import pallas as pl
from jax.experimental.pallas import tpu as pltpu

F32 = jnp.float32
BF16 = jnp.bfloat16

EPS = 1e-6
GRID_W = 64
ROPE_THETA = 10000.0
HEAD_DIM = 128
A_Q_HEADS = 8
A_KV_HEADS = 2
B_Q_HEADS = 8
B_KV_HEADS = 2
REP = 4
WINDOW = 128
SSM_HEAD_DIM = 64
SSM_HEADS = 32
SSM_GROUPS = 8
SSM_REP = SSM_HEADS // SSM_GROUPS
D_STATE = 128
SSM_CHUNK = 128
SSD_SUB = 2
HPAD = 128

V7X_VMEM_BYTES = 64 * 1024 * 1024
VMEM_LIMIT = V7X_VMEM_BYTES - 8 * 1024 * 1024

TM = 256
TM_OUT = 512
CONV_COLS = 512
TQ_A = 128
NB_A = 4
TQ_B = 512
TK_B = 512
LANES = 128
LOG2E = 1.4426950408889634

QA0, GA0, QB0, GB0, KA0, VA0, KB0, VB0 = 0, 1024, 2048, 3072, 4096, 4352, 4608, 4864
ATTN_COLS = 5120

NEG = -1e30


def _params(sem, vmem=VMEM_LIMIT):
    return pltpu.CompilerParams(dimension_semantics=sem, vmem_limit_bytes=vmem)


def _silu(t):
    return t * (1.0 / (1.0 + jnp.exp2(t * (-LOG2E))))


def _rms(t, w):
    return t * lax.rsqrt(jnp.mean(t * t, axis=-1, keepdims=True) + EPS) * w


def _dot(a, b):
    return jnp.dot(a, b, preferred_element_type=F32)


def _dot_nt(a, b):
    return lax.dot_general(a, b, (((1,), (1,)), ((), ())), preferred_element_type=F32)


def _dot_hi(a, b):
    return jnp.dot(a, b, preferred_element_type=F32, precision=lax.Precision.HIGHEST)


class _Part(NamedTuple):
    idx: int
    rows: int
    tm: int


def _parts(n_lat_tok, n_ctx_tok, tm):
    return (_Part(0, n_lat_tok, tm), _Part(1, n_ctx_tok, n_ctx_tok))


_ROW = lambda bi, i: (bi, i, 0)
_CONST = lambda bi, i: (0, 0)


def _mod_spec(p, d):
    return pl.BlockSpec((None, None, 3, d), lambda bi, i: (bi, p.idx, 0, 0))


def _mod_kernel(cc_ref, w_ref, b_ref, o_ref):
    o_ref[...] = _dot_hi(_silu(cc_ref[...]), w_ref[...]) + b_ref[...]


def _modulation(cc, w_ada, b_ada):
    depth, d, d3 = w_ada.shape
    return pl.pallas_call(
        _mod_kernel,
        grid=(depth,),
        in_specs=[
            pl.BlockSpec((8, d), lambda l: (0, 0)),
            pl.BlockSpec((None, d, d3), lambda l: (l, 0, 0)),
            pl.BlockSpec((None, 1, d3), lambda l: (l, 0, 0)),
        ],
        out_specs=pl.BlockSpec((None, 8, d3), lambda l: (l, 0, 0)),
        out_shape=jax.ShapeDtypeStruct((depth, 8, d3), F32),
        compiler_params=_params(("arbitrary",)),
        name="modulation",
    )(cc, w_ada, b_ada.reshape(depth, 1, d3))


def _pre_norm(x, mod, w):
    return _rms(x, w) * (1.0 + mod[1:2]) + mod[0:1]


def _attn_in_kernel(x_ref, mod_ref, np_ref, w_ref, cos_ref, sin_ref, qn_ref, kn_ref, o_ref):
    hb = _pre_norm(x_ref[...], mod_ref[...], np_ref[...]).astype(BF16)
    cos = cos_ref[...]
    sin = sin_ref[...]
    scale = HEAD_DIM ** -0.5

    def rope(t):
        return t * cos + pltpu.roll(t, HEAD_DIM // 2, 1) * sin

    nblk = 512
    for j in range(ATTN_COLS // nblk):
        c0 = j * nblk
        t = _dot(hb, w_ref[:, c0:c0 + nblk])
        for hh in range(nblk // HEAD_DIM):
            col = c0 + hh * HEAD_DIM
            th = t[:, hh * HEAD_DIM:(hh + 1) * HEAD_DIM]
            if col < GA0:
                th = rope(th) * (scale * LOG2E)
            elif col < QB0 or GB0 <= col < KA0:
                th = _silu(th)
            elif col < GB0:
                th = rope(_rms(th, qn_ref[...])) * (scale * LOG2E)
            elif col < VA0:
                th = rope(th)
            elif KB0 <= col < VB0:
                th = rope(_rms(th, kn_ref[...]))
            o_ref[:, col:col + HEAD_DIM] = th.astype(BF16)


def _attn_in(parts, xs, modl, norm_pre, w, tables, qn, kn):
    b, d = modl.shape[0], modl.shape[-1]

    def call(p):
        table = pl.BlockSpec((p.tm, HEAD_DIM), lambda bi, i: (i, 0))
        return pl.pallas_call(
            _attn_in_kernel,
            grid=(b, p.rows // p.tm),
            in_specs=[
                pl.BlockSpec((None, p.tm, d), _ROW),
                _mod_spec(p, d),
                pl.BlockSpec((1, d), _CONST),
                pl.BlockSpec((d, ATTN_COLS), _CONST),
                table,
                table,
                pl.BlockSpec((1, HEAD_DIM), _CONST),
                pl.BlockSpec((1, HEAD_DIM), _CONST),
            ],
            out_specs=pl.BlockSpec((None, p.tm, ATTN_COLS), _ROW),
            out_shape=jax.ShapeDtypeStruct((b, p.rows, ATTN_COLS), BF16),
            compiler_params=_params(("parallel", "parallel")),
            name="attn_in",
        )(xs[p.idx], modl, norm_pre, w, *tables[p.idx], qn, kn)

    return tuple(call(p) for p in parts)


def _sink_column(sink_ref, g, n):
    return jnp.concatenate(
        [jnp.broadcast_to(sink_ref[g * REP + h:g * REP + h + 1, 0:1] * LOG2E, (n, 1)) for h in range(REP)], axis=0)


def _attn_a_kernel(q_ref, kp_ref, kc_ref, kn_ref, vp_ref, vc_ref, vn_ref, kx_ref, vx_ref,
                   g_ref, sink_ref, o_ref, *, n_lat):
    i = pl.program_id(1)
    tq = TQ_A
    rows = REP * tq
    r = lax.broadcasted_iota(jnp.int32, (rows, tq), 0) & (tq - 1)
    c = lax.broadcasted_iota(jnp.int32, (rows, tq), 1)
    for a in range(NB_A):
        blk = NB_A * i + a
        rq = slice(a * tq, (a + 1) * tq)
        m_prev = (c - r) >= jnp.where(blk > 0, 0, tq)
        m_next = (r - c) >= jnp.where(blk < n_lat - 1, 0, tq)
        for g in range(A_KV_HEADS):
            gs = slice(g * HEAD_DIM, (g + 1) * HEAD_DIM)
            if a == 0:
                k_p, v_p = kp_ref[:, gs], vp_ref[:, gs]
            else:
                k_p, v_p = kc_ref[(a - 1) * tq:a * tq, gs], vc_ref[(a - 1) * tq:a * tq, gs]
            if a == NB_A - 1:
                k_n, v_n = kn_ref[:, gs], vn_ref[:, gs]
            else:
                k_n, v_n = kc_ref[(a + 1) * tq:(a + 2) * tq, gs], vc_ref[(a + 1) * tq:(a + 2) * tq, gs]
            k_c, v_c = kc_ref[rq, gs], vc_ref[rq, gs]
            q = jnp.concatenate(
                [q_ref[rq, (g * REP + h) * HEAD_DIM:(g * REP + h + 1) * HEAD_DIM] for h in range(REP)], axis=0)
            sp = jnp.where(m_prev, _dot_nt(q, k_p), NEG)
            sc = _dot_nt(q, k_c)
            sn = jnp.where(m_next, _dot_nt(q, k_n), NEG)
            sx = _dot_nt(q, kx_ref[:, gs])
            sk = _sink_column(sink_ref, g, tq)
            sx0, sx1 = sx[:, 0:tq], sx[:, tq:2 * tq]
            m_t = jnp.maximum(jnp.maximum(jnp.maximum(sp, sc), jnp.maximum(sn, sx0)), sx1)
            m = jnp.maximum(jnp.max(m_t, axis=1, keepdims=True), sk)
            pp = jnp.exp2(sp - m)
            pc = jnp.exp2(sc - m)
            pn = jnp.exp2(sn - m)
            px = jnp.exp2(sx - m)
            den_t = (pp + pc) + (pn + px[:, 0:tq]) + px[:, tq:2 * tq]
            den = jnp.sum(den_t, axis=1, keepdims=True) + jnp.exp2(sk - m)
            o = (_dot(pp.astype(BF16), v_p) + _dot(pc.astype(BF16), v_c)
                 + _dot(pn.astype(BF16), v_n) + _dot(px.astype(BF16), vx_ref[:, gs])) * (1.0 / den)
            for h in range(REP):
                cs = slice((g * REP + h) * HEAD_DIM, (g * REP + h + 1) * HEAD_DIM)
                o_ref[rq, cs] = (o[h * tq:(h + 1) * tq] * g_ref[rq, cs].astype(F32)).astype(BF16)


def _attn_a_ctx_kernel(q_ref, kx_ref, vx_ref, g_ref, sink_ref, o_ref):
    n = q_ref.shape[0]
    for g in range(A_KV_HEADS):
        gs = slice(g * HEAD_DIM, (g + 1) * HEAD_DIM)
        q = jnp.concatenate(
            [q_ref[:, (g * REP + h) * HEAD_DIM:(g * REP + h + 1) * HEAD_DIM] for h in range(REP)], axis=0)
        sx = _dot_nt(q, kx_ref[:, gs])
        sk = _sink_column(sink_ref, g, n)
        m = jnp.maximum(jnp.max(sx, axis=1, keepdims=True), sk)
        px = jnp.exp2(sx - m)
        den = jnp.sum(px, axis=1, keepdims=True) + jnp.exp2(sk - m)
        o = _dot(px.astype(BF16), vx_ref[:, gs]) * (1.0 / den)
        for h in range(REP):
            cs = slice((g * REP + h) * HEAD_DIM, (g * REP + h + 1) * HEAD_DIM)
            o_ref[:, cs] = (o[h * n:(h + 1) * n] * g_ref[:, cs].astype(F32)).astype(BF16)


def _attn_a(qkv, sink_b):
    q_lat, q_ctx = qkv
    b, n_lat_tok, _ = q_lat.shape
    ctx_len = q_ctx.shape[1]
    tq = TQ_A
    n_lat = n_lat_tok // tq
    kvw = A_KV_HEADS * HEAD_DIM
    step = NB_A * tq
    assert ctx_len == 2 * tq and n_lat_tok % step == 0
    qw = A_Q_HEADS * HEAD_DIM
    prev = lambda cb: (lambda bi, i: (bi, jnp.maximum(NB_A * i - 1, 0), cb))
    cur = lambda cb: (lambda bi, i: (bi, i, cb))
    nxt = lambda cb: (lambda bi, i: (bi, jnp.minimum(NB_A * (i + 1), n_lat - 1), cb))
    whole = lambda cb: (lambda bi, i: (bi, 0, cb))
    kb, vb = KA0 // kvw, VA0 // kvw
    sink_spec = pl.BlockSpec((A_Q_HEADS, HEAD_DIM), _CONST)
    y_lat = pl.pallas_call(
        functools.partial(_attn_a_kernel, n_lat=n_lat),
        grid=(b, n_lat_tok // step),
        in_specs=[
            pl.BlockSpec((None, step, qw), cur(QA0 // qw)),
            pl.BlockSpec((None, tq, kvw), prev(kb)),
            pl.BlockSpec((None, step, kvw), cur(kb)),
            pl.BlockSpec((None, tq, kvw), nxt(kb)),
            pl.BlockSpec((None, tq, kvw), prev(vb)),
            pl.BlockSpec((None, step, kvw), cur(vb)),
            pl.BlockSpec((None, tq, kvw), nxt(vb)),
            pl.BlockSpec((None, ctx_len, kvw), whole(kb)),
            pl.BlockSpec((None, ctx_len, kvw), whole(vb)),
            pl.BlockSpec((None, step, qw), cur(GA0 // qw)),
            sink_spec,
        ],
        out_specs=pl.BlockSpec((None, step, qw), _ROW),
        out_shape=jax.ShapeDtypeStruct((b, n_lat_tok, qw), BF16),
        compiler_params=_params(("parallel", "parallel")),
        name="attn_window",
    )(q_lat, q_lat, q_lat, q_lat, q_lat, q_lat, q_lat, q_ctx, q_ctx, q_lat, sink_b)
    y_ctx = pl.pallas_call(
        _attn_a_ctx_kernel,
        grid=(b, 1),
        in_specs=[
            pl.BlockSpec((None, ctx_len, qw), whole(QA0 // qw)),
            pl.BlockSpec((None, ctx_len, kvw), whole(kb)),
            pl.BlockSpec((None, ctx_len, kvw), whole(vb)),
            pl.BlockSpec((None, ctx_len, qw), whole(GA0 // qw)),
            sink_spec,
        ],
        out_specs=pl.BlockSpec((None, ctx_len, qw), _ROW),
        out_shape=jax.ShapeDtypeStruct((b, ctx_len, qw), BF16),
        compiler_params=_params(("parallel", "parallel")),
        name="attn_window_ctx",
    )(q_ctx, q_ctx, q_ctx, q_ctx, sink_b)
    return y_lat, y_ctx


def _attn_b_kernel(q_ref, k_ref, v_ref, kx_ref, vx_ref, g_ref, o_ref, m_sc, l_sc, acc_sc, p_sc, alpha_sc):
    tq, tk = TQ_B, TK_B
    n_lat_tok = k_ref.shape[0]
    q = jnp.concatenate([q_ref[:, h * HEAD_DIM:(h + 1) * HEAD_DIM] for h in range(REP)], axis=0)
    m_sc[...] = jnp.full(m_sc.shape, NEG, F32)
    l_sc[...] = jnp.zeros(l_sc.shape, F32)
    acc_sc[...] = jnp.zeros(acc_sc.shape, F32)

    rb = 128

    def scores(k, slot):
        nk = k.shape[0]
        s = _dot_nt(q, k)
        for r0 in range(0, REP * tq, rb):
            rs = slice(r0, r0 + rb)
            s_b = s[rs]
            m_old = m_sc[rs]
            m_new = jnp.maximum(m_old, jnp.max(s_b, axis=1, keepdims=True))
            alpha = jnp.exp2(m_old - m_new)
            p = jnp.exp2(s_b - jnp.tile(m_new, (1, nk // LANES)))
            l_sc[rs] = alpha * l_sc[rs] + jnp.sum(p, axis=1, keepdims=True)
            m_sc[rs] = m_new
            alpha_sc[slot, rs] = alpha
            p_sc[slot, rs, 0:nk] = p.astype(BF16)

    def accumulate(slot, v):
        nk = v.shape[0]
        acc_sc[...] = alpha_sc[slot] * acc_sc[...] + _dot(p_sc[slot, :, 0:nk], v)

    def kv(ref, ci):
        if isinstance(ci, int):
            return ref[ci * tk:(ci + 1) * tk, :]
        return ref[pl.ds(pl.multiple_of(ci * tk, tk), tk), :]

    n_main = n_lat_tok // tk
    assert n_main % 2 == 0 and n_lat_tok % tk == 0
    scores(kv(k_ref, 0), 0)

    def pair(c1):
        scores(kv(k_ref, c1), 1)
        accumulate(0, kv(v_ref, c1 - 1))
        scores(kv(k_ref, c1 + 1), 0)
        accumulate(1, kv(v_ref, c1))

    def body(j, carry):
        pair(4 * j + 1)
        pair(4 * j + 3)
        return carry

    n_quads = (n_main - 2) // 4
    lax.fori_loop(0, n_quads, body, 0)
    for c1 in range(4 * n_quads + 1, n_main - 1, 2):
        pair(c1)
    scores(kv(k_ref, n_main - 1), 1)
    accumulate(0, kv(v_ref, n_main - 2))
    scores(kx_ref[...], 0)
    accumulate(1, kv(v_ref, n_main - 1))
    accumulate(0, vx_ref[...])

    o = acc_sc[...] * (1.0 / l_sc[...])
    for h in range(REP):
        cs = slice(h * HEAD_DIM, (h + 1) * HEAD_DIM)
        o_ref[:, cs] = (o[h * tq:(h + 1) * tq] * g_ref[:, cs].astype(F32)).astype(BF16)


def _attn_b_ctx_kernel(q_ref, k_ref, v_ref, g_ref, o_ref):
    n = q_ref.shape[0]
    q = jnp.concatenate([q_ref[:, h * HEAD_DIM:(h + 1) * HEAD_DIM] for h in range(REP)], axis=0)
    s = _dot_nt(q, k_ref[...])
    p = jnp.exp2(s - jnp.max(s, axis=1, keepdims=True))
    o = _dot(p.astype(BF16), v_ref[...]) * (1.0 / jnp.sum(p, axis=1, keepdims=True))
    for h in range(REP):
        cs = slice(h * HEAD_DIM, (h + 1) * HEAD_DIM)
        o_ref[:, cs] = (o[h * n:(h + 1) * n] * g_ref[:, cs].astype(F32)).astype(BF16)


def _attn_b(qkv):
    q_lat, q_ctx = qkv
    b, n_lat_tok, _ = q_lat.shape
    n_ctx_tok = q_ctx.shape[1]
    tq = TQ_B
    gw = REP * HEAD_DIM
    assert n_lat_tok % tq == 0 and n_ctx_tok <= TK_B and n_ctx_tok % LANES == 0
    kcol, vcol = KB0 // HEAD_DIM, VB0 // HEAD_DIM
    y_lat = pl.pallas_call(
        _attn_b_kernel,
        grid=(b, B_KV_HEADS, n_lat_tok // tq),
        in_specs=[
            pl.BlockSpec((None, tq, gw), lambda bi, g, i: (bi, i, QB0 // gw + g)),
            pl.BlockSpec((None, n_lat_tok, HEAD_DIM), lambda bi, g, i: (bi, 0, kcol + g)),
            pl.BlockSpec((None, n_lat_tok, HEAD_DIM), lambda bi, g, i: (bi, 0, vcol + g)),
            pl.BlockSpec((None, n_ctx_tok, HEAD_DIM), lambda bi, g, i: (bi, 0, kcol + g)),
            pl.BlockSpec((None, n_ctx_tok, HEAD_DIM), lambda bi, g, i: (bi, 0, vcol + g)),
            pl.BlockSpec((None, tq, gw), lambda bi, g, i: (bi, i, GB0 // gw + g)),
        ],
        out_specs=pl.BlockSpec((None, tq, gw), lambda bi, g, i: (bi, i, g)),
        out_shape=jax.ShapeDtypeStruct((b, n_lat_tok, B_Q_HEADS * HEAD_DIM), BF16),
        scratch_shapes=[
            pltpu.VMEM((REP * tq, LANES), F32),
            pltpu.VMEM((REP * tq, LANES), F32),
            pltpu.VMEM((REP * tq, HEAD_DIM), F32),
            pltpu.VMEM((2, REP * tq, TK_B), BF16),
            pltpu.VMEM((2, REP * tq, LANES), F32),
        ],
        compiler_params=_params(("parallel", "parallel", "parallel")),
        name="attn_dense",
    )(q_lat, q_lat, q_lat, q_ctx, q_ctx, q_lat)
    y_ctx = pl.pallas_call(
        _attn_b_ctx_kernel,
        grid=(b, B_KV_HEADS),
        in_specs=[
            pl.BlockSpec((None, n_ctx_tok, gw), lambda bi, g: (bi, 0, QB0 // gw + g)),
            pl.BlockSpec((None, n_ctx_tok, HEAD_DIM), lambda bi, g: (bi, 0, kcol + g)),
            pl.BlockSpec((None, n_ctx_tok, HEAD_DIM), lambda bi, g: (bi, 0, vcol + g)),
            pl.BlockSpec((None, n_ctx_tok, gw), lambda bi, g: (bi, 0, GB0 // gw + g)),
        ],
        out_specs=pl.BlockSpec((None, n_ctx_tok, gw), lambda bi, g: (bi, 0, g)),
        out_shape=jax.ShapeDtypeStruct((b, n_ctx_tok, B_Q_HEADS * HEAD_DIM), BF16),
        compiler_params=_params(("parallel", "parallel")),
        name="attn_dense_ctx",
    )(q_ctx, q_ctx, q_ctx, q_ctx)
    return y_lat, y_ctx


def _residual(x, y, mod, w_post):
    return x + mod[2:3] * _rms(y, w_post)


def _attn_out_kernel(ya_ref, yb_ref, wa_ref, wb_ref, x_ref, mod_ref, np_ref, o_ref):
    y = _dot(ya_ref[...], wa_ref[...]) + _dot(yb_ref[...], wb_ref[...])
    o_ref[...] = _residual(x_ref[...], y, mod_ref[...], np_ref[...])


def _attn_out(parts, ya, yb, wa, wb, xs, modl, norm_post):
    b, d = modl.shape[0], modl.shape[-1]

    def call(p):
        return pl.pallas_call(
            _attn_out_kernel,
            grid=(b, p.rows // p.tm),
            in_specs=[
                pl.BlockSpec((None, p.tm, wa.shape[0]), _ROW),
                pl.BlockSpec((None, p.tm, wb.shape[0]), _ROW),
                pl.BlockSpec(wa.shape, _CONST),
                pl.BlockSpec(wb.shape, _CONST),
                pl.BlockSpec((None, p.tm, d), _ROW),
                _mod_spec(p, d),
                pl.BlockSpec((1, d), _CONST),
            ],
            out_specs=pl.BlockSpec((None, p.tm, d), _ROW),
            out_shape=jax.ShapeDtypeStruct((b, p.rows, d), F32),
            compiler_params=_params(("parallel", "parallel")),
            name="attn_out",
        )(ya[p.idx], yb[p.idx], wa, wb, xs[p.idx], modl, norm_post)

    return tuple(call(p) for p in parts)


def _softplus(t):
    return jnp.maximum(t, 0.0) + jnp.log(1.0 + jnp.exp(-jnp.abs(t)))


def _ssm_in_kernel(x_ref, xp_ref, xn_ref, mod_ref, np_ref, wz_ref, wx_ref, wdt_ref, wdtt_ref,
                   cw_ref, cb_ref, dtb_ref, dtbt_ref, z_ref, xbc_ref, dt_ref, dtt_ref):
    i = pl.program_id(1)
    tm = x_ref.shape[0]
    mod = mod_ref[...]
    w_pre = np_ref[...]
    h = _pre_norm(x_ref[...], mod, w_pre)
    hb = h.astype(BF16)
    has_prev = (i > 0).astype(F32)
    has_next = (i < pl.num_programs(1) - 1).astype(F32)
    hp = _pre_norm(xp_ref[...], mod, w_pre) * has_prev
    hn = _pre_norm(xn_ref[...], mod, w_pre) * has_next
    ext = jnp.concatenate([hp, h, hn], axis=0).astype(BF16)

    z_ref[...] = _silu(_dot(hb, wz_ref[...])).astype(BF16)
    dt_ref[...] = _softplus(_dot(hb, wdt_ref[...]) + dtb_ref[...])
    dtt_ref[...] = _softplus(_dot_nt(wdtt_ref[...], hb) + dtbt_ref[...])

    nblk = CONV_COLS
    nt = tm // 8
    sub = lax.broadcasted_iota(jnp.int32, (8, nblk), 0)
    for j in range(wx_ref.shape[1] // nblk):
        cs = slice(j * nblk, (j + 1) * nblk)
        u3 = _dot(ext, wx_ref[:, cs]).reshape(nt + 2, 8, nblk)
        dn = pltpu.roll(u3, 1, 1)
        upw = pltpu.roll(u3, 7, 1)
        u_prev = jnp.where(sub == 0, dn[0:nt], dn[1:nt + 1])
        u_next = jnp.where(sub == 7, upw[2:nt + 2], upw[1:nt + 1])
        conv = (cb_ref[:, cs] + cw_ref[0:1, cs] * u_prev + cw_ref[1:2, cs] * u3[1:nt + 1]
                + cw_ref[2:3, cs] * u_next)
        xbc_ref[:, cs] = _silu(conv).reshape(tm, nblk).astype(BF16)


def _ssm_in(parts, xs, modl, norm_pre, wz, wx, wdt, wdtt, cw, cb, dtb, dtbt):
    b, d = modl.shape[0], modl.shape[-1]
    nh2 = wdt.shape[1]

    def call(p):
        r8 = p.tm // 8
        last8 = p.rows // 8 - 1
        src = xs[p.idx]
        return pl.pallas_call(
            _ssm_in_kernel,
            grid=(b, p.rows // p.tm),
            in_specs=[
                pl.BlockSpec((None, p.tm, d), _ROW),
                pl.BlockSpec((None, 8, d), lambda bi, i: (bi, jnp.maximum(i * r8 - 1, 0), 0)),
                pl.BlockSpec((None, 8, d), lambda bi, i: (bi, jnp.minimum((i + 1) * r8, last8), 0)),
                _mod_spec(p, d),
                pl.BlockSpec((1, d), _CONST),
                pl.BlockSpec(wz.shape, _CONST),
                pl.BlockSpec(wx.shape, _CONST),
                pl.BlockSpec(wdt.shape, _CONST),
                pl.BlockSpec(wdtt.shape, _CONST),
                pl.BlockSpec(cw.shape, _CONST),
                pl.BlockSpec(cb.shape, _CONST),
                pl.BlockSpec(dtb.shape, _CONST),
                pl.BlockSpec(dtbt.shape, _CONST),
            ],
            out_specs=[
                pl.BlockSpec((None, p.tm, wz.shape[1]), _ROW),
                pl.BlockSpec((None, p.tm, wx.shape[1]), _ROW),
                pl.BlockSpec((None, p.tm, nh2), _ROW),
                pl.BlockSpec((None, nh2, p.tm), lambda bi, i: (bi, 0, i)),
            ],
            out_shape=[
                jax.ShapeDtypeStruct((b, p.rows, wz.shape[1]), BF16),
                jax.ShapeDtypeStruct((b, p.rows, wx.shape[1]), BF16),
                jax.ShapeDtypeStruct((b, p.rows, nh2), F32),
                jax.ShapeDtypeStruct((b, nh2, p.rows), F32),
            ],
            compiler_params=_params(("parallel", "parallel")),
            name="ssm_in",
        )(src, src, src, modl, norm_pre, wz, wx, wdt, wdtt, cw, cb, dtb, dtbt)

    lat, cx = (call(p) for p in parts)
    return tuple(zip(lat, cx))


def _split3(t):
    hi = t.astype(BF16)
    r1 = t - hi.astype(F32)
    mid = r1.astype(BF16)
    lo = (r1 - mid.astype(F32)).astype(BF16)
    return hi, mid, lo


def _ones_dot_lhs(tri01, a):
    return _dot(jnp.concatenate([tri01] * 3, axis=1), jnp.concatenate(_split3(a), axis=0))


def _ones_dot_rhs(at, tri01):
    return _dot(jnp.concatenate(_split3(at), axis=1), jnp.concatenate([tri01] * 3, axis=0))


def _ssd_direction(xbc_ref, rs, dt, dtt, a_row, a_col, h_sc, y_ref, reverse, hoff):
    q_len = SSM_CHUNK
    d_inner = SSM_HEADS * SSM_HEAD_DIM
    gw = SSM_REP * SSM_HEAD_DIM
    row = lax.broadcasted_iota(jnp.int32, (q_len, q_len), 0)
    col = lax.broadcasted_iota(jnp.int32, (q_len, q_len), 1)
    lower = row >= col
    upper = row <= col
    mask = upper if reverse else lower
    tri = jnp.where(mask, 1.0, 0.0).astype(BF16)
    tri_t = jnp.where(lower if reverse else upper, 1.0, 0.0).astype(BF16)

    a = dt * a_row
    at = dtt * a_col
    acum = _ones_dot_lhs(tri, a)
    acum_t = _ones_dot_rhs(at, tri_t)
    total_t = jnp.sum(at, axis=1, keepdims=True)
    acum_t = acum_t - jnp.log2(dtt)
    w_t = jnp.exp2(total_t - acum_t)
    etot = jnp.exp2(jnp.sum(a, axis=0, keepdims=True))
    head_of_col = jnp.right_shift(lax.broadcasted_iota(jnp.int32, (q_len, gw), 1),
                                  SSM_HEAD_DIM.bit_length() - 1)

    def group(g):
        b_g = xbc_ref[rs, d_inner + g * D_STATE:d_inner + (g + 1) * D_STATE]
        c_g = xbc_ref[rs, d_inner + (SSM_GROUPS + g) * D_STATE:d_inner + (SSM_GROUPS + g + 1) * D_STATE]
        gsl = slice(g * gw, (g + 1) * gw)
        x_g = xbc_ref[rs, gsl]
        cb = _dot_nt(c_g, b_g)
        c_f = c_g.astype(F32)
        b_t = b_g.astype(F32).T
        h_g = h_sc[g]
        h_b = h_g.astype(BF16)
        ms, ces, bws, bdx, bdh = [], [], [], [], []
        etot_row = None
        for r in range(SSM_REP):
            h = hoff + g * SSM_REP + r
            acb = jnp.broadcast_to(acum[:, h:h + 1], (q_len, q_len))
            decay = jnp.exp2(jnp.where(mask, acb - acum_t[h:h + 1, :], NEG))
            ms.append((cb * decay).astype(BF16))
            ces.append((c_f * jnp.exp2(acb)).astype(BF16))
            bws.append((b_t * w_t[h:h + 1, :]).astype(BF16))
            sel = head_of_col == r
            bdx.append(jnp.where(sel, x_g, jnp.zeros_like(x_g)))
            bdh.append(jnp.where(sel, h_b, jnp.zeros_like(h_b)))
            e_r = jnp.broadcast_to(etot[:, h:h + 1], (1, gw))
            etot_row = e_r if etot_row is None else jnp.where(head_of_col[0:1] == r, e_r, etot_row)
        bdx = jnp.concatenate(bdx, axis=0)
        bdh = jnp.concatenate(bdh, axis=0)
        lhs = jnp.concatenate([jnp.concatenate(ms, axis=1), jnp.concatenate(bws, axis=1)], axis=0)
        res = _dot(lhs, bdx)
        y = res[0:q_len] + _dot(jnp.concatenate(ces, axis=1), bdh)
        y_ref[rs, gsl] = y.astype(BF16)
        h_sc[g] = h_g * etot_row + res[q_len:]

    return group


def _ssd_kernel(*refs, has_init, emit_state):
    xf_ref, xb_ref, dtf_ref, dtb_ref, dttf_ref, dttb_ref, alog_ref, alogt_ref = refs[:8]
    refs = refs[8:]
    if has_init:
        hf0_ref, hb0_ref = refs[:2]
        refs = refs[2:]
    yf_ref, yb_ref = refs[:2]
    refs = refs[2:]
    if emit_state:
        hf_out_ref, hb_out_ref = refs[:2]
        refs = refs[2:]
    hf_sc, hb_sc = refs

    @pl.when(pl.program_id(1) == 0)
    def _():
        if has_init:
            hf_sc[...] = hf0_ref[...]
            hb_sc[...] = hb0_ref[...]
        else:
            hf_sc[...] = jnp.zeros(hf_sc.shape, F32)
            hb_sc[...] = jnp.zeros(hb_sc.shape, F32)

    a_row = -jnp.exp(alog_ref[...]) * LOG2E
    a_col = -jnp.exp(alogt_ref[...]) * LOG2E
    for sub in range(SSD_SUB):
        rs = slice(sub * SSM_CHUNK, (sub + 1) * SSM_CHUNK)
        fwd = _ssd_direction(xf_ref, rs, dtf_ref[rs, :], dttf_ref[:, rs], a_row, a_col, hf_sc, yf_ref, False, 0)
        for g in range(SSM_GROUPS):
            fwd(g)
    for sub in reversed(range(SSD_SUB)):
        rs = slice(sub * SSM_CHUNK, (sub + 1) * SSM_CHUNK)
        bwd = _ssd_direction(xb_ref, rs, dtb_ref[rs, :], dttb_ref[:, rs], a_row, a_col, hb_sc, yb_ref, True,
                             SSM_HEADS)
        for g in range(SSM_GROUPS):
            bwd(g)

    if emit_state:
        hf_out_ref[...] = hf_sc[...]
        hb_out_ref[...] = hb_sc[...]


def _ssd_stream(xbc, dt, dtt, alog, alogt, init, emit_state):
    b, rows, cw = xbc.shape
    q_len = SSD_SUB * SSM_CHUNK
    assert rows % q_len == 0
    n_c = rows // q_len
    d_inner = SSM_HEADS * SSM_HEAD_DIM
    nh2 = dt.shape[-1]
    state_shape = (SSM_GROUPS, D_STATE, SSM_REP * SSM_HEAD_DIM)
    asc = lambda bi, j: (bi, j, 0)
    desc = lambda bi, j: (bi, n_c - 1 - j, 0)
    state_spec = pl.BlockSpec((None,) + state_shape, lambda bi, j: (bi, 0, 0, 0))
    in_specs = [
        pl.BlockSpec((None, q_len, cw), asc),
        pl.BlockSpec((None, q_len, cw), desc),
        pl.BlockSpec((None, q_len, nh2), asc),
        pl.BlockSpec((None, q_len, nh2), desc),
        pl.BlockSpec((None, nh2, q_len), lambda bi, j: (bi, 0, j)),
        pl.BlockSpec((None, nh2, q_len), lambda bi, j: (bi, 0, n_c - 1 - j)),
        pl.BlockSpec(alog.shape, _CONST),
        pl.BlockSpec(alogt.shape, _CONST),
    ]
    args = [xbc, xbc, dt, dt, dtt, dtt, alog, alogt]
    if init is not None:
        in_specs += [state_spec, state_spec]
        args += list(init)
    out_specs = [pl.BlockSpec((None, q_len, d_inner), asc), pl.BlockSpec((None, q_len, d_inner), desc)]
    out_shape = [jax.ShapeDtypeStruct((b, rows, d_inner), BF16)] * 2
    if emit_state:
        out_specs += [state_spec, state_spec]
        out_shape += [jax.ShapeDtypeStruct((b,) + state_shape, F32)] * 2
    return pl.pallas_call(
        functools.partial(_ssd_kernel, has_init=init is not None, emit_state=emit_state),
        grid=(b, n_c),
        in_specs=in_specs,
        out_specs=out_specs,
        out_shape=out_shape,
        scratch_shapes=[pltpu.VMEM(state_shape, F32)] * 2,
        compiler_params=_params(("parallel", "arbitrary")),
        name="ssd_scan",
    )(*args)


def _ssd(xbc, dt, dtt, alog, alogt):
    yf_c, yb_c, hf, hb = _ssd_stream(xbc[1], dt[1], dtt[1], alog, alogt, None, True)
    yf_l, yb_l = _ssd_stream(xbc[0], dt[0], dtt[0], alog, alogt, (hf, hb), False)
    return (yf_l, yf_c), (yb_l, yb_c)


def _ssm_out_kernel(yf_ref, yb_ref, xs_ref, z_ref, dsk_ref, nw_ref, w_ref, x_ref, mod_ref, np_ref, o_ref):
    y = yf_ref[...].astype(F32) + yb_ref[...].astype(F32) + dsk_ref[...] * xs_ref[...].astype(F32)
    gated = y * z_ref[...].astype(F32)
    gsz = gated.shape[1] // SSM_GROUPS
    parts = []
    for g in range(SSM_GROUPS):
        t = gated[:, g * gsz:(g + 1) * gsz]
        parts.append(t * lax.rsqrt(jnp.mean(t * t, axis=-1, keepdims=True) + EPS))
    gn = (jnp.concatenate(parts, axis=1) * nw_ref[...]).astype(BF16)
    o_ref[...] = _residual(x_ref[...], _dot(gn, w_ref[...]), mod_ref[...], np_ref[...])


def _ssm_out(parts, yf, yb, xbc, z, dsk, nw, w, xs, modl, norm_post):
    b, d = modl.shape[0], modl.shape[-1]
    di = w.shape[0]

    def call(p):
        wide = pl.BlockSpec((None, p.tm, di), _ROW)
        return pl.pallas_call(
            _ssm_out_kernel,
            grid=(b, p.rows // p.tm),
            in_specs=[
                wide, wide, wide, wide,
                pl.BlockSpec((1, di), _CONST),
                pl.BlockSpec((1, di), _CONST),
                pl.BlockSpec(w.shape, _CONST),
                pl.BlockSpec((None, p.tm, d), _ROW),
                _mod_spec(p, d),
                pl.BlockSpec((1, d), _CONST),
            ],
            out_specs=pl.BlockSpec((None, p.tm, d), _ROW),
            out_shape=jax.ShapeDtypeStruct((b, p.rows, d), F32),
            compiler_params=_params(("parallel", "parallel")),
            name="ssm_out",
        )(yf[p.idx], yb[p.idx], xbc[p.idx], z[p.idx], dsk, nw, w, xs[p.idx], modl, norm_post)

    return tuple(call(p) for p in parts)


def _rope_tables(n_lat_tok, n_ctx_tok):
    t = np.arange(n_lat_tok)
    n_freq = HEAD_DIM // 4
    inv = 1.0 / (ROPE_THETA ** (jnp.arange(n_freq, dtype=F32) / n_freq))
    rowp = jnp.asarray(t // GRID_W, F32)
    colp = jnp.asarray(t % GRID_W, F32)
    ang = jnp.concatenate([rowp[:, None] * inv, colp[:, None] * inv], axis=-1)
    cos, sin = jnp.cos(ang), jnp.sin(ang)
    lat = (jnp.concatenate([cos, cos], axis=-1), jnp.concatenate([-sin, sin], axis=-1))
    cx = (jnp.ones((n_ctx_tok, HEAD_DIM), F32), jnp.zeros((n_ctx_tok, HEAD_DIM), F32))
    return lat, cx


_DEINT = np.concatenate([np.arange(0, HEAD_DIM, 2), np.arange(1, HEAD_DIM, 2)])


def _attn_weight_columns():
    qa, ka, va, ga, qb, kb, vb, gb = 0, 1024, 1280, 1536, 2560, 3584, 3840, 4096

    def heads(start, n, perm):
        base = start + HEAD_DIM * np.arange(n)[:, None]
        return (base + (_DEINT if perm else np.arange(HEAD_DIM))[None, :]).reshape(-1)

    return np.concatenate([
        heads(qa, 8, True), heads(ga, 8, False), heads(qb, 8, True), heads(gb, 8, False),
        heads(ka, 2, True), heads(va, 2, False), heads(kb, 2, True), heads(vb, 2, False)])


_ATTN_COLS_IDX = _attn_weight_columns()


def kernel(x, c, ctx, c_ctx, w_ada, b_ada, norm_pre, norm_post, attn_w_in, attn_w_out, attn_sink,
           attn_q_norm, attn_k_norm, ssm_w_in, ssm_conv_w, ssm_conv_b, ssm_dt_bias, ssm_a_log, ssm_d,
           ssm_norm, ssm_w_out):
    bsz, n_lat_tok, d = x.shape
    n_ctx_tok = ctx.shape[1]
    depth = w_ada.shape[0]
    assert n_lat_tok % TM_OUT == 0 and n_lat_tok % TM == 0 and n_ctx_tok % LANES == 0 and bsz <= 7
    d_inner = SSM_HEADS * SSM_HEAD_DIM
    bc_w = 2 * SSM_GROUPS * D_STATE

    xs = (x, ctx)
    in_parts = _parts(n_lat_tok, n_ctx_tok, TM)
    out_parts = _parts(n_lat_tok, n_ctx_tok, TM_OUT)
    cc = jnp.zeros((8, d), F32).at[:bsz].set(c).at[bsz].set(c_ctx)
    mod = _modulation(cc, w_ada, b_ada)
    mod = mod.reshape(depth, 8, 3, d)
    tables = _rope_tables(n_lat_tok, n_ctx_tok)

    for l in range(depth):
        l_out_parts = out_parts[:1] if l == depth - 1 else out_parts
        modl = jnp.stack([mod[l, :bsz], jnp.broadcast_to(mod[l, bsz], (bsz, 3, d))], axis=1)
        npre = norm_pre[l].reshape(1, d)
        npost = norm_post[l].reshape(1, d)
        i = l // 2
        if l % 2 == 0:
            w = attn_w_in[i][:, _ATTN_COLS_IDX].astype(BF16)
            qn = attn_q_norm[i][_DEINT].reshape(1, HEAD_DIM)
            kn = attn_k_norm[i][_DEINT].reshape(1, HEAD_DIM)
            qkv = _attn_in(in_parts, xs, modl, npre, w, tables, qn, kn)
            sink_b = jnp.broadcast_to(attn_sink[i][:, None], (A_Q_HEADS, HEAD_DIM))
            ya = _attn_a(qkv, sink_b)
            yb = _attn_b(qkv)
            wo = attn_w_out[i].astype(BF16)
            aq = A_Q_HEADS * HEAD_DIM
            xs = _attn_out(l_out_parts, ya, yb, wo[:aq], wo[aq:], xs, modl, npost)
        else:
            w = ssm_w_in[i]
            wz = w[:, :d_inner].astype(BF16)
            wx = w[:, d_inner:2 * d_inner + bc_w].astype(BF16)
            wdt = jnp.pad(w[:, 2 * d_inner + bc_w:], ((0, 0), (0, HPAD - 2 * SSM_HEADS))).astype(BF16)
            dtb = jnp.pad(ssm_dt_bias[i].reshape(1, -1), ((0, 0), (0, HPAD - 2 * SSM_HEADS)))
            z, xbc, dt, dtt = _ssm_in(in_parts, xs, modl, npre, wz, wx, wdt, wdt.T, ssm_conv_w[i],
                                      ssm_conv_b[i].reshape(1, -1), dtb, dtb.reshape(-1, 1))
            alog = jnp.pad(ssm_a_log[i].reshape(1, -1), ((0, 0), (0, HPAD - 2 * SSM_HEADS)))
            yf, ybk = _ssd(xbc, dt, dtt, alog, alog.reshape(-1, 1))
            dsk = jnp.repeat(ssm_d[i], SSM_HEAD_DIM).reshape(1, d_inner)
            xs = _ssm_out(l_out_parts, yf, ybk, xbc, z, dsk, ssm_norm[i].reshape(1, d_inner),
                          ssm_w_out[i].astype(BF16), xs, modl, npost)
    return xs[0]
```

```python
import functools
from typing import NamedTuple

import numpy as np
import jax
import jax.numpy as jnp
from jax import lax
from jax.experimental import pallas as pl
from jax.experimental.pallas import tpu as pltpu

F32 = jnp.float32
BF16 = jnp.bfloat16

EPS = 1e-6
GRID_W = 64
ROPE_THETA = 10000.0
HEAD_DIM = 128
A_Q_HEADS = 8
A_KV_HEADS = 2
B_Q_HEADS = 8
B_KV_HEADS = 2
REP = 4
WINDOW = 128
SSM_HEAD_DIM = 64
SSM_HEADS = 32
SSM_GROUPS = 8
SSM_REP = SSM_HEADS // SSM_GROUPS
D_STATE = 128
SSM_CHUNK = 128
SSD_SUB = 2
HPAD = 128

V7X_VMEM_BYTES = 64 * 1024 * 1024
VMEM_LIMIT = V7X_VMEM_BYTES - 8 * 1024 * 1024

TM = 256
TM_OUT = 512
CONV_COLS = 512
TQ_A = 128
NB_A = 8
TQ_B = 512
TK_B = 512
LANES = 128
LOG2E = 1.4426950408889634

QA0, GA0, QB0, GB0, KA0, VA0, KB0, VB0 = 0, 1024, 2048, 3072, 4096, 4352, 4608, 4864
ATTN_COLS = 5120

NEG = -1e30


def _params(sem, vmem=VMEM_LIMIT):
    return pltpu.CompilerParams(dimension_semantics=sem, vmem_limit_bytes=vmem)


def _silu(t):
    return t * (1.0 / (1.0 + jnp.exp2(t * (-LOG2E))))


def _rms(t, w):
    return t * lax.rsqrt(jnp.mean(t * t, axis=-1, keepdims=True) + EPS) * w


def _dot(a, b):
    return jnp.dot(a, b, preferred_element_type=F32)


def _dot_nt(a, b):
    return lax.dot_general(a, b, (((1,), (1,)), ((), ())), preferred_element_type=F32)


def _dot_hi(a, b):
    return jnp.dot(a, b, preferred_element_type=F32, precision=lax.Precision.HIGHEST)


class _Part(NamedTuple):
    idx: int
    rows: int
    tm: int


def _parts(n_lat_tok, n_ctx_tok, tm):
    return (_Part(0, n_lat_tok, tm), _Part(1, n_ctx_tok, n_ctx_tok))


_ROW = lambda bi, i: (bi, i, 0)
_CONST = lambda bi, i: (0, 0)


def _mod_spec(p, d):
    return pl.BlockSpec((None, None, 3, d), lambda bi, i: (bi, p.idx, 0, 0))


def _mod_kernel(cc_ref, w_ref, b_ref, o_ref):
    o_ref[...] = _dot_hi(_silu(cc_ref[...]), w_ref[...]) + b_ref[...]


def _modulation(cc, w_ada, b_ada):
    depth, d, d3 = w_ada.shape
    return pl.pallas_call(
        _mod_kernel,
        grid=(depth,),
        in_specs=[
            pl.BlockSpec((8, d), lambda l: (0, 0)),
            pl.BlockSpec((None, d, d3), lambda l: (l, 0, 0)),
            pl.BlockSpec((None, 1, d3), lambda l: (l, 0, 0)),
        ],
        out_specs=pl.BlockSpec((None, 8, d3), lambda l: (l, 0, 0)),
        out_shape=jax.ShapeDtypeStruct((depth, 8, d3), F32),
        compiler_params=_params(("arbitrary",)),
        name="modulation",
    )(cc, w_ada, b_ada.reshape(depth, 1, d3))


def _pre_norm(x, mod, w):
    return _rms(x, w) * (1.0 + mod[1:2]) + mod[0:1]


def _attn_in_kernel(x_ref, mod_ref, np_ref, w_ref, cos_ref, sin_ref, qn_ref, kn_ref, o_ref):
    hb = _pre_norm(x_ref[...], mod_ref[...], np_ref[...]).astype(BF16)
    cos = cos_ref[...]
    sin = sin_ref[...]
    scale = HEAD_DIM ** -0.5

    def rope(t):
        return t * cos + pltpu.roll(t, HEAD_DIM // 2, 1) * sin

    nblk = 512
    for j in range(ATTN_COLS // nblk):
        c0 = j * nblk
        t = _dot(hb, w_ref[:, c0:c0 + nblk])
        for hh in range(nblk // HEAD_DIM):
            col = c0 + hh * HEAD_DIM
            th = t[:, hh * HEAD_DIM:(hh + 1) * HEAD_DIM]
            if col < GA0:
                th = rope(th) * (scale * LOG2E)
            elif col < QB0 or GB0 <= col < KA0:
                th = _silu(th)
            elif col < GB0:
                th = rope(_rms(th, qn_ref[...])) * (scale * LOG2E)
            elif col < VA0:
                th = rope(th)
            elif KB0 <= col < VB0:
                th = rope(_rms(th, kn_ref[...]))
            o_ref[:, col:col + HEAD_DIM] = th.astype(BF16)


def _attn_in(parts, xs, modl, norm_pre, w, tables, qn, kn):
    b, d = modl.shape[0], modl.shape[-1]

    def call(p):
        table = pl.BlockSpec((p.tm, HEAD_DIM), lambda bi, i: (i, 0))
        return pl.pallas_call(
            _attn_in_kernel,
            grid=(b, p.rows // p.tm),
            in_specs=[
                pl.BlockSpec((None, p.tm, d), _ROW),
                _mod_spec(p, d),
                pl.BlockSpec((1, d), _CONST),
                pl.BlockSpec((d, ATTN_COLS), _CONST),
                table,
                table,
                pl.BlockSpec((1, HEAD_DIM), _CONST),
                pl.BlockSpec((1, HEAD_DIM), _CONST),
            ],
            out_specs=pl.BlockSpec((None, p.tm, ATTN_COLS), _ROW),
            out_shape=jax.ShapeDtypeStruct((b, p.rows, ATTN_COLS), BF16),
            compiler_params=_params(("parallel", "parallel")),
            name="attn_in",
        )(xs[p.idx], modl, norm_pre, w, *tables[p.idx], qn, kn)

    return tuple(call(p) for p in parts)


def _sink_column(sink_ref, g, n):
    return jnp.concatenate(
        [jnp.broadcast_to(sink_ref[g * REP + h:g * REP + h + 1, 0:1] * LOG2E, (n, 1)) for h in range(REP)], axis=0)


def _attn_a_kernel(q_ref, kp_ref, kc_ref, kn_ref, vp_ref, vc_ref, vn_ref, kx_ref, vx_ref,
                   g_ref, sink_ref, o_ref, *, n_lat):
    i = pl.program_id(1)
    tq = TQ_A
    rows = REP * tq
    r = lax.broadcasted_iota(jnp.int32, (rows, tq), 0) & (tq - 1)
    c = lax.broadcasted_iota(jnp.int32, (rows, tq), 1)
    for a in range(NB_A):
        blk = NB_A * i + a
        rq = slice(a * tq, (a + 1) * tq)
        m_prev = (c - r) >= jnp.where(blk > 0, 0, tq)
        m_next = (r - c) >= jnp.where(blk < n_lat - 1, 0, tq)
        for g in range(A_KV_HEADS):
            gs = slice(g * HEAD_DIM, (g + 1) * HEAD_DIM)
            if a == 0:
                k_p, v_p = kp_ref[:, gs], vp_ref[:, gs]
            else:
                k_p, v_p = kc_ref[(a - 1) * tq:a * tq, gs], vc_ref[(a - 1) * tq:a * tq, gs]
            if a == NB_A - 1:
                k_n, v_n = kn_ref[:, gs], vn_ref[:, gs]
            else:
                k_n, v_n = kc_ref[(a + 1) * tq:(a + 2) * tq, gs], vc_ref[(a + 1) * tq:(a + 2) * tq, gs]
            k_c, v_c = kc_ref[rq, gs], vc_ref[rq, gs]
            q = jnp.concatenate(
                [q_ref[rq, (g * REP + h) * HEAD_DIM:(g * REP + h + 1) * HEAD_DIM] for h in range(REP)], axis=0)
            sp = jnp.where(m_prev, _dot_nt(q, k_p), NEG)
            sc = _dot_nt(q, k_c)
            sn = jnp.where(m_next, _dot_nt(q, k_n), NEG)
            sx = _dot_nt(q, kx_ref[:, gs])
            sk = _sink_column(sink_ref, g, tq)
            sx0, sx1 = sx[:, 0:tq], sx[:, tq:2 * tq]
            m_t = jnp.maximum(jnp.maximum(jnp.maximum(sp, sc), jnp.maximum(sn, sx0)), sx1)
            m = jnp.maximum(jnp.max(m_t, axis=1, keepdims=True), sk)
            pp = jnp.exp2(sp - m)
            pc = jnp.exp2(sc - m)
            pn = jnp.exp2(sn - m)
            px = jnp.exp2(sx - m)
            den_t = (pp + pc) + (pn + px[:, 0:tq]) + px[:, tq:2 * tq]
            den = jnp.sum(den_t, axis=1, keepdims=True) + jnp.exp2(sk - m)
            o = (_dot(pp.astype(BF16), v_p) + _dot(pc.astype(BF16), v_c)
                 + _dot(pn.astype(BF16), v_n) + _dot(px.astype(BF16), vx_ref[:, gs])) * (1.0 / den)
            for h in range(REP):
                cs = slice((g * REP + h) * HEAD_DIM, (g * REP + h + 1) * HEAD_DIM)
                o_ref[rq, cs] = (o[h * tq:(h + 1) * tq] * g_ref[rq, cs].astype(F32)).astype(BF16)


def _attn_a_ctx_kernel(q_ref, kx_ref, vx_ref, g_ref, sink_ref, o_ref):
    n = q_ref.shape[0]
    for g in range(A_KV_HEADS):
        gs = slice(g * HEAD_DIM, (g + 1) * HEAD_DIM)
        q = jnp.concatenate(
            [q_ref[:, (g * REP + h) * HEAD_DIM:(g * REP + h + 1) * HEAD_DIM] for h in range(REP)], axis=0)
        sx = _dot_nt(q, kx_ref[:, gs])
        sk = _sink_column(sink_ref, g, n)
        m = jnp.maximum(jnp.max(sx, axis=1, keepdims=True), sk)
        px = jnp.exp2(sx - m)
        den = jnp.sum(px, axis=1, keepdims=True) + jnp.exp2(sk - m)
        o = _dot(px.astype(BF16), vx_ref[:, gs]) * (1.0 / den)
        for h in range(REP):
            cs = slice((g * REP + h) * HEAD_DIM, (g * REP + h + 1) * HEAD_DIM)
            o_ref[:, cs] = (o[h * n:(h + 1) * n] * g_ref[:, cs].astype(F32)).astype(BF16)


def _attn_a(qkv, sink_b):
    q_lat, q_ctx = qkv
    b, n_lat_tok, _ = q_lat.shape
    ctx_len = q_ctx.shape[1]
    tq = TQ_A
    n_lat = n_lat_tok // tq
    kvw = A_KV_HEADS * HEAD_DIM
    step = NB_A * tq
    assert ctx_len == 2 * tq and n_lat_tok % step == 0
    qw = A_Q_HEADS * HEAD_DIM
    prev = lambda cb: (lambda bi, i: (bi, jnp.maximum(NB_A * i - 1, 0), cb))
    cur = lambda cb: (lambda bi, i: (bi, i, cb))
    nxt = lambda cb: (lambda bi, i: (bi, jnp.minimum(NB_A * (i + 1), n_lat - 1), cb))
    whole = lambda cb: (lambda bi, i: (bi, 0, cb))
    kb, vb = KA0 // kvw, VA0 // kvw
    sink_spec = pl.BlockSpec((A_Q_HEADS, HEAD_DIM), _CONST)
    y_lat = pl.pallas_call(
        functools.partial(_attn_a_kernel, n_lat=n_lat),
        grid=(b, n_lat_tok // step),
        in_specs=[
            pl.BlockSpec((None, step, qw), cur(QA0 // qw)),
            pl.BlockSpec((None, tq, kvw), prev(kb)),
            pl.BlockSpec((None, step, kvw), cur(kb)),
            pl.BlockSpec((None, tq, kvw), nxt(kb)),
            pl.BlockSpec((None, tq, kvw), prev(vb)),
            pl.BlockSpec((None, step, kvw), cur(vb)),
            pl.BlockSpec((None, tq, kvw), nxt(vb)),
            pl.BlockSpec((None, ctx_len, kvw), whole(kb)),
            pl.BlockSpec((None, ctx_len, kvw), whole(vb)),
            pl.BlockSpec((None, step, qw), cur(GA0 // qw)),
            sink_spec,
        ],
        out_specs=pl.BlockSpec((None, step, qw), _ROW),
        out_shape=jax.ShapeDtypeStruct((b, n_lat_tok, qw), BF16),
        compiler_params=_params(("parallel", "parallel")),
        name="attn_window",
    )(q_lat, q_lat, q_lat, q_lat, q_lat, q_lat, q_lat, q_ctx, q_ctx, q_lat, sink_b)
    y_ctx = pl.pallas_call(
        _attn_a_ctx_kernel,
        grid=(b, 1),
        in_specs=[
            pl.BlockSpec((None, ctx_len, qw), whole(QA0 // qw)),
            pl.BlockSpec((None, ctx_len, kvw), whole(kb)),
            pl.BlockSpec((None, ctx_len, kvw), whole(vb)),
            pl.BlockSpec((None, ctx_len, qw), whole(GA0 // qw)),
            sink_spec,
        ],
        out_specs=pl.BlockSpec((None, ctx_len, qw), _ROW),
        out_shape=jax.ShapeDtypeStruct((b, ctx_len, qw), BF16),
        compiler_params=_params(("parallel", "parallel")),
        name="attn_window_ctx",
    )(q_ctx, q_ctx, q_ctx, q_ctx, sink_b)
    return y_lat, y_ctx


def _attn_b_kernel(q_ref, k_ref, v_ref, kx_ref, vx_ref, g_ref, o_ref, m_sc, l_sc, acc_sc, p_sc, alpha_sc):
    tq, tk = TQ_B, TK_B
    n_lat_tok = k_ref.shape[0]
    q = jnp.concatenate([q_ref[:, h * HEAD_DIM:(h + 1) * HEAD_DIM] for h in range(REP)], axis=0)
    m_sc[...] = jnp.full(m_sc.shape, NEG, F32)
    l_sc[...] = jnp.zeros(l_sc.shape, F32)
    acc_sc[...] = jnp.zeros(acc_sc.shape, F32)

    rb = 128

    def scores(k, slot):
        nk = k.shape[0]
        s = _dot_nt(q, k)
        for r0 in range(0, REP * tq, rb):
            rs = slice(r0, r0 + rb)
            s_b = s[rs]
            m_old = m_sc[rs]
            m_new = jnp.maximum(m_old, jnp.max(s_b, axis=1, keepdims=True))
            alpha = jnp.exp2(m_old - m_new)
            p = jnp.exp2(s_b - jnp.tile(m_new, (1, nk // LANES)))
            l_sc[rs] = alpha * l_sc[rs] + jnp.sum(p, axis=1, keepdims=True)
            m_sc[rs] = m_new
            alpha_sc[slot, rs] = alpha
            p_sc[slot, rs, 0:nk] = p.astype(BF16)

    def accumulate(slot, v):
        nk = v.shape[0]
        acc_sc[...] = alpha_sc[slot] * acc_sc[...] + _dot(p_sc[slot, :, 0:nk], v)

    def kv(ref, ci):
        if isinstance(ci, int):
            return ref[ci * tk:(ci + 1) * tk, :]
        return ref[pl.ds(pl.multiple_of(ci * tk, tk), tk), :]

    n_main = n_lat_tok // tk
    assert n_main % 2 == 0 and n_lat_tok % tk == 0
    scores(kv(k_ref, 0), 0)

    def pair(c1):
        scores(kv(k_ref, c1), 1)
        accumulate(0, kv(v_ref, c1 - 1))
        scores(kv(k_ref, c1 + 1), 0)
        accumulate(1, kv(v_ref, c1))

    def body(j, carry):
        pair(4 * j + 1)
        pair(4 * j + 3)
        return carry

    n_quads = (n_main - 2) // 4
    lax.fori_loop(0, n_quads, body, 0)
    for c1 in range(4 * n_quads + 1, n_main - 1, 2):
        pair(c1)
    scores(kv(k_ref, n_main - 1), 1)
    accumulate(0, kv(v_ref, n_main - 2))
    scores(kx_ref[...], 0)
    accumulate(1, kv(v_ref, n_main - 1))
    accumulate(0, vx_ref[...])

    o = acc_sc[...] * (1.0 / l_sc[...])
    for h in range(REP):
        cs = slice(h * HEAD_DIM, (h + 1) * HEAD_DIM)
        o_ref[:, cs] = (o[h * tq:(h + 1) * tq] * g_ref[:, cs].astype(F32)).astype(BF16)


def _attn_b_ctx_kernel(q_ref, k_ref, v_ref, g_ref, o_ref):
    n = q_ref.shape[0]
    q = jnp.concatenate([q_ref[:, h * HEAD_DIM:(h + 1) * HEAD_DIM] for h in range(REP)], axis=0)
    s = _dot_nt(q, k_ref[...])
    p = jnp.exp2(s - jnp.max(s, axis=1, keepdims=True))
    o = _dot(p.astype(BF16), v_ref[...]) * (1.0 / jnp.sum(p, axis=1, keepdims=True))
    for h in range(REP):
        cs = slice(h * HEAD_DIM, (h + 1) * HEAD_DIM)
        o_ref[:, cs] = (o[h * n:(h + 1) * n] * g_ref[:, cs].astype(F32)).astype(BF16)


def _attn_b(qkv):
    q_lat, q_ctx = qkv
    b, n_lat_tok, _ = q_lat.shape
    n_ctx_tok = q_ctx.shape[1]
    tq = TQ_B
    gw = REP * HEAD_DIM
    assert n_lat_tok % tq == 0 and n_ctx_tok <= TK_B and n_ctx_tok % LANES == 0
    kcol, vcol = KB0 // HEAD_DIM, VB0 // HEAD_DIM
    y_lat = pl.pallas_call(
        _attn_b_kernel,
        grid=(b, B_KV_HEADS, n_lat_tok // tq),
        in_specs=[
            pl.BlockSpec((None, tq, gw), lambda bi, g, i: (bi, i, QB0 // gw + g)),
            pl.BlockSpec((None, n_lat_tok, HEAD_DIM), lambda bi, g, i: (bi, 0, kcol + g)),
            pl.BlockSpec((None, n_lat_tok, HEAD_DIM), lambda bi, g, i: (bi, 0, vcol + g)),
            pl.BlockSpec((None, n_ctx_tok, HEAD_DIM), lambda bi, g, i: (bi, 0, kcol + g)),
            pl.BlockSpec((None, n_ctx_tok, HEAD_DIM), lambda bi, g, i: (bi, 0, vcol + g)),
            pl.BlockSpec((None, tq, gw), lambda bi, g, i: (bi, i, GB0 // gw + g)),
        ],
        out_specs=pl.BlockSpec((None, tq, gw), lambda bi, g, i: (bi, i, g)),
        out_shape=jax.ShapeDtypeStruct((b, n_lat_tok, B_Q_HEADS * HEAD_DIM), BF16),
        scratch_shapes=[
            pltpu.VMEM((REP * tq, LANES), F32),
            pltpu.VMEM((REP * tq, LANES), F32),
            pltpu.VMEM((REP * tq, HEAD_DIM), F32),
            pltpu.VMEM((2, REP * tq, TK_B), BF16),
            pltpu.VMEM((2, REP * tq, LANES), F32),
        ],
        compiler_params=_params(("parallel", "parallel", "parallel")),
        name="attn_dense",
    )(q_lat, q_lat, q_lat, q_ctx, q_ctx, q_lat)
    y_ctx = pl.pallas_call(
        _attn_b_ctx_kernel,
        grid=(b, B_KV_HEADS),
        in_specs=[
            pl.BlockSpec((None, n_ctx_tok, gw), lambda bi, g: (bi, 0, QB0 // gw + g)),
            pl.BlockSpec((None, n_ctx_tok, HEAD_DIM), lambda bi, g: (bi, 0, kcol + g)),
            pl.BlockSpec((None, n_ctx_tok, HEAD_DIM), lambda bi, g: (bi, 0, vcol + g)),
            pl.BlockSpec((None, n_ctx_tok, gw), lambda bi, g: (bi, 0, GB0 // gw + g)),
        ],
        out_specs=pl.BlockSpec((None, n_ctx_tok, gw), lambda bi, g: (bi, 0, g)),
        out_shape=jax.ShapeDtypeStruct((b, n_ctx_tok, B_Q_HEADS * HEAD_DIM), BF16),
        compiler_params=_params(("parallel", "parallel")),
        name="attn_dense_ctx",
    )(q_ctx, q_ctx, q_ctx, q_ctx)
    return y_lat, y_ctx


def _residual(x, y, mod, w_post):
    return x + mod[2:3] * _rms(y, w_post)


def _attn_out_kernel(ya_ref, yb_ref, wa_ref, wb_ref, x_ref, mod_ref, np_ref, o_ref):
    y = _dot(ya_ref[...], wa_ref[...]) + _dot(yb_ref[...], wb_ref[...])
    o_ref[...] = _residual(x_ref[...], y, mod_ref[...], np_ref[...])


def _attn_out(parts, ya, yb, wa, wb, xs, modl, norm_post):
    b, d = modl.shape[0], modl.shape[-1]

    def call(p):
        return pl.pallas_call(
            _attn_out_kernel,
            grid=(b, p.rows // p.tm),
            in_specs=[
                pl.BlockSpec((None, p.tm, wa.shape[0]), _ROW),
                pl.BlockSpec((None, p.tm, wb.shape[0]), _ROW),
                pl.BlockSpec(wa.shape, _CONST),
                pl.BlockSpec(wb.shape, _CONST),
                pl.BlockSpec((None, p.tm, d), _ROW),
                _mod_spec(p, d),
                pl.BlockSpec((1, d), _CONST),
            ],
            out_specs=pl.BlockSpec((None, p.tm, d), _ROW),
            out_shape=jax.ShapeDtypeStruct((b, p.rows, d), F32),
            compiler_params=_params(("parallel", "parallel")),
            name="attn_out",
        )(ya[p.idx], yb[p.idx], wa, wb, xs[p.idx], modl, norm_post)

    return tuple(call(p) for p in parts)


def _softplus(t):
    return jnp.maximum(t, 0.0) + jnp.log(1.0 + jnp.exp(-jnp.abs(t)))


def _ssm_in_kernel(x_ref, xp_ref, xn_ref, mod_ref, np_ref, wz_ref, wx_ref, wdt_ref, wdtt_ref,
                   cw_ref, cb_ref, dtb_ref, dtbt_ref, z_ref, xbc_ref, dt_ref, dtt_ref):
    i = pl.program_id(1)
    tm = x_ref.shape[0]
    mod = mod_ref[...]
    w_pre = np_ref[...]
    h = _pre_norm(x_ref[...], mod, w_pre)
    hb = h.astype(BF16)
    has_prev = (i > 0).astype(F32)
    has_next = (i < pl.num_programs(1) - 1).astype(F32)
    hp = _pre_norm(xp_ref[...], mod, w_pre) * has_prev
    hn = _pre_norm(xn_ref[...], mod, w_pre) * has_next
    ext = jnp.concatenate([hp, h, hn], axis=0).astype(BF16)

    z_ref[...] = _silu(_dot(hb, wz_ref[...])).astype(BF16)
    dt_ref[...] = _softplus(_dot(hb, wdt_ref[...]) + dtb_ref[...])
    dtt_ref[...] = _softplus(_dot_nt(wdtt_ref[...], hb) + dtbt_ref[...])

    nblk = CONV_COLS
    nt = tm // 8
    sub = lax.broadcasted_iota(jnp.int32, (8, nblk), 0)
    for j in range(wx_ref.shape[1] // nblk):
        cs = slice(j * nblk, (j + 1) * nblk)
        u3 = _dot(ext, wx_ref[:, cs]).reshape(nt + 2, 8, nblk)
        dn = pltpu.roll(u3, 1, 1)
        upw = pltpu.roll(u3, 7, 1)
        u_prev = jnp.where(sub == 0, dn[0:nt], dn[1:nt + 1])
        u_next = jnp.where(sub == 7, upw[2:nt + 2], upw[1:nt + 1])
        conv = (cb_ref[:, cs] + cw_ref[0:1, cs] * u_prev + cw_ref[1:2, cs] * u3[1:nt + 1]
                + cw_ref[2:3, cs] * u_next)
        xbc_ref[:, cs] = _silu(conv).reshape(tm, nblk).astype(BF16)


def _ssm_in(parts, xs, modl, norm_pre, wz, wx, wdt, wdtt, cw, cb, dtb, dtbt):
    b, d = modl.shape[0], modl.shape[-1]
    nh2 = wdt.shape[1]

    def call(p):
        r8 = p.tm // 8
        last8 = p.rows // 8 - 1
        src = xs[p.idx]
        return pl.pallas_call(
            _ssm_in_kernel,
            grid=(b, p.rows // p.tm),
            in_specs=[
                pl.BlockSpec((None, p.tm, d), _ROW),
                pl.BlockSpec((None, 8, d), lambda bi, i: (bi, jnp.maximum(i * r8 - 1, 0), 0)),
                pl.BlockSpec((None, 8, d), lambda bi, i: (bi, jnp.minimum((i + 1) * r8, last8), 0)),
                _mod_spec(p, d),
                pl.BlockSpec((1, d), _CONST),
                pl.BlockSpec(wz.shape, _CONST),
                pl.BlockSpec(wx.shape, _CONST),
                pl.BlockSpec(wdt.shape, _CONST),
                pl.BlockSpec(wdtt.shape, _CONST),
                pl.BlockSpec(cw.shape, _CONST),
                pl.BlockSpec(cb.shape, _CONST),
                pl.BlockSpec(dtb.shape, _CONST),
                pl.BlockSpec(dtbt.shape, _CONST),
            ],
            out_specs=[
                pl.BlockSpec((None, p.tm, wz.shape[1]), _ROW),
                pl.BlockSpec((None, p.tm, wx.shape[1]), _ROW),
                pl.BlockSpec((None, p.tm, nh2), _ROW),
                pl.BlockSpec((None, nh2, p.tm), lambda bi, i: (bi, 0, i)),
            ],
            out_shape=[
                jax.ShapeDtypeStruct((b, p.rows, wz.shape[1]), BF16),
                jax.ShapeDtypeStruct((b, p.rows, wx.shape[1]), BF16),
                jax.ShapeDtypeStruct((b, p.rows, nh2), F32),
                jax.ShapeDtypeStruct((b, nh2, p.rows), F32),
            ],
            compiler_params=_params(("parallel", "parallel")),
            name="ssm_in",
        )(src, src, src, modl, norm_pre, wz, wx, wdt, wdtt, cw, cb, dtb, dtbt)

    lat, cx = (call(p) for p in parts)
    return tuple(zip(lat, cx))


def _split3(t):
    hi = t.astype(BF16)
    r1 = t - hi.astype(F32)
    mid = r1.astype(BF16)
    lo = (r1 - mid.astype(F32)).astype(BF16)
    return hi, mid, lo


def _ones_dot_lhs(tri01, a):
    return _dot(jnp.concatenate([tri01] * 3, axis=1), jnp.concatenate(_split3(a), axis=0))


def _ones_dot_rhs(at, tri01):
    return _dot(jnp.concatenate(_split3(at), axis=1), jnp.concatenate([tri01] * 3, axis=0))


def _ssd_direction(xbc_ref, rs, dt, dtt, a_row, a_col, h_sc, y_ref, reverse, hoff):
    q_len = SSM_CHUNK
    d_inner = SSM_HEADS * SSM_HEAD_DIM
    gw = SSM_REP * SSM_HEAD_DIM
    row = lax.broadcasted_iota(jnp.int32, (q_len, q_len), 0)
    col = lax.broadcasted_iota(jnp.int32, (q_len, q_len), 1)
    lower = row >= col
    upper = row <= col
    mask = upper if reverse else lower
    tri = jnp.where(mask, 1.0, 0.0).astype(BF16)
    tri_t = jnp.where(lower if reverse else upper, 1.0, 0.0).astype(BF16)

    a = dt * a_row
    at = dtt * a_col
    acum = _ones_dot_lhs(tri, a)
    acum_t = _ones_dot_rhs(at, tri_t)
    total_t = jnp.sum(at, axis=1, keepdims=True)
    acum_t = acum_t - jnp.log2(dtt)
    w_t = jnp.exp2(total_t - acum_t)
    etot = jnp.exp2(jnp.sum(a, axis=0, keepdims=True))
    head_of_col = jnp.right_shift(lax.broadcasted_iota(jnp.int32, (q_len, gw), 1),
                                  SSM_HEAD_DIM.bit_length() - 1)

    def group(g):
        b_g = xbc_ref[rs, d_inner + g * D_STATE:d_inner + (g + 1) * D_STATE]
        c_g = xbc_ref[rs, d_inner + (SSM_GROUPS + g) * D_STATE:d_inner + (SSM_GROUPS + g + 1) * D_STATE]
        gsl = slice(g * gw, (g + 1) * gw)
        x_g = xbc_ref[rs, gsl]
        cb = _dot_nt(c_g, b_g)
        c_f = c_g.astype(F32)
        b_t = b_g.astype(F32).T
        h_g = h_sc[g]
        h_b = h_g.astype(BF16)
        ms, ces, bws, bdx, bdh = [], [], [], [], []
        etot_row = None
        for r in range(SSM_REP):
            h = hoff + g * SSM_REP + r
            acb = jnp.broadcast_to(acum[:, h:h + 1], (q_len, q_len))
            decay = jnp.exp2(jnp.where(mask, acb - acum_t[h:h + 1, :], NEG))
            ms.append((cb * decay).astype(BF16))
            ces.append((c_f * jnp.exp2(acb)).astype(BF16))
            bws.append((b_t * w_t[h:h + 1, :]).astype(BF16))
            sel = head_of_col == r
            bdx.append(jnp.where(sel, x_g, jnp.zeros_like(x_g)))
            bdh.append(jnp.where(sel, h_b, jnp.zeros_like(h_b)))
            e_r = jnp.broadcast_to(etot[:, h:h + 1], (1, gw))
            etot_row = e_r if etot_row is None else jnp.where(head_of_col[0:1] == r, e_r, etot_row)
        bdx = jnp.concatenate(bdx, axis=0)
        bdh = jnp.concatenate(bdh, axis=0)
        lhs = jnp.concatenate([jnp.concatenate(ms, axis=1), jnp.concatenate(bws, axis=1)], axis=0)
        res = _dot(lhs, bdx)
        y = res[0:q_len] + _dot(jnp.concatenate(ces, axis=1), bdh)
        y_ref[rs, gsl] = y.astype(BF16)
        h_sc[g] = h_g * etot_row + res[q_len:]

    return group


def _ssd_kernel(*refs, has_init, emit_state):
    xf_ref, xb_ref, dtf_ref, dtb_ref, dttf_ref, dttb_ref, alog_ref, alogt_ref = refs[:8]
    refs = refs[8:]
    if has_init:
        hf0_ref, hb0_ref = refs[:2]
        refs = refs[2:]
    yf_ref, yb_ref = refs[:2]
    refs = refs[2:]
    if emit_state:
        hf_out_ref, hb_out_ref = refs[:2]
        refs = refs[2:]
    hf_sc, hb_sc = refs

    @pl.when(pl.program_id(1) == 0)
    def _():
        if has_init:
            hf_sc[...] = hf0_ref[...]
            hb_sc[...] = hb0_ref[...]
        else:
            hf_sc[...] = jnp.zeros(hf_sc.shape, F32)
            hb_sc[...] = jnp.zeros(hb_sc.shape, F32)

    a_row = -jnp.exp(alog_ref[...]) * LOG2E
    a_col = -jnp.exp(alogt_ref[...]) * LOG2E
    for sub in range(SSD_SUB):
        rs = slice(sub * SSM_CHUNK, (sub + 1) * SSM_CHUNK)
        fwd = _ssd_direction(xf_ref, rs, dtf_ref[rs, :], dttf_ref[:, rs], a_row, a_col, hf_sc, yf_ref, False, 0)
        for g in range(SSM_GROUPS):
            fwd(g)
    for sub in reversed(range(SSD_SUB)):
        rs = slice(sub * SSM_CHUNK, (sub + 1) * SSM_CHUNK)
        bwd = _ssd_direction(xb_ref, rs, dtb_ref[rs, :], dttb_ref[:, rs], a_row, a_col, hb_sc, yb_ref, True,
                             SSM_HEADS)
        for g in range(SSM_GROUPS):
            bwd(g)

    if emit_state:
        hf_out_ref[...] = hf_sc[...]
        hb_out_ref[...] = hb_sc[...]


def _ssd_stream(xbc, dt, dtt, alog, alogt, init, emit_state):
    b, rows, cw = xbc.shape
    q_len = SSD_SUB * SSM_CHUNK
    assert rows % q_len == 0
    n_c = rows // q_len
    d_inner = SSM_HEADS * SSM_HEAD_DIM
    nh2 = dt.shape[-1]
    state_shape = (SSM_GROUPS, D_STATE, SSM_REP * SSM_HEAD_DIM)
    asc = lambda bi, j: (bi, j, 0)
    desc = lambda bi, j: (bi, n_c - 1 - j, 0)
    state_spec = pl.BlockSpec((None,) + state_shape, lambda bi, j: (bi, 0, 0, 0))
    in_specs = [
        pl.BlockSpec((None, q_len, cw), asc),
        pl.BlockSpec((None, q_len, cw), desc),
        pl.BlockSpec((None, q_len, nh2), asc),
        pl.BlockSpec((None, q_len, nh2), desc),
        pl.BlockSpec((None, nh2, q_len), lambda bi, j: (bi, 0, j)),
        pl.BlockSpec((None, nh2, q_len), lambda bi, j: (bi, 0, n_c - 1 - j)),
        pl.BlockSpec(alog.shape, _CONST),
        pl.BlockSpec(alogt.shape, _CONST),
    ]
    args = [xbc, xbc, dt, dt, dtt, dtt, alog, alogt]
    if init is not None:
        in_specs += [state_spec, state_spec]
        args += list(init)
    out_specs = [pl.BlockSpec((None, q_len, d_inner), asc), pl.BlockSpec((None, q_len, d_inner), desc)]
    out_shape = [jax.ShapeDtypeStruct((b, rows, d_inner), BF16)] * 2
    if emit_state:
        out_specs += [state_spec, state_spec]
        out_shape += [jax.ShapeDtypeStruct((b,) + state_shape, F32)] * 2
    return pl.pallas_call(
        functools.partial(_ssd_kernel, has_init=init is not None, emit_state=emit_state),
        grid=(b, n_c),
        in_specs=in_specs,
        out_specs=out_specs,
        out_shape=out_shape,
        scratch_shapes=[pltpu.VMEM(state_shape, F32)] * 2,
        compiler_params=_params(("parallel", "arbitrary")),
        name="ssd_scan",
    )(*args)


def _ssd(xbc, dt, dtt, alog, alogt):
    yf_c, yb_c, hf, hb = _ssd_stream(xbc[1], dt[1], dtt[1], alog, alogt, None, True)
    yf_l, yb_l = _ssd_stream(xbc[0], dt[0], dtt[0], alog, alogt, (hf, hb), False)
    return (yf_l, yf_c), (yb_l, yb_c)


def _ssm_out_kernel(yf_ref, yb_ref, xs_ref, z_ref, dsk_ref, nw_ref, w_ref, x_ref, mod_ref, np_ref, o_ref):
    y = yf_ref[...].astype(F32) + yb_ref[...].astype(F32) + dsk_ref[...] * xs_ref[...].astype(F32)
    gated = y * z_ref[...].astype(F32)
    gsz = gated.shape[1] // SSM_GROUPS
    parts = []
    for g in range(SSM_GROUPS):
        t = gated[:, g * gsz:(g + 1) * gsz]
        parts.append(t * lax.rsqrt(jnp.mean(t * t, axis=-1, keepdims=True) + EPS))
    gn = (jnp.concatenate(parts, axis=1) * nw_ref[...]).astype(BF16)
    o_ref[...] = _residual(x_ref[...], _dot(gn, w_ref[...]), mod_ref[...], np_ref[...])


def _ssm_out(parts, yf, yb, xbc, z, dsk, nw, w, xs, modl, norm_post):
    b, d = modl.shape[0], modl.shape[-1]
    di = w.shape[0]

    def call(p):
        wide = pl.BlockSpec((None, p.tm, di), _ROW)
        return pl.pallas_call(
            _ssm_out_kernel,
            grid=(b, p.rows // p.tm),
            in_specs=[
                wide, wide, wide, wide,
                pl.BlockSpec((1, di), _CONST),
                pl.BlockSpec((1, di), _CONST),
                pl.BlockSpec(w.shape, _CONST),
                pl.BlockSpec((None, p.tm, d), _ROW),
                _mod_spec(p, d),
                pl.BlockSpec((1, d), _CONST),
            ],
            out_specs=pl.BlockSpec((None, p.tm, d), _ROW),
            out_shape=jax.ShapeDtypeStruct((b, p.rows, d), F32),
            compiler_params=_params(("parallel", "parallel")),
            name="ssm_out",
        )(yf[p.idx], yb[p.idx], xbc[p.idx], z[p.idx], dsk, nw, w, xs[p.idx], modl, norm_post)

    return tuple(call(p) for p in parts)


def _rope_tables(n_lat_tok, n_ctx_tok):
    t = np.arange(n_lat_tok)
    n_freq = HEAD_DIM // 4
    inv = 1.0 / (ROPE_THETA ** (jnp.arange(n_freq, dtype=F32) / n_freq))
    rowp = jnp.asarray(t // GRID_W, F32)
    colp = jnp.asarray(t % GRID_W, F32)
    ang = jnp.concatenate([rowp[:, None] * inv, colp[:, None] * inv], axis=-1)
    cos, sin = jnp.cos(ang), jnp.sin(ang)
    lat = (jnp.concatenate([cos, cos], axis=-1), jnp.concatenate([-sin, sin], axis=-1))
    cx = (jnp.ones((n_ctx_tok, HEAD_DIM), F32), jnp.zeros((n_ctx_tok, HEAD_DIM), F32))
    return lat, cx


_DEINT = np.concatenate([np.arange(0, HEAD_DIM, 2), np.arange(1, HEAD_DIM, 2)])


def _attn_weight_columns():
    qa, ka, va, ga, qb, kb, vb, gb = 0, 1024, 1280, 1536, 2560, 3584, 3840, 4096

    def heads(start, n, perm):
        base = start + HEAD_DIM * np.arange(n)[:, None]
        return (base + (_DEINT if perm else np.arange(HEAD_DIM))[None, :]).reshape(-1)

    return np.concatenate([
        heads(qa, 8, True), heads(ga, 8, False), heads(qb, 8, True), heads(gb, 8, False),
        heads(ka, 2, True), heads(va, 2, False), heads(kb, 2, True), heads(vb, 2, False)])


_ATTN_COLS_IDX = _attn_weight_columns()


def kernel(x, c, ctx, c_ctx, w_ada, b_ada, norm_pre, norm_post, attn_w_in, attn_w_out, attn_sink,
           attn_q_norm, attn_k_norm, ssm_w_in, ssm_conv_w, ssm_conv_b, ssm_dt_bias, ssm_a_log, ssm_d,
           ssm_norm, ssm_w_out):
    bsz, n_lat_tok, d = x.shape
    n_ctx_tok = ctx.shape[1]
    depth = w_ada.shape[0]
    assert n_lat_tok % TM_OUT == 0 and n_lat_tok % TM == 0 and n_ctx_tok % LANES == 0 and bsz <= 7
    d_inner = SSM_HEADS * SSM_HEAD_DIM
    bc_w = 2 * SSM_GROUPS * D_STATE

    xs = (x, ctx)
    in_parts = _parts(n_lat_tok, n_ctx_tok, TM)
    out_parts = _parts(n_lat_tok, n_ctx_tok, TM_OUT)
    cc = jnp.zeros((8, d), F32).at[:bsz].set(c).at[bsz].set(c_ctx)
    mod = _modulation(cc, w_ada, b_ada)
    mod = mod.reshape(depth, 8, 3, d)
    tables = _rope_tables(n_lat_tok, n_ctx_tok)

    for l in range(depth):
        l_out_parts = out_parts[:1] if l == depth - 1 else out_parts
        modl = jnp.stack([mod[l, :bsz], jnp.broadcast_to(mod[l, bsz], (bsz, 3, d))], axis=1)
        npre = norm_pre[l].reshape(1, d)
        npost = norm_post[l].reshape(1, d)
        i = l // 2
        if l % 2 == 0:
            w = attn_w_in[i][:, _ATTN_COLS_IDX].astype(BF16)
            qn = attn_q_norm[i][_DEINT].reshape(1, HEAD_DIM)
            kn = attn_k_norm[i][_DEINT].reshape(1, HEAD_DIM)
            qkv = _attn_in(in_parts, xs, modl, npre, w, tables, qn, kn)
            sink_b = jnp.broadcast_to(attn_sink[i][:, None], (A_Q_HEADS, HEAD_DIM))
            ya = _attn_a(qkv, sink_b)
            yb = _attn_b(qkv)
            wo = attn_w_out[i].astype(BF16)
            aq = A_Q_HEADS * HEAD_DIM
            xs = _attn_out(l_out_parts, ya, yb, wo[:aq], wo[aq:], xs, modl, npost)
        else:
            w = ssm_w_in[i]
            wz = w[:, :d_inner].astype(BF16)
            wx = w[:, d_inner:2 * d_inner + bc_w].astype(BF16)
            wdt = jnp.pad(w[:, 2 * d_inner + bc_w:], ((0, 0), (0, HPAD - 2 * SSM_HEADS))).astype(BF16)
            dtb = jnp.pad(ssm_dt_bias[i].reshape(1, -1), ((0, 0), (0, HPAD - 2 * SSM_HEADS)))
            z, xbc, dt, dtt = _ssm_in(in_parts, xs, modl, npre, wz, wx, wdt, wdt.T, ssm_conv_w[i],
                                      ssm_conv_b[i].reshape(1, -1), dtb, dtb.reshape(-1, 1))
            alog = jnp.pad(ssm_a_log[i].reshape(1, -1), ((0, 0), (0, HPAD - 2 * SSM_HEADS)))
            yf, ybk = _ssd(xbc, dt, dtt, alog, alog.reshape(-1, 1))
            dsk = jnp.repeat(ssm_d[i], SSM_HEAD_DIM).reshape(1, d_inner)
            xs = _ssm_out(l_out_parts, yf, ybk, xbc, z, dsk, ssm_norm[i].reshape(1, d_inner),
                          ssm_w_out[i].astype(BF16), xs, modl, npost)
    return xs[0]
```

```python
import functools
from typing import NamedTuple

import numpy as np
import jax
import jax.numpy as jnp
from jax import lax
from jax.experimental import pallas as pl
from jax.experimental.pallas import tpu as pltpu

F32 = jnp.float32
BF16 = jnp.bfloat16

EPS = 1e-6
GRID_W = 64
ROPE_THETA = 10000.0
HEAD_DIM = 128
A_Q_HEADS = 8
A_KV_HEADS = 2
B_Q_HEADS = 8
B_KV_HEADS = 2
REP = 4
WINDOW = 128
SSM_HEAD_DIM = 64
SSM_HEADS = 32
SSM_GROUPS = 8
SSM_REP = SSM_HEADS // SSM_GROUPS
D_STATE = 128
SSM_CHUNK = 128
SSD_SUB = 4
HPAD = 128

V7X_VMEM_BYTES = 64 * 1024 * 1024
VMEM_LIMIT = V7X_VMEM_BYTES - 8 * 1024 * 1024

TM = 256
TM_OUT = 512
CONV_COLS = 512
TQ_A = 128
NB_A = 8
TQ_B = 512
TK_B = 512
LANES = 128
LOG2E = 1.4426950408889634

QA0, GA0, QB0, GB0, KA0, VA0, KB0, VB0 = 0, 1024, 2048, 3072, 4096, 4352, 4608, 4864
ATTN_COLS = 5120

NEG = -1e30


def _params(sem, vmem=VMEM_LIMIT):
    return pltpu.CompilerParams(dimension_semantics=sem, vmem_limit_bytes=vmem)


def _silu(t):
    return t * (1.0 / (1.0 + jnp.exp2(t * (-LOG2E))))


def _rms(t, w):
    return t * lax.rsqrt(jnp.mean(t * t, axis=-1, keepdims=True) + EPS) * w


def _dot(a, b):
    return jnp.dot(a, b, preferred_element_type=F32)


def _dot_nt(a, b):
    return lax.dot_general(a, b, (((1,), (1,)), ((), ())), preferred_element_type=F32)


def _dot_hi(a, b):
    return jnp.dot(a, b, preferred_element_type=F32, precision=lax.Precision.HIGHEST)


class _Part(NamedTuple):
    idx: int
    rows: int
    tm: int


def _parts(n_lat_tok, n_ctx_tok, tm):
    return (_Part(0, n_lat_tok, tm), _Part(1, n_ctx_tok, n_ctx_tok))


_ROW = lambda bi, i: (bi, i, 0)
_CONST = lambda bi, i: (0, 0)


def _mod_spec(p, d):
    return pl.BlockSpec((None, None, 3, d), lambda bi, i: (bi, p.idx, 0, 0))


def _mod_kernel(cc_ref, w_ref, b_ref, o_ref):
    o_ref[...] = _dot_hi(_silu(cc_ref[...]), w_ref[...]) + b_ref[...]


def _modulation(cc, w_ada, b_ada):
    depth, d, d3 = w_ada.shape
    return pl.pallas_call(
        _mod_kernel,
        grid=(depth,),
        in_specs=[
            pl.BlockSpec((8, d), lambda l: (0, 0)),
            pl.BlockSpec((None, d, d3), lambda l: (l, 0, 0)),
            pl.BlockSpec((None, 1, d3), lambda l: (l, 0, 0)),
        ],
        out_specs=pl.BlockSpec((None, 8, d3), lambda l: (l, 0, 0)),
        out_shape=jax.ShapeDtypeStruct((depth, 8, d3), F32),
        compiler_params=_params(("arbitrary",)),
        name="modulation",
    )(cc, w_ada, b_ada.reshape(depth, 1, d3))


def _pre_norm(x, mod, w):
    return _rms(x, w) * (1.0 + mod[1:2]) + mod[0:1]


def _attn_in_kernel(x_ref, mod_ref, np_ref, w_ref, cos_ref, sin_ref, qn_ref, kn_ref, o_ref):
    hb = _pre_norm(x_ref[...], mod_ref[...], np_ref[...]).astype(BF16)
    cos = cos_ref[...]
    sin = sin_ref[...]
    scale = HEAD_DIM ** -0.5

    def rope(t):
        return t * cos + pltpu.roll(t, HEAD_DIM // 2, 1) * sin

    nblk = 512
    for j in range(ATTN_COLS // nblk):
        c0 = j * nblk
        t = _dot(hb, w_ref[:, c0:c0 + nblk])
        for hh in range(nblk // HEAD_DIM):
            col = c0 + hh * HEAD_DIM
            th = t[:, hh * HEAD_DIM:(hh + 1) * HEAD_DIM]
            if col < GA0:
                th = rope(th) * (scale * LOG2E)
            elif col < QB0 or GB0 <= col < KA0:
                th = _silu(th)
            elif col < GB0:
                th = rope(_rms(th, qn_ref[...])) * (scale * LOG2E)
            elif col < VA0:
                th = rope(th)
            elif KB0 <= col < VB0:
                th = rope(_rms(th, kn_ref[...]))
            o_ref[:, col:col + HEAD_DIM] = th.astype(BF16)


def _attn_in(parts, xs, modl, norm_pre, w, tables, qn, kn):
    b, d = modl.shape[0], modl.shape[-1]

    def call(p):
        table = pl.BlockSpec((p.tm, HEAD_DIM), lambda bi, i: (i, 0))
        return pl.pallas_call(
            _attn_in_kernel,
            grid=(b, p.rows // p.tm),
            in_specs=[
                pl.BlockSpec((None, p.tm, d), _ROW),
                _mod_spec(p, d),
                pl.BlockSpec((1, d), _CONST),
                pl.BlockSpec((d, ATTN_COLS), _CONST),
                table,
                table,
                pl.BlockSpec((1, HEAD_DIM), _CONST),
                pl.BlockSpec((1, HEAD_DIM), _CONST),
            ],
            out_specs=pl.BlockSpec((None, p.tm, ATTN_COLS), _ROW),
            out_shape=jax.ShapeDtypeStruct((b, p.rows, ATTN_COLS), BF16),
            compiler_params=_params(("parallel", "parallel")),
            name="attn_in",
        )(xs[p.idx], modl, norm_pre, w, *tables[p.idx], qn, kn)

    return tuple(call(p) for p in parts)


def _sink_column(sink_ref, g, n):
    return jnp.concatenate(
        [jnp.broadcast_to(sink_ref[g * REP + h:g * REP + h + 1, 0:1] * LOG2E, (n, 1)) for h in range(REP)], axis=0)


def _attn_a_kernel(q_ref, kp_ref, kc_ref, kn_ref, vp_ref, vc_ref, vn_ref, kx_ref, vx_ref,
                   g_ref, sink_ref, o_ref, *, n_lat):
    i = pl.program_id(1)
    tq = TQ_A
    rows = REP * tq
    r = lax.broadcasted_iota(jnp.int32, (rows, tq), 0) & (tq - 1)
    c = lax.broadcasted_iota(jnp.int32, (rows, tq), 1)
    for a in range(NB_A):
        blk = NB_A * i + a
        rq = slice(a * tq, (a + 1) * tq)
        m_prev = (c - r) >= jnp.where(blk > 0, 0, tq)
        m_next = (r - c) >= jnp.where(blk < n_lat - 1, 0, tq)
        for g in range(A_KV_HEADS):
            gs = slice(g * HEAD_DIM, (g + 1) * HEAD_DIM)
            if a == 0:
                k_p, v_p = kp_ref[:, gs], vp_ref[:, gs]
            else:
                k_p, v_p = kc_ref[(a - 1) * tq:a * tq, gs], vc_ref[(a - 1) * tq:a * tq, gs]
            if a == NB_A - 1:
                k_n, v_n = kn_ref[:, gs], vn_ref[:, gs]
            else:
                k_n, v_n = kc_ref[(a + 1) * tq:(a + 2) * tq, gs], vc_ref[(a + 1) * tq:(a + 2) * tq, gs]
            k_c, v_c = kc_ref[rq, gs], vc_ref[rq, gs]
            q = jnp.concatenate(
                [q_ref[rq, (g * REP + h) * HEAD_DIM:(g * REP + h + 1) * HEAD_DIM] for h in range(REP)], axis=0)
            sp = jnp.where(m_prev, _dot_nt(q, k_p), NEG)
            sc = _dot_nt(q, k_c)
            sn = jnp.where(m_next, _dot_nt(q, k_n), NEG)
            sx = _dot_nt(q, kx_ref[:, gs])
            sk = _sink_column(sink_ref, g, tq)
            sx0, sx1 = sx[:, 0:tq], sx[:, tq:2 * tq]
            m_t = jnp.maximum(jnp.maximum(jnp.maximum(sp, sc), jnp.maximum(sn, sx0)), sx1)
            m = jnp.maximum(jnp.max(m_t, axis=1, keepdims=True), sk)
            pp = jnp.exp2(sp - m)
            pc = jnp.exp2(sc - m)
            pn = jnp.exp2(sn - m)
            px = jnp.exp2(sx - m)
            den_t = (pp + pc) + (pn + px[:, 0:tq]) + px[:, tq:2 * tq]
            den = jnp.sum(den_t, axis=1, keepdims=True) + jnp.exp2(sk - m)
            o = (_dot(pp.astype(BF16), v_p) + _dot(pc.astype(BF16), v_c)
                 + _dot(pn.astype(BF16), v_n) + _dot(px.astype(BF16), vx_ref[:, gs])) * (1.0 / den)
            for h in range(REP):
                cs = slice((g * REP + h) * HEAD_DIM, (g * REP + h + 1) * HEAD_DIM)
                o_ref[rq, cs] = (o[h * tq:(h + 1) * tq] * g_ref[rq, cs].astype(F32)).astype(BF16)


def _attn_a_ctx_kernel(q_ref, kx_ref, vx_ref, g_ref, sink_ref, o_ref):
    n = q_ref.shape[0]
    for g in range(A_KV_HEADS):
        gs = slice(g * HEAD_DIM, (g + 1) * HEAD_DIM)
        q = jnp.concatenate(
            [q_ref[:, (g * REP + h) * HEAD_DIM:(g * REP + h + 1) * HEAD_DIM] for h in range(REP)], axis=0)
        sx = _dot_nt(q, kx_ref[:, gs])
        sk = _sink_column(sink_ref, g, n)
        m = jnp.maximum(jnp.max(sx, axis=1, keepdims=True), sk)
        px = jnp.exp2(sx - m)
        den = jnp.sum(px, axis=1, keepdims=True) + jnp.exp2(sk - m)
        o = _dot(px.astype(BF16), vx_ref[:, gs]) * (1.0 / den)
        for h in range(REP):
            cs = slice((g * REP + h) * HEAD_DIM, (g * REP + h + 1) * HEAD_DIM)
            o_ref[:, cs] = (o[h * n:(h + 1) * n] * g_ref[:, cs].astype(F32)).astype(BF16)


def _attn_a(qkv, sink_b):
    q_lat, q_ctx = qkv
    b, n_lat_tok, _ = q_lat.shape
    ctx_len = q_ctx.shape[1]
    tq = TQ_A
    n_lat = n_lat_tok // tq
    kvw = A_KV_HEADS * HEAD_DIM
    step = NB_A * tq
    assert ctx_len == 2 * tq and n_lat_tok % step == 0
    qw = A_Q_HEADS * HEAD_DIM
    prev = lambda cb: (lambda bi, i: (bi, jnp.maximum(NB_A * i - 1, 0), cb))
    cur = lambda cb: (lambda bi, i: (bi, i, cb))
    nxt = lambda cb: (lambda bi, i: (bi, jnp.minimum(NB_A * (i + 1), n_lat - 1), cb))
    whole = lambda cb: (lambda bi, i: (bi, 0, cb))
    kb, vb = KA0 // kvw, VA0 // kvw
    sink_spec = pl.BlockSpec((A_Q_HEADS, HEAD_DIM), _CONST)
    y_lat = pl.pallas_call(
        functools.partial(_attn_a_kernel, n_lat=n_lat),
        grid=(b, n_lat_tok // step),
        in_specs=[
            pl.BlockSpec((None, step, qw), cur(QA0 // qw)),
            pl.BlockSpec((None, tq, kvw), prev(kb)),
            pl.BlockSpec((None, step, kvw), cur(kb)),
            pl.BlockSpec((None, tq, kvw), nxt(kb)),
            pl.BlockSpec((None, tq, kvw), prev(vb)),
            pl.BlockSpec((None, step, kvw), cur(vb)),
            pl.BlockSpec((None, tq, kvw), nxt(vb)),
            pl.BlockSpec((None, ctx_len, kvw), whole(kb)),
            pl.BlockSpec((None, ctx_len, kvw), whole(vb)),
            pl.BlockSpec((None, step, qw), cur(GA0 // qw)),
            sink_spec,
        ],
        out_specs=pl.BlockSpec((None, step, qw), _ROW),
        out_shape=jax.ShapeDtypeStruct((b, n_lat_tok, qw), BF16),
        compiler_params=_params(("parallel", "parallel")),
        name="attn_window",
    )(q_lat, q_lat, q_lat, q_lat, q_lat, q_lat, q_lat, q_ctx, q_ctx, q_lat, sink_b)
    y_ctx = pl.pallas_call(
        _attn_a_ctx_kernel,
        grid=(b, 1),
        in_specs=[
            pl.BlockSpec((None, ctx_len, qw), whole(QA0 // qw)),
            pl.BlockSpec((None, ctx_len, kvw), whole(kb)),
            pl.BlockSpec((None, ctx_len, kvw), whole(vb)),
            pl.BlockSpec((None, ctx_len, qw), whole(GA0 // qw)),
            sink_spec,
        ],
        out_specs=pl.BlockSpec((None, ctx_len, qw), _ROW),
        out_shape=jax.ShapeDtypeStruct((b, ctx_len, qw), BF16),
        compiler_params=_params(("parallel", "parallel")),
        name="attn_window_ctx",
    )(q_ctx, q_ctx, q_ctx, q_ctx, sink_b)
    return y_lat, y_ctx


def _attn_b_kernel(q_ref, k_ref, v_ref, kx_ref, vx_ref, g_ref, o_ref, m_sc, l_sc, acc_sc, p_sc, alpha_sc):
    tq, tk = TQ_B, TK_B
    n_lat_tok = k_ref.shape[0]
    q = jnp.concatenate([q_ref[:, h * HEAD_DIM:(h + 1) * HEAD_DIM] for h in range(REP)], axis=0)
    m_sc[...] = jnp.full(m_sc.shape, NEG, F32)
    l_sc[...] = jnp.zeros(l_sc.shape, F32)
    acc_sc[...] = jnp.zeros(acc_sc.shape, F32)

    rb = 128

    def scores(k, slot):
        nk = k.shape[0]
        s = _dot_nt(q, k)
        for r0 in range(0, REP * tq, rb):
            rs = slice(r0, r0 + rb)
            s_b = s[rs]
            m_old = m_sc[rs]
            m_new = jnp.maximum(m_old, jnp.max(s_b, axis=1, keepdims=True))
            alpha = jnp.exp2(m_old - m_new)
            p = jnp.exp2(s_b - jnp.tile(m_new, (1, nk // LANES)))
            l_sc[rs] = alpha * l_sc[rs] + jnp.sum(p, axis=1, keepdims=True)
            m_sc[rs] = m_new
            alpha_sc[slot, rs] = alpha
            p_sc[slot, rs, 0:nk] = p.astype(BF16)

    def accumulate(slot, v):
        nk = v.shape[0]
        acc_sc[...] = alpha_sc[slot] * acc_sc[...] + _dot(p_sc[slot, :, 0:nk], v)

    def kv(ref, ci):
        if isinstance(ci, int):
            return ref[ci * tk:(ci + 1) * tk, :]
        return ref[pl.ds(pl.multiple_of(ci * tk, tk), tk), :]

    n_main = n_lat_tok // tk
    assert n_main % 2 == 0 and n_lat_tok % tk == 0
    scores(kv(k_ref, 0), 0)

    def pair(c1):
        scores(kv(k_ref, c1), 1)
        accumulate(0, kv(v_ref, c1 - 1))
        scores(kv(k_ref, c1 + 1), 0)
        accumulate(1, kv(v_ref, c1))

    def body(j, carry):
        pair(4 * j + 1)
        pair(4 * j + 3)
        return carry

    n_quads = (n_main - 2) // 4
    lax.fori_loop(0, n_quads, body, 0)
    for c1 in range(4 * n_quads + 1, n_main - 1, 2):
        pair(c1)
    scores(kv(k_ref, n_main - 1), 1)
    accumulate(0, kv(v_ref, n_main - 2))
    scores(kx_ref[...], 0)
    accumulate(1, kv(v_ref, n_main - 1))
    accumulate(0, vx_ref[...])

    o = acc_sc[...] * (1.0 / l_sc[...])
    for h in range(REP):
        cs = slice(h * HEAD_DIM, (h + 1) * HEAD_DIM)
        o_ref[:, cs] = (o[h * tq:(h + 1) * tq] * g_ref[:, cs].astype(F32)).astype(BF16)


def _attn_b_ctx_kernel(q_ref, k_ref, v_ref, g_ref, o_ref):
    n = q_ref.shape[0]
    q = jnp.concatenate([q_ref[:, h * HEAD_DIM:(h + 1) * HEAD_DIM] for h in range(REP)], axis=0)
    s = _dot_nt(q, k_ref[...])
    p = jnp.exp2(s - jnp.max(s, axis=1, keepdims=True))
    o = _dot(p.astype(BF16), v_ref[...]) * (1.0 / jnp.sum(p, axis=1, keepdims=True))
    for h in range(REP):
        cs = slice(h * HEAD_DIM, (h + 1) * HEAD_DIM)
        o_ref[:, cs] = (o[h * n:(h + 1) * n] * g_ref[:, cs].astype(F32)).astype(BF16)


def _attn_b(qkv):
    q_lat, q_ctx = qkv
    b, n_lat_tok, _ = q_lat.shape
    n_ctx_tok = q_ctx.shape[1]
    tq = TQ_B
    gw = REP * HEAD_DIM
    assert n_lat_tok % tq == 0 and n_ctx_tok <= TK_B and n_ctx_tok % LANES == 0
    kcol, vcol = KB0 // HEAD_DIM, VB0 // HEAD_DIM
    y_lat = pl.pallas_call(
        _attn_b_kernel,
        grid=(b, B_KV_HEADS, n_lat_tok // tq),
        in_specs=[
            pl.BlockSpec((None, tq, gw), lambda bi, g, i: (bi, i, QB0 // gw + g)),
            pl.BlockSpec((None, n_lat_tok, HEAD_DIM), lambda bi, g, i: (bi, 0, kcol + g)),
            pl.BlockSpec((None, n_lat_tok, HEAD_DIM), lambda bi, g, i: (bi, 0, vcol + g)),
            pl.BlockSpec((None, n_ctx_tok, HEAD_DIM), lambda bi, g, i: (bi, 0, kcol + g)),
            pl.BlockSpec((None, n_ctx_tok, HEAD_DIM), lambda bi, g, i: (bi, 0, vcol + g)),
            pl.BlockSpec((None, tq, gw), lambda bi, g, i: (bi, i, GB0 // gw + g)),
        ],
        out_specs=pl.BlockSpec((None, tq, gw), lambda bi, g, i: (bi, i, g)),
        out_shape=jax.ShapeDtypeStruct((b, n_lat_tok, B_Q_HEADS * HEAD_DIM), BF16),
        scratch_shapes=[
            pltpu.VMEM((REP * tq, LANES), F32),
            pltpu.VMEM((REP * tq, LANES), F32),
            pltpu.VMEM((REP * tq, HEAD_DIM), F32),
            pltpu.VMEM((2, REP * tq, TK_B), BF16),
            pltpu.VMEM((2, REP * tq, LANES), F32),
        ],
        compiler_params=_params(("parallel", "parallel", "parallel")),
        name="attn_dense",
    )(q_lat, q_lat, q_lat, q_ctx, q_ctx, q_lat)
    y_ctx = pl.pallas_call(
        _attn_b_ctx_kernel,
        grid=(b, B_KV_HEADS),
        in_specs=[
            pl.BlockSpec((None, n_ctx_tok, gw), lambda bi, g: (bi, 0, QB0 // gw + g)),
            pl.BlockSpec((None, n_ctx_tok, HEAD_DIM), lambda bi, g: (bi, 0, kcol + g)),
            pl.BlockSpec((None, n_ctx_tok, HEAD_DIM), lambda bi, g: (bi, 0, vcol + g)),
            pl.BlockSpec((None, n_ctx_tok, gw), lambda bi, g: (bi, 0, GB0 // gw + g)),
        ],
        out_specs=pl.BlockSpec((None, n_ctx_tok, gw), lambda bi, g: (bi, 0, g)),
        out_shape=jax.ShapeDtypeStruct((b, n_ctx_tok, B_Q_HEADS * HEAD_DIM), BF16),
        compiler_params=_params(("parallel", "parallel")),
        name="attn_dense_ctx",
    )(q_ctx, q_ctx, q_ctx, q_ctx)
    return y_lat, y_ctx


def _residual(x, y, mod, w_post):
    return x + mod[2:3] * _rms(y, w_post)


def _attn_out_kernel(ya_ref, yb_ref, wa_ref, wb_ref, x_ref, mod_ref, np_ref, o_ref):
    y = _dot(ya_ref[...], wa_ref[...]) + _dot(yb_ref[...], wb_ref[...])
    o_ref[...] = _residual(x_ref[...], y, mod_ref[...], np_ref[...])


def _attn_out(parts, ya, yb, wa, wb, xs, modl, norm_post):
    b, d = modl.shape[0], modl.shape[-1]

    def call(p):
        return pl.pallas_call(
            _attn_out_kernel,
            grid=(b, p.rows // p.tm),
            in_specs=[
                pl.BlockSpec((None, p.tm, wa.shape[0]), _ROW),
                pl.BlockSpec((None, p.tm, wb.shape[0]), _ROW),
                pl.BlockSpec(wa.shape, _CONST),
                pl.BlockSpec(wb.shape, _CONST),
                pl.BlockSpec((None, p.tm, d), _ROW),
                _mod_spec(p, d),
                pl.BlockSpec((1, d), _CONST),
            ],
            out_specs=pl.BlockSpec((None, p.tm, d), _ROW),
            out_shape=jax.ShapeDtypeStruct((b, p.rows, d), F32),
            compiler_params=_params(("parallel", "parallel")),
            name="attn_out",
        )(ya[p.idx], yb[p.idx], wa, wb, xs[p.idx], modl, norm_post)

    return tuple(call(p) for p in parts)


def _softplus(t):
    return jnp.maximum(t, 0.0) + jnp.log(1.0 + jnp.exp(-jnp.abs(t)))


def _ssm_in_kernel(x_ref, xp_ref, xn_ref, mod_ref, np_ref, wz_ref, wx_ref, wdt_ref, wdtt_ref,
                   cw_ref, cb_ref, dtb_ref, dtbt_ref, z_ref, xbc_ref, dt_ref, dtt_ref):
    i = pl.program_id(1)
    tm = x_ref.shape[0]
    mod = mod_ref[...]
    w_pre = np_ref[...]
    h = _pre_norm(x_ref[...], mod, w_pre)
    hb = h.astype(BF16)
    has_prev = (i > 0).astype(F32)
    has_next = (i < pl.num_programs(1) - 1).astype(F32)
    hp = _pre_norm(xp_ref[...], mod, w_pre) * has_prev
    hn = _pre_norm(xn_ref[...], mod, w_pre) * has_next
    ext = jnp.concatenate([hp, h, hn], axis=0).astype(BF16)

    z_ref[...] = _silu(_dot(hb, wz_ref[...])).astype(BF16)
    dt_ref[...] = _softplus(_dot(hb, wdt_ref[...]) + dtb_ref[...])
    dtt_ref[...] = _softplus(_dot_nt(wdtt_ref[...], hb) + dtbt_ref[...])

    nblk = CONV_COLS
    nt = tm // 8
    sub = lax.broadcasted_iota(jnp.int32, (8, nblk), 0)
    for j in range(wx_ref.shape[1] // nblk):
        cs = slice(j * nblk, (j + 1) * nblk)
        u3 = _dot(ext, wx_ref[:, cs]).reshape(nt + 2, 8, nblk)
        dn = pltpu.roll(u3, 1, 1)
        upw = pltpu.roll(u3, 7, 1)
        u_prev = jnp.where(sub == 0, dn[0:nt], dn[1:nt + 1])
        u_next = jnp.where(sub == 7, upw[2:nt + 2], upw[1:nt + 1])
        conv = (cb_ref[:, cs] + cw_ref[0:1, cs] * u_prev + cw_ref[1:2, cs] * u3[1:nt + 1]
                + cw_ref[2:3, cs] * u_next)
        xbc_ref[:, cs] = _silu(conv).reshape(tm, nblk).astype(BF16)


def _ssm_in(parts, xs, modl, norm_pre, wz, wx, wdt, wdtt, cw, cb, dtb, dtbt):
    b, d = modl.shape[0], modl.shape[-1]
    nh2 = wdt.shape[1]

    def call(p):
        r8 = p.tm // 8
        last8 = p.rows // 8 - 1
        src = xs[p.idx]
        return pl.pallas_call(
            _ssm_in_kernel,
            grid=(b, p.rows // p.tm),
            in_specs=[
                pl.BlockSpec((None, p.tm, d), _ROW),
                pl.BlockSpec((None, 8, d), lambda bi, i: (bi, jnp.maximum(i * r8 - 1, 0), 0)),
                pl.BlockSpec((None, 8, d), lambda bi, i: (bi, jnp.minimum((i + 1) * r8, last8), 0)),
                _mod_spec(p, d),
                pl.BlockSpec((1, d), _CONST),
                pl.BlockSpec(wz.shape, _CONST),
                pl.BlockSpec(wx.shape, _CONST),
                pl.BlockSpec(wdt.shape, _CONST),
                pl.BlockSpec(wdtt.shape, _CONST),
                pl.BlockSpec(cw.shape, _CONST),
                pl.BlockSpec(cb.shape, _CONST),
                pl.BlockSpec(dtb.shape, _CONST),
                pl.BlockSpec(dtbt.shape, _CONST),
            ],
            out_specs=[
                pl.BlockSpec((None, p.tm, wz.shape[1]), _ROW),
                pl.BlockSpec((None, p.tm, wx.shape[1]), _ROW),
                pl.BlockSpec((None, p.tm, nh2), _ROW),
                pl.BlockSpec((None, nh2, p.tm), lambda bi, i: (bi, 0, i)),
            ],
            out_shape=[
                jax.ShapeDtypeStruct((b, p.rows, wz.shape[1]), BF16),
                jax.ShapeDtypeStruct((b, p.rows, wx.shape[1]), BF16),
                jax.ShapeDtypeStruct((b, p.rows, nh2), F32),
                jax.ShapeDtypeStruct((b, nh2, p.rows), F32),
            ],
            compiler_params=_params(("parallel", "parallel")),
            name="ssm_in",
        )(src, src, src, modl, norm_pre, wz, wx, wdt, wdtt, cw, cb, dtb, dtbt)

    lat, cx = (call(p) for p in parts)
    return tuple(zip(lat, cx))


def _split3(t):
    hi = t.astype(BF16)
    r1 = t - hi.astype(F32)
    mid = r1.astype(BF16)
    lo = (r1 - mid.astype(F32)).astype(BF16)
    return hi, mid, lo


def _ones_dot_lhs(tri01, a):
    return _dot(jnp.concatenate([tri01] * 3, axis=1), jnp.concatenate(_split3(a), axis=0))


def _ones_dot_rhs(at, tri01):
    return _dot(jnp.concatenate(_split3(at), axis=1), jnp.concatenate([tri01] * 3, axis=0))


def _ssd_direction(xbc_ref, rs, dt, dtt, a_row, a_col, h_sc, y_ref, reverse, hoff):
    q_len = SSM_CHUNK
    d_inner = SSM_HEADS * SSM_HEAD_DIM
    gw = SSM_REP * SSM_HEAD_DIM
    row = lax.broadcasted_iota(jnp.int32, (q_len, q_len), 0)
    col = lax.broadcasted_iota(jnp.int32, (q_len, q_len), 1)
    lower = row >= col
    upper = row <= col
    mask = upper if reverse else lower
    tri = jnp.where(mask, 1.0, 0.0).astype(BF16)
    tri_t = jnp.where(lower if reverse else upper, 1.0, 0.0).astype(BF16)

    a = dt * a_row
    at = dtt * a_col
    acum = _ones_dot_lhs(tri, a)
    acum_t = _ones_dot_rhs(at, tri_t)
    total_t = jnp.sum(at, axis=1, keepdims=True)
    acum_t = acum_t - jnp.log2(dtt)
    w_t = jnp.exp2(total_t - acum_t)
    etot = jnp.exp2(jnp.sum(a, axis=0, keepdims=True))
    head_of_col = jnp.right_shift(lax.broadcasted_iota(jnp.int32, (q_len, gw), 1),
                                  SSM_HEAD_DIM.bit_length() - 1)

    def group(g):
        b_g = xbc_ref[rs, d_inner + g * D_STATE:d_inner + (g + 1) * D_STATE]
        c_g = xbc_ref[rs, d_inner + (SSM_GROUPS + g) * D_STATE:d_inner + (SSM_GROUPS + g + 1) * D_STATE]
        gsl = slice(g * gw, (g + 1) * gw)
        x_g = xbc_ref[rs, gsl]
        cb = _dot_nt(c_g, b_g)
        b_t = b_g.astype(F32).T
        h_g = h_sc[g]
        y_state = _dot(c_g, h_g.astype(BF16))
        ms, bws, bdx = [], [], []
        etot_row = None
        e_in = None
        for r in range(SSM_REP):
            h = hoff + g * SSM_REP + r
            acb = jnp.broadcast_to(acum[:, h:h + 1], (q_len, q_len))
            decay = jnp.exp2(jnp.where(mask, acb - acum_t[h:h + 1, :], NEG))
            ms.append((cb * decay).astype(BF16))
            bws.append((b_t * w_t[h:h + 1, :]).astype(BF16))
            sel = head_of_col == r
            bdx.append(jnp.where(sel, x_g, jnp.zeros_like(x_g)))
            e_q = jnp.exp2(acb)
            e_q = jnp.concatenate([e_q] * (gw // q_len), axis=1)
            e_in = e_q if e_in is None else jnp.where(sel, e_q, e_in)
            e_r = jnp.broadcast_to(etot[:, h:h + 1], (1, gw))
            etot_row = e_r if etot_row is None else jnp.where(head_of_col[0:1] == r, e_r, etot_row)
        bdx = jnp.concatenate(bdx, axis=0)
        lhs = jnp.concatenate([jnp.concatenate(ms, axis=1), jnp.concatenate(bws, axis=1)], axis=0)
        res = _dot(lhs, bdx)
        y = res[0:q_len] + y_state * e_in
        y_ref[rs, gsl] = y.astype(BF16)
        h_sc[g] = h_g * etot_row + res[q_len:]

    return group


def _ssd_kernel(*refs, n_sub, has_init, emit_state):
    xf_ref, xb_ref, dtf_ref, dtb_ref, dttf_ref, dttb_ref, alog_ref, alogt_ref = refs[:8]
    refs = refs[8:]
    if has_init:
        hf0_ref, hb0_ref = refs[:2]
        refs = refs[2:]
    yf_ref, yb_ref = refs[:2]
    refs = refs[2:]
    if emit_state:
        hf_out_ref, hb_out_ref = refs[:2]
        refs = refs[2:]
    hf_sc, hb_sc = refs

    @pl.when(pl.program_id(1) == 0)
    def _():
        if has_init:
            hf_sc[...] = hf0_ref[...]
            hb_sc[...] = hb0_ref[...]
        else:
            hf_sc[...] = jnp.zeros(hf_sc.shape, F32)
            hb_sc[...] = jnp.zeros(hb_sc.shape, F32)

    a_row = -jnp.exp(alog_ref[...]) * LOG2E
    a_col = -jnp.exp(alogt_ref[...]) * LOG2E
    for sub in range(n_sub):
        rs = slice(sub * SSM_CHUNK, (sub + 1) * SSM_CHUNK)
        fwd = _ssd_direction(xf_ref, rs, dtf_ref[rs, :], dttf_ref[:, rs], a_row, a_col, hf_sc, yf_ref, False, 0)
        for g in range(SSM_GROUPS):
            fwd(g)
    for sub in reversed(range(n_sub)):
        rs = slice(sub * SSM_CHUNK, (sub + 1) * SSM_CHUNK)
        bwd = _ssd_direction(xb_ref, rs, dtb_ref[rs, :], dttb_ref[:, rs], a_row, a_col, hb_sc, yb_ref, True,
                             SSM_HEADS)
        for g in range(SSM_GROUPS):
            bwd(g)

    if emit_state:
        hf_out_ref[...] = hf_sc[...]
        hb_out_ref[...] = hb_sc[...]


def _ssd_stream(xbc, dt, dtt, alog, alogt, init, emit_state):
    b, rows, cw = xbc.shape
    n_sub = min(SSD_SUB, rows // SSM_CHUNK)
    q_len = n_sub * SSM_CHUNK
    assert rows % q_len == 0
    n_c = rows // q_len
    d_inner = SSM_HEADS * SSM_HEAD_DIM
    nh2 = dt.shape[-1]
    state_shape = (SSM_GROUPS, D_STATE, SSM_REP * SSM_HEAD_DIM)
    asc = lambda bi, j: (bi, j, 0)
    desc = lambda bi, j: (bi, n_c - 1 - j, 0)
    state_spec = pl.BlockSpec((None,) + state_shape, lambda bi, j: (bi, 0, 0, 0))
    in_specs = [
        pl.BlockSpec((None, q_len, cw), asc),
        pl.BlockSpec((None, q_len, cw), desc),
        pl.BlockSpec((None, q_len, nh2), asc),
        pl.BlockSpec((None, q_len, nh2), desc),
        pl.BlockSpec((None, nh2, q_len), lambda bi, j: (bi, 0, j)),
        pl.BlockSpec((None, nh2, q_len), lambda bi, j: (bi, 0, n_c - 1 - j)),
        pl.BlockSpec(alog.shape, _CONST),
        pl.BlockSpec(alogt.shape, _CONST),
    ]
    args = [xbc, xbc, dt, dt, dtt, dtt, alog, alogt]
    if init is not None:
        in_specs += [state_spec, state_spec]
        args += list(init)
    out_specs = [pl.BlockSpec((None, q_len, d_inner), asc), pl.BlockSpec((None, q_len, d_inner), desc)]
    out_shape = [jax.ShapeDtypeStruct((b, rows, d_inner), BF16)] * 2
    if emit_state:
        out_specs += [state_spec, state_spec]
        out_shape += [jax.ShapeDtypeStruct((b,) + state_shape, F32)] * 2
    return pl.pallas_call(
        functools.partial(_ssd_kernel, n_sub=n_sub, has_init=init is not None, emit_state=emit_state),
        grid=(b, n_c),
        in_specs=in_specs,
        out_specs=out_specs,
        out_shape=out_shape,
        scratch_shapes=[pltpu.VMEM(state_shape, F32)] * 2,
        compiler_params=_params(("parallel", "arbitrary")),
        name="ssd_scan",
    )(*args)


def _ssd(xbc, dt, dtt, alog, alogt):
    yf_c, yb_c, hf, hb = _ssd_stream(xbc[1], dt[1], dtt[1], alog, alogt, None, True)
    yf_l, yb_l = _ssd_stream(xbc[0], dt[0], dtt[0], alog, alogt, (hf, hb), False)
    return (yf_l, yf_c), (yb_l, yb_c)


def _ssm_out_kernel(yf_ref, yb_ref, xs_ref, z_ref, dsk_ref, nw_ref, w_ref, x_ref, mod_ref, np_ref, o_ref):
    y = yf_ref[...].astype(F32) + yb_ref[...].astype(F32) + dsk_ref[...] * xs_ref[...].astype(F32)
    gated = y * z_ref[...].astype(F32)
    gsz = gated.shape[1] // SSM_GROUPS
    parts = []
    for g in range(SSM_GROUPS):
        t = gated[:, g * gsz:(g + 1) * gsz]
        parts.append(t * lax.rsqrt(jnp.mean(t * t, axis=-1, keepdims=True) + EPS))
    gn = (jnp.concatenate(parts, axis=1) * nw_ref[...]).astype(BF16)
    o_ref[...] = _residual(x_ref[...], _dot(gn, w_ref[...]), mod_ref[...], np_ref[...])


def _ssm_out(parts, yf, yb, xbc, z, dsk, nw, w, xs, modl, norm_post):
    b, d = modl.shape[0], modl.shape[-1]
    di = w.shape[0]

    def call(p):
        wide = pl.BlockSpec((None, p.tm, di), _ROW)
        return pl.pallas_call(
            _ssm_out_kernel,
            grid=(b, p.rows // p.tm),
            in_specs=[
                wide, wide, wide, wide,
                pl.BlockSpec((1, di), _CONST),
                pl.BlockSpec((1, di), _CONST),
                pl.BlockSpec(w.shape, _CONST),
                pl.BlockSpec((None, p.tm, d), _ROW),
                _mod_spec(p, d),
                pl.BlockSpec((1, d), _CONST),
            ],
            out_specs=pl.BlockSpec((None, p.tm, d), _ROW),
            out_shape=jax.ShapeDtypeStruct((b, p.rows, d), F32),
            compiler_params=_params(("parallel", "parallel")),
            name="ssm_out",
        )(yf[p.idx], yb[p.idx], xbc[p.idx], z[p.idx], dsk, nw, w, xs[p.idx], modl, norm_post)

    return tuple(call(p) for p in parts)


def _rope_tables(n_lat_tok, n_ctx_tok):
    t = np.arange(n_lat_tok)
    n_freq = HEAD_DIM // 4
    inv = 1.0 / (ROPE_THETA ** (jnp.arange(n_freq, dtype=F32) / n_freq))
    rowp = jnp.asarray(t // GRID_W, F32)
    colp = jnp.asarray(t % GRID_W, F32)
    ang = jnp.concatenate([rowp[:, None] * inv, colp[:, None] * inv], axis=-1)
    cos, sin = jnp.cos(ang), jnp.sin(ang)
    lat = (jnp.concatenate([cos, cos], axis=-1), jnp.concatenate([-sin, sin], axis=-1))
    cx = (jnp.ones((n_ctx_tok, HEAD_DIM), F32), jnp.zeros((n_ctx_tok, HEAD_DIM), F32))
    return lat, cx


_DEINT = np.concatenate([np.arange(0, HEAD_DIM, 2), np.arange(1, HEAD_DIM, 2)])


def _attn_weight_columns():
    qa, ka, va, ga, qb, kb, vb, gb = 0, 1024, 1280, 1536, 2560, 3584, 3840, 4096

    def heads(start, n, perm):
        base = start + HEAD_DIM * np.arange(n)[:, None]
        return (base + (_DEINT if perm else np.arange(HEAD_DIM))[None, :]).reshape(-1)

    return np.concatenate([
        heads(qa, 8, True), heads(ga, 8, False), heads(qb, 8, True), heads(gb, 8, False),
        heads(ka, 2, True), heads(va, 2, False), heads(kb, 2, True), heads(vb, 2, False)])


_ATTN_COLS_IDX = _attn_weight_columns()


def kernel(x, c, ctx, c_ctx, w_ada, b_ada, norm_pre, norm_post, attn_w_in, attn_w_out, attn_sink,
           attn_q_norm, attn_k_norm, ssm_w_in, ssm_conv_w, ssm_conv_b, ssm_dt_bias, ssm_a_log, ssm_d,
           ssm_norm, ssm_w_out):
    bsz, n_lat_tok, d = x.shape
    n_ctx_tok = ctx.shape[1]
    depth = w_ada.shape[0]
    assert n_lat_tok % TM_OUT == 0 and n_lat_tok % TM == 0 and n_ctx_tok % LANES == 0 and bsz <= 7
    d_inner = SSM_HEADS * SSM_HEAD_DIM
    bc_w = 2 * SSM_GROUPS * D_STATE

    xs = (x, ctx)
    in_parts = _parts(n_lat_tok, n_ctx_tok, TM)
    out_parts = _parts(n_lat_tok, n_ctx_tok, TM_OUT)
    cc = jnp.zeros((8, d), F32).at[:bsz].set(c).at[bsz].set(c_ctx)
    mod = _modulation(cc, w_ada, b_ada)
    mod = mod.reshape(depth, 8, 3, d)
    tables = _rope_tables(n_lat_tok, n_ctx_tok)

    for l in range(depth):
        l_out_parts = out_parts[:1] if l == depth - 1 else out_parts
        modl = jnp.stack([mod[l, :bsz], jnp.broadcast_to(mod[l, bsz], (bsz, 3, d))], axis=1)
        npre = norm_pre[l].reshape(1, d)
        npost = norm_post[l].reshape(1, d)
        i = l // 2
        if l % 2 == 0:
            w = attn_w_in[i][:, _ATTN_COLS_IDX].astype(BF16)
            qn = attn_q_norm[i][_DEINT].reshape(1, HEAD_DIM)
            kn = attn_k_norm[i][_DEINT].reshape(1, HEAD_DIM)
            qkv = _attn_in(in_parts, xs, modl, npre, w, tables, qn, kn)
            sink_b = jnp.broadcast_to(attn_sink[i][:, None], (A_Q_HEADS, HEAD_DIM))
            ya = _attn_a(qkv, sink_b)
            yb = _attn_b(qkv)
            wo = attn_w_out[i].astype(BF16)
            aq = A_Q_HEADS * HEAD_DIM
            xs = _attn_out(l_out_parts, ya, yb, wo[:aq], wo[aq:], xs, modl, npost)
        else:
            w = ssm_w_in[i]
            wz = w[:, :d_inner].astype(BF16)
            wx = w[:, d_inner:2 * d_inner + bc_w].astype(BF16)
            wdt = jnp.pad(w[:, 2 * d_inner + bc_w:], ((0, 0), (0, HPAD - 2 * SSM_HEADS))).astype(BF16)
            dtb = jnp.pad(ssm_dt_bias[i].reshape(1, -1), ((0, 0), (0, HPAD - 2 * SSM_HEADS)))
            z, xbc, dt, dtt = _ssm_in(in_parts, xs, modl, npre, wz, wx, wdt, wdt.T, ssm_conv_w[i],
                                      ssm_conv_b[i].reshape(1, -1), dtb, dtb.reshape(-1, 1))
            alog = jnp.pad(ssm_a_log[i].reshape(1, -1), ((0, 0), (0, HPAD - 2 * SSM_HEADS)))
            yf, ybk = _ssd(xbc, dt, dtt, alog, alog.reshape(-1, 1))
            dsk = jnp.repeat(ssm_d[i], SSM_HEAD_DIM).reshape(1, d_inner)
            xs = _ssm_out(l_out_parts, yf, ybk, xbc, z, dsk, ssm_norm[i].reshape(1, d_inner),
                          ssm_w_out[i].astype(BF16), xs, modl, npost)
    return xs[0]
```

```python
import functools
from typing import NamedTuple

import numpy as np
import jax
import jax.numpy as jnp
from jax import lax
from jax.experimental import pallas as pl
from jax.experimental.pallas import tpu as pltpu

F32 = jnp.float32
BF16 = jnp.bfloat16

EPS = 1e-6
GRID_W = 64
ROPE_THETA = 10000.0
HEAD_DIM = 128
A_Q_HEADS = 8
A_KV_HEADS = 2
B_Q_HEADS = 8
B_KV_HEADS = 2
REP = 4
WINDOW = 128
SSM_HEAD_DIM = 64
SSM_HEADS = 32
SSM_GROUPS = 8
SSM_REP = SSM_HEADS // SSM_GROUPS
D_STATE = 128
SSM_CHUNK = 128
SSD_SUB = 4
HPAD = 128

V7X_VMEM_BYTES = 64 * 1024 * 1024
VMEM_LIMIT = V7X_VMEM_BYTES - 8 * 1024 * 1024

TM = 256
TM_OUT = 512
CONV_COLS = 512
TQ_A = 128
NB_A = 8
TQ_B = 512
TK_B = 512
LANES = 128
LOG2E = 1.4426950408889634

QA0, GA0, QB0, GB0, KA0, VA0, KB0, VB0 = 0, 1024, 2048, 3072, 4096, 4352, 4608, 4864
ATTN_COLS = 5120

NEG = -1e30


def _params(sem, vmem=VMEM_LIMIT):
    return pltpu.CompilerParams(dimension_semantics=sem, vmem_limit_bytes=vmem)


def _silu(t):
    return t * (1.0 / (1.0 + jnp.exp2(t * (-LOG2E))))


def _rms(t, w):
    return t * lax.rsqrt(jnp.mean(t * t, axis=-1, keepdims=True) + EPS) * w


def _dot(a, b):
    return jnp.dot(a, b, preferred_element_type=F32)


def _dot_nt(a, b):
    return lax.dot_general(a, b, (((1,), (1,)), ((), ())), preferred_element_type=F32)


def _dot_hi(a, b):
    return jnp.dot(a, b, preferred_element_type=F32, precision=lax.Precision.HIGHEST)


class _Part(NamedTuple):
    idx: int
    rows: int
    tm: int


def _parts(n_lat_tok, n_ctx_tok, tm):
    return (_Part(0, n_lat_tok, tm), _Part(1, n_ctx_tok, n_ctx_tok))


_ROW = lambda bi, i: (bi, i, 0)
_CONST = lambda bi, i: (0, 0)


def _mod_spec(p, d):
    return pl.BlockSpec((None, None, 3, d), lambda bi, i: (bi, p.idx, 0, 0))


def _mod_kernel(cc_ref, w_ref, b_ref, o_ref):
    o_ref[...] = _dot_hi(_silu(cc_ref[...]), w_ref[...]) + b_ref[...]


def _modulation(cc, w_ada, b_ada):
    depth, d, d3 = w_ada.shape
    return pl.pallas_call(
        _mod_kernel,
        grid=(depth,),
        in_specs=[
            pl.BlockSpec((8, d), lambda l: (0, 0)),
            pl.BlockSpec((None, d, d3), lambda l: (l, 0, 0)),
            pl.BlockSpec((None, 1, d3), lambda l: (l, 0, 0)),
        ],
        out_specs=pl.BlockSpec((None, 8, d3), lambda l: (l, 0, 0)),
        out_shape=jax.ShapeDtypeStruct((depth, 8, d3), F32),
        compiler_params=_params(("arbitrary",)),
        name="modulation",
    )(cc, w_ada, b_ada.reshape(depth, 1, d3))


def _pre_norm(x, mod, w):
    return _rms(x, w) * (1.0 + mod[1:2]) + mod[0:1]


def _attn_in_kernel(x_ref, mod_ref, np_ref, w_ref, cos_ref, sin_ref, qn_ref, kn_ref, o_ref):
    hb = _pre_norm(x_ref[...], mod_ref[...], np_ref[...]).astype(BF16)
    cos = cos_ref[...]
    sin = sin_ref[...]
    scale = HEAD_DIM ** -0.5

    def rope(t):
        return t * cos + pltpu.roll(t, HEAD_DIM // 2, 1) * sin

    nblk = 512
    for j in range(ATTN_COLS // nblk):
        c0 = j * nblk
        t = _dot(hb, w_ref[:, c0:c0 + nblk])
        for hh in range(nblk // HEAD_DIM):
            col = c0 + hh * HEAD_DIM
            th = t[:, hh * HEAD_DIM:(hh + 1) * HEAD_DIM]
            if col < GA0:
                th = rope(th) * (scale * LOG2E)
            elif col < QB0 or GB0 <= col < KA0:
                th = _silu(th)
            elif col < GB0:
                th = rope(_rms(th, qn_ref[...])) * (scale * LOG2E)
            elif col < VA0:
                th = rope(th)
            elif KB0 <= col < VB0:
                th = rope(_rms(th, kn_ref[...]))
            o_ref[:, col:col + HEAD_DIM] = th.astype(BF16)


def _attn_in(parts, xs, modl, norm_pre, w, tables, qn, kn):
    b, d = modl.shape[0], modl.shape[-1]

    def call(p):
        table = pl.BlockSpec((p.tm, HEAD_DIM), lambda bi, i: (i, 0))
        return pl.pallas_call(
            _attn_in_kernel,
            grid=(b, p.rows // p.tm),
            in_specs=[
                pl.BlockSpec((None, p.tm, d), _ROW),
                _mod_spec(p, d),
                pl.BlockSpec((1, d), _CONST),
                pl.BlockSpec((d, ATTN_COLS), _CONST),
                table,
                table,
                pl.BlockSpec((1, HEAD_DIM), _CONST),
                pl.BlockSpec((1, HEAD_DIM), _CONST),
            ],
            out_specs=pl.BlockSpec((None, p.tm, ATTN_COLS), _ROW),
            out_shape=jax.ShapeDtypeStruct((b, p.rows, ATTN_COLS), BF16),
            compiler_params=_params(("parallel", "parallel")),
            name="attn_in",
        )(xs[p.idx], modl, norm_pre, w, *tables[p.idx], qn, kn)

    return tuple(call(p) for p in parts)


def _sink_column(sink_ref, g, n):
    return jnp.concatenate(
        [jnp.broadcast_to(sink_ref[g * REP + h:g * REP + h + 1, 0:1] * LOG2E, (n, 1)) for h in range(REP)], axis=0)


def _attn_a_kernel(q_ref, kp_ref, kc_ref, kn_ref, vp_ref, vc_ref, vn_ref, kx_ref, vx_ref,
                   g_ref, sink_ref, o_ref, *, n_lat):
    i = pl.program_id(1)
    tq = TQ_A
    rows = REP * tq
    r = lax.broadcasted_iota(jnp.int32, (rows, tq), 0) & (tq - 1)
    c = lax.broadcasted_iota(jnp.int32, (rows, tq), 1)
    for a in range(NB_A):
        blk = NB_A * i + a
        rq = slice(a * tq, (a + 1) * tq)
        m_prev = (c - r) >= jnp.where(blk > 0, 0, tq)
        m_next = (r - c) >= jnp.where(blk < n_lat - 1, 0, tq)
        for g in range(A_KV_HEADS):
            gs = slice(g * HEAD_DIM, (g + 1) * HEAD_DIM)
            if a == 0:
                k_p, v_p = kp_ref[:, gs], vp_ref[:, gs]
            else:
                k_p, v_p = kc_ref[(a - 1) * tq:a * tq, gs], vc_ref[(a - 1) * tq:a * tq, gs]
            if a == NB_A - 1:
                k_n, v_n = kn_ref[:, gs], vn_ref[:, gs]
            else:
                k_n, v_n = kc_ref[(a + 1) * tq:(a + 2) * tq, gs], vc_ref[(a + 1) * tq:(a + 2) * tq, gs]
            k_c, v_c = kc_ref[rq, gs], vc_ref[rq, gs]
            q = jnp.concatenate(
                [q_ref[rq, (g * REP + h) * HEAD_DIM:(g * REP + h + 1) * HEAD_DIM] for h in range(REP)], axis=0)
            sp = jnp.where(m_prev, _dot_nt(q, k_p), NEG)
            sc = _dot_nt(q, k_c)
            sn = jnp.where(m_next, _dot_nt(q, k_n), NEG)
            sx = _dot_nt(q, kx_ref[:, gs])
            sk = _sink_column(sink_ref, g, tq)
            sx0, sx1 = sx[:, 0:tq], sx[:, tq:2 * tq]
            m_t = jnp.maximum(jnp.maximum(jnp.maximum(sp, sc), jnp.maximum(sn, sx0)), sx1)
            m = jnp.maximum(jnp.max(m_t, axis=1, keepdims=True), sk)
            pp = jnp.exp2(sp - m)
            pc = jnp.exp2(sc - m)
            pn = jnp.exp2(sn - m)
            px = jnp.exp2(sx - m)
            den_t = (pp + pc) + (pn + px[:, 0:tq]) + px[:, tq:2 * tq]
            den = jnp.sum(den_t, axis=1, keepdims=True) + jnp.exp2(sk - m)
            o = (_dot(pp.astype(BF16), v_p) + _dot(pc.astype(BF16), v_c)
                 + _dot(pn.astype(BF16), v_n) + _dot(px.astype(BF16), vx_ref[:, gs])) * (1.0 / den)
            for h in range(REP):
                cs = slice((g * REP + h) * HEAD_DIM, (g * REP + h + 1) * HEAD_DIM)
                o_ref[rq, cs] = (o[h * tq:(h + 1) * tq] * g_ref[rq, cs].astype(F32)).astype(BF16)


def _attn_a_ctx_kernel(q_ref, kx_ref, vx_ref, g_ref, sink_ref, o_ref):
    n = q_ref.shape[0]
    for g in range(A_KV_HEADS):
        gs = slice(g * HEAD_DIM, (g + 1) * HEAD_DIM)
        q = jnp.concatenate(
            [q_ref[:, (g * REP + h) * HEAD_DIM:(g * REP + h + 1) * HEAD_DIM] for h in range(REP)], axis=0)
        sx = _dot_nt(q, kx_ref[:, gs])
        sk = _sink_column(sink_ref, g, n)
        m = jnp.maximum(jnp.max(sx, axis=1, keepdims=True), sk)
        px = jnp.exp2(sx - m)
        den = jnp.sum(px, axis=1, keepdims=True) + jnp.exp2(sk - m)
        o = _dot(px.astype(BF16), vx_ref[:, gs]) * (1.0 / den)
        for h in range(REP):
            cs = slice((g * REP + h) * HEAD_DIM, (g * REP + h + 1) * HEAD_DIM)
            o_ref[:, cs] = (o[h * n:(h + 1) * n] * g_ref[:, cs].astype(F32)).astype(BF16)


def _attn_a(qkv, sink_b):
    q_lat, q_ctx = qkv
    b, n_lat_tok, _ = q_lat.shape
    ctx_len = q_ctx.shape[1]
    tq = TQ_A
    n_lat = n_lat_tok // tq
    kvw = A_KV_HEADS * HEAD_DIM
    step = NB_A * tq
    assert ctx_len == 2 * tq and n_lat_tok % step == 0
    qw = A_Q_HEADS * HEAD_DIM
    prev = lambda cb: (lambda bi, i: (bi, jnp.maximum(NB_A * i - 1, 0), cb))
    cur = lambda cb: (lambda bi, i: (bi, i, cb))
    nxt = lambda cb: (lambda bi, i: (bi, jnp.minimum(NB_A * (i + 1), n_lat - 1), cb))
    whole = lambda cb: (lambda bi, i: (bi, 0, cb))
    kb, vb = KA0 // kvw, VA0 // kvw
    sink_spec = pl.BlockSpec((A_Q_HEADS, HEAD_DIM), _CONST)
    y_lat = pl.pallas_call(
        functools.partial(_attn_a_kernel, n_lat=n_lat),
        grid=(b, n_lat_tok // step),
        in_specs=[
            pl.BlockSpec((None, step, qw), cur(QA0 // qw)),
            pl.BlockSpec((None, tq, kvw), prev(kb)),
            pl.BlockSpec((None, step, kvw), cur(kb)),
            pl.BlockSpec((None, tq, kvw), nxt(kb)),
            pl.BlockSpec((None, tq, kvw), prev(vb)),
            pl.BlockSpec((None, step, kvw), cur(vb)),
            pl.BlockSpec((None, tq, kvw), nxt(vb)),
            pl.BlockSpec((None, ctx_len, kvw), whole(kb)),
            pl.BlockSpec((None, ctx_len, kvw), whole(vb)),
            pl.BlockSpec((None, step, qw), cur(GA0 // qw)),
            sink_spec,
        ],
        out_specs=pl.BlockSpec((None, step, qw), _ROW),
        out_shape=jax.ShapeDtypeStruct((b, n_lat_tok, qw), BF16),
        compiler_params=_params(("parallel", "parallel")),
        name="attn_window",
    )(q_lat, q_lat, q_lat, q_lat, q_lat, q_lat, q_lat, q_ctx, q_ctx, q_lat, sink_b)
    y_ctx = pl.pallas_call(
        _attn_a_ctx_kernel,
        grid=(b, 1),
        in_specs=[
            pl.BlockSpec((None, ctx_len, qw), whole(QA0 // qw)),
            pl.BlockSpec((None, ctx_len, kvw), whole(kb)),
            pl.BlockSpec((None, ctx_len, kvw), whole(vb)),
            pl.BlockSpec((None, ctx_len, qw), whole(GA0 // qw)),
            sink_spec,
        ],
        out_specs=pl.BlockSpec((None, ctx_len, qw), _ROW),
        out_shape=jax.ShapeDtypeStruct((b, ctx_len, qw), BF16),
        compiler_params=_params(("parallel", "parallel")),
        name="attn_window_ctx",
    )(q_ctx, q_ctx, q_ctx, q_ctx, sink_b)
    return y_lat, y_ctx


def _attn_b_kernel(q_ref, k_ref, v_ref, kx_ref, vx_ref, g_ref, o_ref, m_sc, l_sc, acc_sc, p_sc, alpha_sc):
    tq, tk = TQ_B, TK_B
    n_lat_tok = k_ref.shape[0]
    q = jnp.concatenate([q_ref[:, h * HEAD_DIM:(h + 1) * HEAD_DIM] for h in range(REP)], axis=0)
    m_sc[...] = jnp.full(m_sc.shape, NEG, F32)
    l_sc[...] = jnp.zeros(l_sc.shape, F32)
    acc_sc[...] = jnp.zeros(acc_sc.shape, F32)

    rb = 128

    def scores(k, slot):
        nk = k.shape[0]
        s = _dot_nt(q, k)
        for r0 in range(0, REP * tq, rb):
            rs = slice(r0, r0 + rb)
            s_b = s[rs]
            m_old = m_sc[rs]
            m_new = jnp.maximum(m_old, jnp.max(s_b, axis=1, keepdims=True))
            alpha = jnp.exp2(m_old - m_new)
            p = jnp.exp2(s_b - jnp.tile(m_new, (1, nk // LANES)))
            l_sc[rs] = alpha * l_sc[rs] + jnp.sum(p, axis=1, keepdims=True)
            m_sc[rs] = m_new
            alpha_sc[slot, rs] = alpha
            p_sc[slot, rs, 0:nk] = p.astype(BF16)

    def accumulate(slot, v):
        nk = v.shape[0]
        acc_sc[...] = alpha_sc[slot] * acc_sc[...] + _dot(p_sc[slot, :, 0:nk], v)

    def kv(ref, ci):
        if isinstance(ci, int):
            return ref[ci * tk:(ci + 1) * tk, :]
        return ref[pl.ds(pl.multiple_of(ci * tk, tk), tk), :]

    n_main = n_lat_tok // tk
    assert n_main % 2 == 0 and n_lat_tok % tk == 0
    scores(kv(k_ref, 0), 0)

    def pair(c1):
        scores(kv(k_ref, c1), 1)
        accumulate(0, kv(v_ref, c1 - 1))
        scores(kv(k_ref, c1 + 1), 0)
        accumulate(1, kv(v_ref, c1))

    def body(j, carry):
        pair(4 * j + 1)
        pair(4 * j + 3)
        return carry

    n_quads = (n_main - 2) // 4
    lax.fori_loop(0, n_quads, body, 0)
    for c1 in range(4 * n_quads + 1, n_main - 1, 2):
        pair(c1)
    scores(kv(k_ref, n_main - 1), 1)
    accumulate(0, kv(v_ref, n_main - 2))
    scores(kx_ref[...], 0)
    accumulate(1, kv(v_ref, n_main - 1))
    accumulate(0, vx_ref[...])

    o = acc_sc[...] * (1.0 / l_sc[...])
    for h in range(REP):
        cs = slice(h * HEAD_DIM, (h + 1) * HEAD_DIM)
        o_ref[:, cs] = (o[h * tq:(h + 1) * tq] * g_ref[:, cs].astype(F32)).astype(BF16)


def _attn_b_ctx_kernel(q_ref, k_ref, v_ref, g_ref, o_ref):
    n = q_ref.shape[0]
    q = jnp.concatenate([q_ref[:, h * HEAD_DIM:(h + 1) * HEAD_DIM] for h in range(REP)], axis=0)
    s = _dot_nt(q, k_ref[...])
    p = jnp.exp2(s - jnp.max(s, axis=1, keepdims=True))
    o = _dot(p.astype(BF16), v_ref[...]) * (1.0 / jnp.sum(p, axis=1, keepdims=True))
    for h in range(REP):
        cs = slice(h * HEAD_DIM, (h + 1) * HEAD_DIM)
        o_ref[:, cs] = (o[h * n:(h + 1) * n] * g_ref[:, cs].astype(F32)).astype(BF16)


def _attn_b(qkv):
    q_lat, q_ctx = qkv
    b, n_lat_tok, _ = q_lat.shape
    n_ctx_tok = q_ctx.shape[1]
    tq = TQ_B
    gw = REP * HEAD_DIM
    assert n_lat_tok % tq == 0 and n_ctx_tok <= TK_B and n_ctx_tok % LANES == 0
    kcol, vcol = KB0 // HEAD_DIM, VB0 // HEAD_DIM
    y_lat = pl.pallas_call(
        _attn_b_kernel,
        grid=(b, B_KV_HEADS, n_lat_tok // tq),
        in_specs=[
            pl.BlockSpec((None, tq, gw), lambda bi, g, i: (bi, i, QB0 // gw + g)),
            pl.BlockSpec((None, n_lat_tok, HEAD_DIM), lambda bi, g, i: (bi, 0, kcol + g)),
            pl.BlockSpec((None, n_lat_tok, HEAD_DIM), lambda bi, g, i: (bi, 0, vcol + g)),
            pl.BlockSpec((None, n_ctx_tok, HEAD_DIM), lambda bi, g, i: (bi, 0, kcol + g)),
            pl.BlockSpec((None, n_ctx_tok, HEAD_DIM), lambda bi, g, i: (bi, 0, vcol + g)),
            pl.BlockSpec((None, tq, gw), lambda bi, g, i: (bi, i, GB0 // gw + g)),
        ],
        out_specs=pl.BlockSpec((None, tq, gw), lambda bi, g, i: (bi, i, g)),
        out_shape=jax.ShapeDtypeStruct((b, n_lat_tok, B_Q_HEADS * HEAD_DIM), BF16),
        scratch_shapes=[
            pltpu.VMEM((REP * tq, LANES), F32),
            pltpu.VMEM((REP * tq, LANES), F32),
            pltpu.VMEM((REP * tq, HEAD_DIM), F32),
            pltpu.VMEM((2, REP * tq, TK_B), BF16),
            pltpu.VMEM((2, REP * tq, LANES), F32),
        ],
        compiler_params=_params(("parallel", "parallel", "parallel")),
        name="attn_dense",
    )(q_lat, q_lat, q_lat, q_ctx, q_ctx, q_lat)
    y_ctx = pl.pallas_call(
        _attn_b_ctx_kernel,
        grid=(b, B_KV_HEADS),
        in_specs=[
            pl.BlockSpec((None, n_ctx_tok, gw), lambda bi, g: (bi, 0, QB0 // gw + g)),
            pl.BlockSpec((None, n_ctx_tok, HEAD_DIM), lambda bi, g: (bi, 0, kcol + g)),
            pl.BlockSpec((None, n_ctx_tok, HEAD_DIM), lambda bi, g: (bi, 0, vcol + g)),
            pl.BlockSpec((None, n_ctx_tok, gw), lambda bi, g: (bi, 0, GB0 // gw + g)),
        ],
        out_specs=pl.BlockSpec((None, n_ctx_tok, gw), lambda bi, g: (bi, 0, g)),
        out_shape=jax.ShapeDtypeStruct((b, n_ctx_tok, B_Q_HEADS * HEAD_DIM), BF16),
        compiler_params=_params(("parallel", "parallel")),
        name="attn_dense_ctx",
    )(q_ctx, q_ctx, q_ctx, q_ctx)
    return y_lat, y_ctx


def _residual(x, y, mod, w_post):
    return x + mod[2:3] * _rms(y, w_post)


def _attn_out_kernel(ya_ref, yb_ref, wa_ref, wb_ref, x_ref, mod_ref, np_ref, o_ref):
    y = _dot(ya_ref[...], wa_ref[...]) + _dot(yb_ref[...], wb_ref[...])
    o_ref[...] = _residual(x_ref[...], y, mod_ref[...], np_ref[...])


def _attn_out(parts, ya, yb, wa, wb, xs, modl, norm_post):
    b, d = modl.shape[0], modl.shape[-1]

    def call(p):
        return pl.pallas_call(
            _attn_out_kernel,
            grid=(b, p.rows // p.tm),
            in_specs=[
                pl.BlockSpec((None, p.tm, wa.shape[0]), _ROW),
                pl.BlockSpec((None, p.tm, wb.shape[0]), _ROW),
                pl.BlockSpec(wa.shape, _CONST),
                pl.BlockSpec(wb.shape, _CONST),
                pl.BlockSpec((None, p.tm, d), _ROW),
                _mod_spec(p, d),
                pl.BlockSpec((1, d), _CONST),
            ],
            out_specs=pl.BlockSpec((None, p.tm, d), _ROW),
            out_shape=jax.ShapeDtypeStruct((b, p.rows, d), F32),
            compiler_params=_params(("parallel", "parallel")),
            name="attn_out",
        )(ya[p.idx], yb[p.idx], wa, wb, xs[p.idx], modl, norm_post)

    return tuple(call(p) for p in parts)


def _softplus(t):
    return jnp.maximum(t, 0.0) + jnp.log(1.0 + jnp.exp(-jnp.abs(t)))


def _ssm_in_kernel(x_ref, xp_ref, xn_ref, mod_ref, np_ref, wz_ref, wx_ref, wdt_ref, wdtt_ref,
                   cw_ref, cb_ref, dtb_ref, dtbt_ref, z_ref, xbc_ref, dt_ref, dtt_ref):
    i = pl.program_id(1)
    tm = x_ref.shape[0]
    mod = mod_ref[...]
    w_pre = np_ref[...]
    h = _pre_norm(x_ref[...], mod, w_pre)
    hb = h.astype(BF16)
    has_prev = (i > 0).astype(F32)
    has_next = (i < pl.num_programs(1) - 1).astype(F32)
    hp = _pre_norm(xp_ref[...], mod, w_pre) * has_prev
    hn = _pre_norm(xn_ref[...], mod, w_pre) * has_next
    ext = jnp.concatenate([hp, h, hn], axis=0).astype(BF16)

    z_ref[...] = _silu(_dot(hb, wz_ref[...])).astype(BF16)
    dt_ref[...] = _softplus(_dot(hb, wdt_ref[...]) + dtb_ref[...])
    dtt_ref[...] = _softplus(_dot_nt(wdtt_ref[...], hb) + dtbt_ref[...])

    nblk = CONV_COLS
    nt = tm // 8
    sub = lax.broadcasted_iota(jnp.int32, (8, nblk), 0)
    for j in range(wx_ref.shape[1] // nblk):
        cs = slice(j * nblk, (j + 1) * nblk)
        u3 = _dot(ext, wx_ref[:, cs]).reshape(nt + 2, 8, nblk)
        dn = pltpu.roll(u3, 1, 1)
        upw = pltpu.roll(u3, 7, 1)
        u_prev = jnp.where(sub == 0, dn[0:nt], dn[1:nt + 1])
        u_next = jnp.where(sub == 7, upw[2:nt + 2], upw[1:nt + 1])
        conv = (cb_ref[:, cs] + cw_ref[0:1, cs] * u_prev + cw_ref[1:2, cs] * u3[1:nt + 1]
                + cw_ref[2:3, cs] * u_next)
        xbc_ref[:, cs] = _silu(conv).reshape(tm, nblk).astype(BF16)


def _ssm_in(parts, xs, modl, norm_pre, wz, wx, wdt, wdtt, cw, cb, dtb, dtbt):
    b, d = modl.shape[0], modl.shape[-1]
    nh2 = wdt.shape[1]

    def call(p):
        r8 = p.tm // 8
        last8 = p.rows // 8 - 1
        src = xs[p.idx]
        return pl.pallas_call(
            _ssm_in_kernel,
            grid=(b, p.rows // p.tm),
            in_specs=[
                pl.BlockSpec((None, p.tm, d), _ROW),
                pl.BlockSpec((None, 8, d), lambda bi, i: (bi, jnp.maximum(i * r8 - 1, 0), 0)),
                pl.BlockSpec((None, 8, d), lambda bi, i: (bi, jnp.minimum((i + 1) * r8, last8), 0)),
                _mod_spec(p, d),
                pl.BlockSpec((1, d), _CONST),
                pl.BlockSpec(wz.shape, _CONST),
                pl.BlockSpec(wx.shape, _CONST),
                pl.BlockSpec(wdt.shape, _CONST),
                pl.BlockSpec(wdtt.shape, _CONST),
                pl.BlockSpec(cw.shape, _CONST),
                pl.BlockSpec(cb.shape, _CONST),
                pl.BlockSpec(dtb.shape, _CONST),
                pl.BlockSpec(dtbt.shape, _CONST),
            ],
            out_specs=[
                pl.BlockSpec((None, p.tm, wz.shape[1]), _ROW),
                pl.BlockSpec((None, p.tm, wx.shape[1]), _ROW),
                pl.BlockSpec((None, p.tm, nh2), _ROW),
                pl.BlockSpec((None, nh2, p.tm), lambda bi, i: (bi, 0, i)),
            ],
            out_shape=[
                jax.ShapeDtypeStruct((b, p.rows, wz.shape[1]), BF16),
                jax.ShapeDtypeStruct((b, p.rows, wx.shape[1]), BF16),
                jax.ShapeDtypeStruct((b, p.rows, nh2), F32),
                jax.ShapeDtypeStruct((b, nh2, p.rows), F32),
            ],
            compiler_params=_params(("parallel", "parallel")),
            name="ssm_in",
        )(src, src, src, modl, norm_pre, wz, wx, wdt, wdtt, cw, cb, dtb, dtbt)

    lat, cx = (call(p) for p in parts)
    return tuple(zip(lat, cx))


def _split3(t):
    hi = t.astype(BF16)
    r1 = t - hi.astype(F32)
    mid = r1.astype(BF16)
    lo = (r1 - mid.astype(F32)).astype(BF16)
    return hi, mid, lo


def _ones_dot_lhs(tri01, a):
    return _dot(jnp.concatenate([tri01] * 3, axis=1), jnp.concatenate(_split3(a), axis=0))


def _ones_dot_rhs(at, tri01):
    return _dot(jnp.concatenate(_split3(at), axis=1), jnp.concatenate([tri01] * 3, axis=0))


def _ssd_direction(xbc_ref, rs, dt, dtt, a_row, a_col, h_sc, y_ref, reverse, hoff):
    q_len = SSM_CHUNK
    d_inner = SSM_HEADS * SSM_HEAD_DIM
    gw = SSM_REP * SSM_HEAD_DIM
    row = lax.broadcasted_iota(jnp.int32, (q_len, q_len), 0)
    col = lax.broadcasted_iota(jnp.int32, (q_len, q_len), 1)
    lower = row >= col
    upper = row <= col
    mask = upper if reverse else lower
    tri = jnp.where(mask, 1.0, 0.0).astype(BF16)
    tri_t = jnp.where(lower if reverse else upper, 1.0, 0.0).astype(BF16)

    a = dt * a_row
    at = dtt * a_col
    acum = _ones_dot_lhs(tri, a)
    acum_t = _ones_dot_rhs(at, tri_t)
    total_t = jnp.sum(at, axis=1, keepdims=True)
    acum_t = acum_t - jnp.log2(dtt)
    w_t = jnp.exp2(total_t - acum_t)
    etot = jnp.exp2(jnp.sum(a, axis=0, keepdims=True))
    head_of_col = jnp.right_shift(lax.broadcasted_iota(jnp.int32, (q_len, gw), 1),
                                  SSM_HEAD_DIM.bit_length() - 1)

    def group(g):
        b_g = xbc_ref[rs, d_inner + g * D_STATE:d_inner + (g + 1) * D_STATE]
        c_g = xbc_ref[rs, d_inner + (SSM_GROUPS + g) * D_STATE:d_inner + (SSM_GROUPS + g + 1) * D_STATE]
        gsl = slice(g * gw, (g + 1) * gw)
        x_g = xbc_ref[rs, gsl]
        cb = _dot_nt(c_g, b_g)
        b_t = b_g.astype(F32).T
        h_g = h_sc[g]
        y_state = _dot(c_g, h_g.astype(BF16))
        ms, bws, bdx = [], [], []
        etot_row = None
        e_in = None
        for r in range(SSM_REP):
            h = hoff + g * SSM_REP + r
            acb = jnp.broadcast_to(acum[:, h:h + 1], (q_len, q_len))
            decay = jnp.exp2(jnp.where(mask, acb - acum_t[h:h + 1, :], NEG))
            ms.append((cb * decay).astype(BF16))
            bws.append((b_t * w_t[h:h + 1, :]).astype(BF16))
            sel = head_of_col == r
            bdx.append(jnp.where(sel, x_g, jnp.zeros_like(x_g)))
            e_q = jnp.exp2(acb)
            e_q = jnp.concatenate([e_q] * (gw // q_len), axis=1)
            e_in = e_q if e_in is None else jnp.where(sel, e_q, e_in)
            e_r = jnp.broadcast_to(etot[:, h:h + 1], (1, gw))
            etot_row = e_r if etot_row is None else jnp.where(head_of_col[0:1] == r, e_r, etot_row)
        bdx = jnp.concatenate(bdx, axis=0)
        lhs = jnp.concatenate([jnp.concatenate(ms, axis=1), jnp.concatenate(bws, axis=1)], axis=0)
        res = _dot(lhs, bdx)
        y = res[0:q_len] + y_state * e_in
        y_ref[rs, gsl] = y.astype(BF16)
        h_sc[g] = h_g * etot_row + res[q_len:]

    return group


def _ssd_kernel(*refs, n_sub, has_init, emit_state):
    xf_ref, xb_ref, dtf_ref, dtb_ref, dttf_ref, dttb_ref, alog_ref, alogt_ref = refs[:8]
    refs = refs[8:]
    if has_init:
        hf0_ref, hb0_ref = refs[:2]
        refs = refs[2:]
    yf_ref, yb_ref = refs[:2]
    refs = refs[2:]
    if emit_state:
        hf_out_ref, hb_out_ref = refs[:2]
        refs = refs[2:]
    hf_sc, hb_sc = refs

    @pl.when(pl.program_id(1) == 0)
    def _():
        if has_init:
            hf_sc[...] = hf0_ref[...]
            hb_sc[...] = hb0_ref[...]
        else:
            hf_sc[...] = jnp.zeros(hf_sc.shape, F32)
            hb_sc[...] = jnp.zeros(hb_sc.shape, F32)

    a_row = -jnp.exp(alog_ref[...]) * LOG2E
    a_col = -jnp.exp(alogt_ref[...]) * LOG2E
    for sub in range(n_sub):
        rs = slice(sub * SSM_CHUNK, (sub + 1) * SSM_CHUNK)
        fwd = _ssd_direction(xf_ref, rs, dtf_ref[rs, :], dttf_ref[:, rs], a_row, a_col, hf_sc, yf_ref, False, 0)
        for g in range(SSM_GROUPS):
            fwd(g)
    for sub in reversed(range(n_sub)):
        rs = slice(sub * SSM_CHUNK, (sub + 1) * SSM_CHUNK)
        bwd = _ssd_direction(xb_ref, rs, dtb_ref[rs, :], dttb_ref[:, rs], a_row, a_col, hb_sc, yb_ref, True,
                             SSM_HEADS)
        for g in range(SSM_GROUPS):
            bwd(g)

    if emit_state:
        hf_out_ref[...] = hf_sc[...]
        hb_out_ref[...] = hb_sc[...]


def _ssd_stream(xbc, dt, dtt, alog, alogt, init, emit_state):
    b, rows, cw = xbc.shape
    n_sub = min(SSD_SUB, rows // SSM_CHUNK)
    q_len = n_sub * SSM_CHUNK
    assert rows % q_len == 0
    n_c = rows // q_len
    d_inner = SSM_HEADS * SSM_HEAD_DIM
    nh2 = dt.shape[-1]
    state_shape = (SSM_GROUPS, D_STATE, SSM_REP * SSM_HEAD_DIM)
    asc = lambda bi, j: (bi, j, 0)
    desc = lambda bi, j: (bi, n_c - 1 - j, 0)
    state_spec = pl.BlockSpec((None,) + state_shape, lambda bi, j: (bi, 0, 0, 0))
    in_specs = [
        pl.BlockSpec((None, q_len, cw), asc),
        pl.BlockSpec((None, q_len, cw), desc),
        pl.BlockSpec((None, q_len, nh2), asc),
        pl.BlockSpec((None, q_len, nh2), desc),
        pl.BlockSpec((None, nh2, q_len), lambda bi, j: (bi, 0, j)),
        pl.BlockSpec((None, nh2, q_len), lambda bi, j: (bi, 0, n_c - 1 - j)),
        pl.BlockSpec(alog.shape, _CONST),
        pl.BlockSpec(alogt.shape, _CONST),
    ]
    args = [xbc, xbc, dt, dt, dtt, dtt, alog, alogt]
    if init is not None:
        in_specs += [state_spec, state_spec]
        args += list(init)
    out_specs = [pl.BlockSpec((None, q_len, d_inner), asc), pl.BlockSpec((None, q_len, d_inner), desc)]
    out_shape = [jax.ShapeDtypeStruct((b, rows, d_inner), BF16)] * 2
    if emit_state:
        out_specs += [state_spec, state_spec]
        out_shape += [jax.ShapeDtypeStruct((b,) + state_shape, F32)] * 2
    return pl.pallas_call(
        functools.partial(_ssd_kernel, n_sub=n_sub, has_init=init is not None, emit_state=emit_state),
        grid=(b, n_c),
        in_specs=in_specs,
        out_specs=out_specs,
        out_shape=out_shape,
        scratch_shapes=[pltpu.VMEM(state_shape, F32)] * 2,
        compiler_params=_params(("parallel", "arbitrary")),
        name="ssd_scan",
    )(*args)


def _ssd(xbc, dt, dtt, alog, alogt):
    yf_c, yb_c, hf, hb = _ssd_stream(xbc[1], dt[1], dtt[1], alog, alogt, None, True)
    yf_l, yb_l = _ssd_stream(xbc[0], dt[0], dtt[0], alog, alogt, (hf, hb), False)
    return (yf_l, yf_c), (yb_l, yb_c)


def _ssm_out_kernel(yf_ref, yb_ref, xs_ref, z_ref, dsk_ref, nw_ref, w_ref, x_ref, mod_ref, np_ref, o_ref):
    y = yf_ref[...].astype(F32) + yb_ref[...].astype(F32) + dsk_ref[...] * xs_ref[...].astype(F32)
    gated = y * z_ref[...].astype(F32)
    gsz = gated.shape[1] // SSM_GROUPS
    parts = []
    for g in range(SSM_GROUPS):
        t = gated[:, g * gsz:(g + 1) * gsz]
        parts.append(t * lax.rsqrt(jnp.mean(t * t, axis=-1, keepdims=True) + EPS))
    gn = (jnp.concatenate(parts, axis=1) * nw_ref[...]).astype(BF16)
    o_ref[...] = _residual(x_ref[...], _dot(gn, w_ref[...]), mod_ref[...], np_ref[...])


def _ssm_out(parts, yf, yb, xbc, z, dsk, nw, w, xs, modl, norm_post):
    b, d = modl.shape[0], modl.shape[-1]
    di = w.shape[0]

    def call(p):
        wide = pl.BlockSpec((None, p.tm, di), _ROW)
        return pl.pallas_call(
            _ssm_out_kernel,
            grid=(b, p.rows // p.tm),
            in_specs=[
                wide, wide, wide, wide,
                pl.BlockSpec((1, di), _CONST),
                pl.BlockSpec((1, di), _CONST),
                pl.BlockSpec(w.shape, _CONST),
                pl.BlockSpec((None, p.tm, d), _ROW),
                _mod_spec(p, d),
                pl.BlockSpec((1, d), _CONST),
            ],
            out_specs=pl.BlockSpec((None, p.tm, d), _ROW),
            out_shape=jax.ShapeDtypeStruct((b, p.rows, d), F32),
            compiler_params=_params(("parallel", "parallel")),
            name="ssm_out",
        )(yf[p.idx], yb[p.idx], xbc[p.idx], z[p.idx], dsk, nw, w, xs[p.idx], modl, norm_post)

    return tuple(call(p) for p in parts)


def _rope_tables(n_lat_tok, n_ctx_tok):
    t = np.arange(n_lat_tok)
    n_freq = HEAD_DIM // 4
    inv = 1.0 / (ROPE_THETA ** (jnp.arange(n_freq, dtype=F32) / n_freq))
    rowp = jnp.asarray(t // GRID_W, F32)
    colp = jnp.asarray(t % GRID_W, F32)
    ang = jnp.concatenate([rowp[:, None] * inv, colp[:, None] * inv], axis=-1)
    cos, sin = jnp.cos(ang), jnp.sin(ang)
    lat = (jnp.concatenate([cos, cos], axis=-1), jnp.concatenate([-sin, sin], axis=-1))
    cx = (jnp.ones((n_ctx_tok, HEAD_DIM), F32), jnp.zeros((n_ctx_tok, HEAD_DIM), F32))
    return lat, cx


_DEINT = np.concatenate([np.arange(0, HEAD_DIM, 2), np.arange(1, HEAD_DIM, 2)])


def _attn_weight(w):
    d = w.shape[0]
    w = w.astype(BF16)

    def deint(seg):
        n = seg.shape[1] // HEAD_DIM
        return seg.reshape(d, n, HEAD_DIM // 2, 2).transpose(0, 1, 3, 2).reshape(d, n * HEAD_DIM)

    qa, ka, va, ga, qb, kb, vb, gb = jnp.split(w, [1024, 1280, 1536, 2560, 3584, 3840, 4096], axis=1)
    return jnp.concatenate([deint(qa), ga, deint(qb), gb, deint(ka), va, deint(kb), vb], axis=1)


def kernel(x, c, ctx, c_ctx, w_ada, b_ada, norm_pre, norm_post, attn_w_in, attn_w_out, attn_sink,
           attn_q_norm, attn_k_norm, ssm_w_in, ssm_conv_w, ssm_conv_b, ssm_dt_bias, ssm_a_log, ssm_d,
           ssm_norm, ssm_w_out):
    bsz, n_lat_tok, d = x.shape
    n_ctx_tok = ctx.shape[1]
    depth = w_ada.shape[0]
    assert n_lat_tok % TM_OUT == 0 and n_lat_tok % TM == 0 and n_ctx_tok % LANES == 0 and bsz <= 7
    d_inner = SSM_HEADS * SSM_HEAD_DIM
    bc_w = 2 * SSM_GROUPS * D_STATE

    xs = (x, ctx)
    in_parts = _parts(n_lat_tok, n_ctx_tok, TM)
    out_parts = _parts(n_lat_tok, n_ctx_tok, TM_OUT)
    cc = jnp.zeros((8, d), F32).at[:bsz].set(c).at[bsz].set(c_ctx)
    mod = _modulation(cc, w_ada, b_ada)
    mod = mod.reshape(depth, 8, 3, d)
    tables = _rope_tables(n_lat_tok, n_ctx_tok)

    for l in range(depth):
        l_out_parts = out_parts[:1] if l == depth - 1 else out_parts
        modl = jnp.stack([mod[l, :bsz], jnp.broadcast_to(mod[l, bsz], (bsz, 3, d))], axis=1)
        npre = norm_pre[l].reshape(1, d)
        npost = norm_post[l].reshape(1, d)
        i = l // 2
        if l % 2 == 0:
            w = _attn_weight(attn_w_in[i])
            qn = attn_q_norm[i][_DEINT].reshape(1, HEAD_DIM)
            kn = attn_k_norm[i][_DEINT].reshape(1, HEAD_DIM)
            qkv = _attn_in(in_parts, xs, modl, npre, w, tables, qn, kn)
            sink_b = jnp.broadcast_to(attn_sink[i][:, None], (A_Q_HEADS, HEAD_DIM))
            ya = _attn_a(qkv, sink_b)
            yb = _attn_b(qkv)
            wo = attn_w_out[i].astype(BF16)
            aq = A_Q_HEADS * HEAD_DIM
            xs = _attn_out(l_out_parts, ya, yb, wo[:aq], wo[aq:], xs, modl, npost)
        else:
            w = ssm_w_in[i]
            wz = w[:, :d_inner].astype(BF16)
            wx = w[:, d_inner:2 * d_inner + bc_w].astype(BF16)
            wdt = jnp.pad(w[:, 2 * d_inner + bc_w:], ((0, 0), (0, HPAD - 2 * SSM_HEADS))).astype(BF16)
            dtb = jnp.pad(ssm_dt_bias[i].reshape(1, -1), ((0, 0), (0, HPAD - 2 * SSM_HEADS)))
            z, xbc, dt, dtt = _ssm_in(in_parts, xs, modl, npre, wz, wx, wdt, wdt.T, ssm_conv_w[i],
                                      ssm_conv_b[i].reshape(1, -1), dtb, dtb.reshape(-1, 1))
            alog = jnp.pad(ssm_a_log[i].reshape(1, -1), ((0, 0), (0, HPAD - 2 * SSM_HEADS)))
            yf, ybk = _ssd(xbc, dt, dtt, alog, alog.reshape(-1, 1))
            dsk = jnp.repeat(ssm_d[i], SSM_HEAD_DIM).reshape(1, d_inner)
            xs = _ssm_out(l_out_parts, yf, ybk, xbc, z, dsk, ssm_norm[i].reshape(1, d_inner),
                          ssm_w_out[i].astype(BF16), xs, modl, npost)
    return xs[0]
```

```python
import functools
from typing import NamedTuple

import numpy as np
import jax
import jax.numpy as jnp
from jax import lax
from jax.experimental import pallas as pl
from jax.experimental.pallas import tpu as pltpu

F32 = jnp.float32
BF16 = jnp.bfloat16

EPS = 1e-6
GRID_W = 64
ROPE_THETA = 10000.0
HEAD_DIM = 128
A_Q_HEADS = 8
A_KV_HEADS = 2
B_Q_HEADS = 8
B_KV_HEADS = 2
REP = 4
WINDOW = 128
SSM_HEAD_DIM = 64
SSM_HEADS = 32
SSM_GROUPS = 8
SSM_REP = SSM_HEADS // SSM_GROUPS
D_STATE = 128
SSM_CHUNK = 128
SSD_SUB = 4
HPAD = 128

V7X_VMEM_BYTES = 64 * 1024 * 1024
VMEM_LIMIT = V7X_VMEM_BYTES - 8 * 1024 * 1024

TM = 256
TM_OUT = 512
CONV_COLS = 512
TQ_A = 128
NB_A = 8
TQ_B = 512
TK_B = 512
LANES = 128
LOG2E = 1.4426950408889634

QA0, GA0, QB0, GB0, KA0, VA0, KB0, VB0 = 0, 1024, 2048, 3072, 4096, 4352, 4608, 4864
ATTN_COLS = 5120

NEG = -1e30


def _params(sem, vmem=VMEM_LIMIT):
    return pltpu.CompilerParams(dimension_semantics=sem, vmem_limit_bytes=vmem)


def _silu(t):
    return t * (1.0 / (1.0 + jnp.exp2(t * (-LOG2E))))


def _rms(t, w):
    return t * lax.rsqrt(jnp.mean(t * t, axis=-1, keepdims=True) + EPS) * w


def _dot(a, b):
    return jnp.dot(a, b, preferred_element_type=F32)


def _dot_nt(a, b):
    return lax.dot_general(a, b, (((1,), (1,)), ((), ())), preferred_element_type=F32)


def _dot_hi(a, b):
    return jnp.dot(a, b, preferred_element_type=F32, precision=lax.Precision.HIGHEST)


class _Part(NamedTuple):
    idx: int
    rows: int
    tm: int


def _parts(n_lat_tok, n_ctx_tok, tm):
    return (_Part(0, n_lat_tok, tm), _Part(1, n_ctx_tok, n_ctx_tok))


_ROW = lambda bi, i: (bi, i, 0)
_CONST = lambda bi, i: (0, 0)


def _mod_spec(p, d):
    return pl.BlockSpec((None, None, 3, d), lambda bi, i: (bi, p.idx, 0, 0))


def _mod_kernel(cc_ref, w_ref, b_ref, o_ref):
    o_ref[...] = _dot_hi(_silu(cc_ref[...]), w_ref[...]) + b_ref[...]


def _modulation(cc, w_ada, b_ada):
    depth, d, d3 = w_ada.shape
    return pl.pallas_call(
        _mod_kernel,
        grid=(depth,),
        in_specs=[
            pl.BlockSpec((8, d), lambda l: (0, 0)),
            pl.BlockSpec((None, d, d3), lambda l: (l, 0, 0)),
            pl.BlockSpec((None, 1, d3), lambda l: (l, 0, 0)),
        ],
        out_specs=pl.BlockSpec((None, 8, d3), lambda l: (l, 0, 0)),
        out_shape=jax.ShapeDtypeStruct((depth, 8, d3), F32),
        compiler_params=_params(("arbitrary",)),
        name="modulation",
    )(cc, w_ada, b_ada.reshape(depth, 1, d3))


def _pre_norm(x, mod, w):
    return _rms(x, w) * (1.0 + mod[1:2]) + mod[0:1]


def _attn_in_kernel(x_ref, mod_ref, np_ref, w_ref, cos_ref, sin_ref, qn_ref, kn_ref, o_ref):
    hb = _pre_norm(x_ref[...], mod_ref[...], np_ref[...]).astype(BF16)
    cos = cos_ref[...]
    sin = sin_ref[...]
    scale = HEAD_DIM ** -0.5

    def rope(t):
        return t * cos + pltpu.roll(t, HEAD_DIM // 2, 1) * sin

    nblk = 512
    for j in range(ATTN_COLS // nblk):
        c0 = j * nblk
        t = _dot(hb, w_ref[:, c0:c0 + nblk])
        for hh in range(nblk // HEAD_DIM):
            col = c0 + hh * HEAD_DIM
            th = t[:, hh * HEAD_DIM:(hh + 1) * HEAD_DIM]
            if col < GA0:
                th = rope(th) * (scale * LOG2E)
            elif col < QB0 or GB0 <= col < KA0:
                th = _silu(th)
            elif col < GB0:
                th = rope(_rms(th, qn_ref[...])) * (scale * LOG2E)
            elif col < VA0:
                th = rope(th)
            elif KB0 <= col < VB0:
                th = rope(_rms(th, kn_ref[...]))
            o_ref[:, col:col + HEAD_DIM] = th.astype(BF16)


def _attn_in(parts, xs, modl, norm_pre, w, tables, qn, kn):
    b, d = modl.shape[0], modl.shape[-1]

    def call(p):
        table = pl.BlockSpec((p.tm, HEAD_DIM), lambda bi, i: (i, 0))
        return pl.pallas_call(
            _attn_in_kernel,
            grid=(b, p.rows // p.tm),
            in_specs=[
                pl.BlockSpec((None, p.tm, d), _ROW),
                _mod_spec(p, d),
                pl.BlockSpec((1, d), _CONST),
                pl.BlockSpec((d, ATTN_COLS), _CONST),
                table,
                table,
                pl.BlockSpec((1, HEAD_DIM), _CONST),
                pl.BlockSpec((1, HEAD_DIM), _CONST),
            ],
            out_specs=pl.BlockSpec((None, p.tm, ATTN_COLS), _ROW),
            out_shape=jax.ShapeDtypeStruct((b, p.rows, ATTN_COLS), BF16),
            compiler_params=_params(("parallel", "parallel")),
            name="attn_in",
        )(xs[p.idx], modl, norm_pre, w, *tables[p.idx], qn, kn)

    return tuple(call(p) for p in parts)


def _sink_column(sink_ref, g, n):
    return jnp.concatenate(
        [jnp.broadcast_to(sink_ref[g * REP + h:g * REP + h + 1, 0:1] * LOG2E, (n, 1)) for h in range(REP)], axis=0)


def _attn_a_kernel(q_ref, kp_ref, kc_ref, kn_ref, vp_ref, vc_ref, vn_ref, kx_ref, vx_ref,
                   g_ref, sink_ref, o_ref, *, n_lat):
    i = pl.program_id(1)
    tq = TQ_A
    rows = REP * tq
    r = lax.broadcasted_iota(jnp.int32, (rows, tq), 0) & (tq - 1)
    c = lax.broadcasted_iota(jnp.int32, (rows, tq), 1)
    for a in range(NB_A):
        blk = NB_A * i + a
        rq = slice(a * tq, (a + 1) * tq)
        m_prev = (c - r) >= jnp.where(blk > 0, 0, tq)
        m_next = (r - c) >= jnp.where(blk < n_lat - 1, 0, tq)
        for g in range(A_KV_HEADS):
            gs = slice(g * HEAD_DIM, (g + 1) * HEAD_DIM)
            if a == 0:
                k_p, v_p = kp_ref[:, gs], vp_ref[:, gs]
            else:
                k_p, v_p = kc_ref[(a - 1) * tq:a * tq, gs], vc_ref[(a - 1) * tq:a * tq, gs]
            if a == NB_A - 1:
                k_n, v_n = kn_ref[:, gs], vn_ref[:, gs]
            else:
                k_n, v_n = kc_ref[(a + 1) * tq:(a + 2) * tq, gs], vc_ref[(a + 1) * tq:(a + 2) * tq, gs]
            k_c, v_c = kc_ref[rq, gs], vc_ref[rq, gs]
            q = jnp.concatenate(
                [q_ref[rq, (g * REP + h) * HEAD_DIM:(g * REP + h + 1) * HEAD_DIM] for h in range(REP)], axis=0)
            sp = jnp.where(m_prev, _dot_nt(q, k_p), NEG)
            sc = _dot_nt(q, k_c)
            sn = jnp.where(m_next, _dot_nt(q, k_n), NEG)
            sx = _dot_nt(q, kx_ref[:, gs])
            sk = _sink_column(sink_ref, g, tq)
            sx0, sx1 = sx[:, 0:tq], sx[:, tq:2 * tq]
            m_t = jnp.maximum(jnp.maximum(jnp.maximum(sp, sc), jnp.maximum(sn, sx0)), sx1)
            m = jnp.maximum(jnp.max(m_t, axis=1, keepdims=True), sk)
            pp = jnp.exp2(sp - m)
            pc = jnp.exp2(sc - m)
            pn = jnp.exp2(sn - m)
            px = jnp.exp2(sx - m)
            den_t = (pp + pc) + (pn + px[:, 0:tq]) + px[:, tq:2 * tq]
            den = jnp.sum(den_t, axis=1, keepdims=True) + jnp.exp2(sk - m)
            o = (_dot(pp.astype(BF16), v_p) + _dot(pc.astype(BF16), v_c)
                 + _dot(pn.astype(BF16), v_n) + _dot(px.astype(BF16), vx_ref[:, gs])) * (1.0 / den)
            for h in range(REP):
                cs = slice((g * REP + h) * HEAD_DIM, (g * REP + h + 1) * HEAD_DIM)
                o_ref[rq, cs] = (o[h * tq:(h + 1) * tq] * g_ref[rq, cs].astype(F32)).astype(BF16)


def _attn_a_ctx_kernel(q_ref, kx_ref, vx_ref, g_ref, sink_ref, o_ref):
    n = q_ref.shape[0]
    for g in range(A_KV_HEADS):
        gs = slice(g * HEAD_DIM, (g + 1) * HEAD_DIM)
        q = jnp.concatenate(
            [q_ref[:, (g * REP + h) * HEAD_DIM:(g * REP + h + 1) * HEAD_DIM] for h in range(REP)], axis=0)
        sx = _dot_nt(q, kx_ref[:, gs])
        sk = _sink_column(sink_ref, g, n)
        m = jnp.maximum(jnp.max(sx, axis=1, keepdims=True), sk)
        px = jnp.exp2(sx - m)
        den = jnp.sum(px, axis=1, keepdims=True) + jnp.exp2(sk - m)
        o = _dot(px.astype(BF16), vx_ref[:, gs]) * (1.0 / den)
        for h in range(REP):
            cs = slice((g * REP + h) * HEAD_DIM, (g * REP + h + 1) * HEAD_DIM)
            o_ref[:, cs] = (o[h * n:(h + 1) * n] * g_ref[:, cs].astype(F32)).astype(BF16)


def _attn_a(qkv, sink_b):
    q_lat, q_ctx = qkv
    b, n_lat_tok, _ = q_lat.shape
    ctx_len = q_ctx.shape[1]
    tq = TQ_A
    n_lat = n_lat_tok // tq
    kvw = A_KV_HEADS * HEAD_DIM
    step = NB_A * tq
    assert ctx_len == 2 * tq and n_lat_tok % step == 0
    qw = A_Q_HEADS * HEAD_DIM
    prev = lambda cb: (lambda bi, i: (bi, jnp.maximum(NB_A * i - 1, 0), cb))
    cur = lambda cb: (lambda bi, i: (bi, i, cb))
    nxt = lambda cb: (lambda bi, i: (bi, jnp.minimum(NB_A * (i + 1), n_lat - 1), cb))
    whole = lambda cb: (lambda bi, i: (bi, 0, cb))
    kb, vb = KA0 // kvw, VA0 // kvw
    sink_spec = pl.BlockSpec((A_Q_HEADS, HEAD_DIM), _CONST)
    y_lat = pl.pallas_call(
        functools.partial(_attn_a_kernel, n_lat=n_lat),
        grid=(b, n_lat_tok // step),
        in_specs=[
            pl.BlockSpec((None, step, qw), cur(QA0 // qw)),
            pl.BlockSpec((None, tq, kvw), prev(kb)),
            pl.BlockSpec((None, step, kvw), cur(kb)),
            pl.BlockSpec((None, tq, kvw), nxt(kb)),
            pl.BlockSpec((None, tq, kvw), prev(vb)),
            pl.BlockSpec((None, step, kvw), cur(vb)),
            pl.BlockSpec((None, tq, kvw), nxt(vb)),
            pl.BlockSpec((None, ctx_len, kvw), whole(kb)),
            pl.BlockSpec((None, ctx_len, kvw), whole(vb)),
            pl.BlockSpec((None, step, qw), cur(GA0 // qw)),
            sink_spec,
        ],
        out_specs=pl.BlockSpec((None, step, qw), _ROW),
        out_shape=jax.ShapeDtypeStruct((b, n_lat_tok, qw), BF16),
        compiler_params=_params(("parallel", "parallel")),
        name="attn_window",
    )(q_lat, q_lat, q_lat, q_lat, q_lat, q_lat, q_lat, q_ctx, q_ctx, q_lat, sink_b)
    y_ctx = pl.pallas_call(
        _attn_a_ctx_kernel,
        grid=(b, 1),
        in_specs=[
            pl.BlockSpec((None, ctx_len, qw), whole(QA0 // qw)),
            pl.BlockSpec((None, ctx_len, kvw), whole(kb)),
            pl.BlockSpec((None, ctx_len, kvw), whole(vb)),
            pl.BlockSpec((None, ctx_len, qw), whole(GA0 // qw)),
            sink_spec,
        ],
        out_specs=pl.BlockSpec((None, ctx_len, qw), _ROW),
        out_shape=jax.ShapeDtypeStruct((b, ctx_len, qw), BF16),
        compiler_params=_params(("parallel", "parallel")),
        name="attn_window_ctx",
    )(q_ctx, q_ctx, q_ctx, q_ctx, sink_b)
    return y_lat, y_ctx


def _attn_b_kernel(q_ref, k_ref, v_ref, kx_ref, vx_ref, g_ref, o_ref, m_sc, l_sc, acc_sc, p_sc, alpha_sc):
    tq, tk = TQ_B, TK_B
    n_lat_tok = k_ref.shape[0]
    q = jnp.concatenate([q_ref[:, h * HEAD_DIM:(h + 1) * HEAD_DIM] for h in range(REP)], axis=0)
    m_sc[...] = jnp.full(m_sc.shape, NEG, F32)
    l_sc[...] = jnp.zeros(l_sc.shape, F32)
    acc_sc[...] = jnp.zeros(acc_sc.shape, F32)

    rb = 128

    def scores(k, slot):
        nk = k.shape[0]
        s = _dot_nt(q, k)
        for r0 in range(0, REP * tq, rb):
            rs = slice(r0, r0 + rb)
            s_b = s[rs]
            m_old = m_sc[rs]
            m_new = jnp.maximum(m_old, jnp.max(s_b, axis=1, keepdims=True))
            alpha = jnp.exp2(m_old - m_new)
            p = jnp.exp2(s_b - jnp.tile(m_new, (1, nk // LANES)))
            l_sc[rs] = alpha * l_sc[rs] + jnp.sum(p, axis=1, keepdims=True)
            m_sc[rs] = m_new
            alpha_sc[slot, rs] = alpha
            p_sc[slot, rs, 0:nk] = p.astype(BF16)

    def accumulate(slot, v):
        nk = v.shape[0]
        acc_sc[...] = alpha_sc[slot] * acc_sc[...] + _dot(p_sc[slot, :, 0:nk], v)

    def kv(ref, ci):
        if isinstance(ci, int):
            return ref[ci * tk:(ci + 1) * tk, :]
        return ref[pl.ds(pl.multiple_of(ci * tk, tk), tk), :]

    n_main = n_lat_tok // tk
    assert n_main % 2 == 0 and n_lat_tok % tk == 0
    scores(kv(k_ref, 0), 0)

    def pair(c1):
        scores(kv(k_ref, c1), 1)
        accumulate(0, kv(v_ref, c1 - 1))
        scores(kv(k_ref, c1 + 1), 0)
        accumulate(1, kv(v_ref, c1))

    def body(j, carry):
        pair(4 * j + 1)
        pair(4 * j + 3)
        return carry

    n_quads = (n_main - 2) // 4
    lax.fori_loop(0, n_quads, body, 0)
    for c1 in range(4 * n_quads + 1, n_main - 1, 2):
        pair(c1)
    scores(kv(k_ref, n_main - 1), 1)
    accumulate(0, kv(v_ref, n_main - 2))
    scores(kx_ref[...], 0)
    accumulate(1, kv(v_ref, n_main - 1))
    accumulate(0, vx_ref[...])

    o = acc_sc[...] * (1.0 / l_sc[...])
    for h in range(REP):
        cs = slice(h * HEAD_DIM, (h + 1) * HEAD_DIM)
        o_ref[:, cs] = (o[h * tq:(h + 1) * tq] * g_ref[:, cs].astype(F32)).astype(BF16)


def _attn_b_ctx_kernel(q_ref, k_ref, v_ref, g_ref, o_ref):
    n = q_ref.shape[0]
    q = jnp.concatenate([q_ref[:, h * HEAD_DIM:(h + 1) * HEAD_DIM] for h in range(REP)], axis=0)
    s = _dot_nt(q, k_ref[...])
    p = jnp.exp2(s - jnp.max(s, axis=1, keepdims=True))
    o = _dot(p.astype(BF16), v_ref[...]) * (1.0 / jnp.sum(p, axis=1, keepdims=True))
    for h in range(REP):
        cs = slice(h * HEAD_DIM, (h + 1) * HEAD_DIM)
        o_ref[:, cs] = (o[h * n:(h + 1) * n] * g_ref[:, cs].astype(F32)).astype(BF16)


def _attn_b(qkv):
    q_lat, q_ctx = qkv
    b, n_lat_tok, _ = q_lat.shape
    n_ctx_tok = q_ctx.shape[1]
    tq = TQ_B
    gw = REP * HEAD_DIM
    assert n_lat_tok % tq == 0 and n_ctx_tok <= TK_B and n_ctx_tok % LANES == 0
    kcol, vcol = KB0 // HEAD_DIM, VB0 // HEAD_DIM
    y_lat = pl.pallas_call(
        _attn_b_kernel,
        grid=(b, B_KV_HEADS, n_lat_tok // tq),
        in_specs=[
            pl.BlockSpec((None, tq, gw), lambda bi, g, i: (bi, i, QB0 // gw + g)),
            pl.BlockSpec((None, n_lat_tok, HEAD_DIM), lambda bi, g, i: (bi, 0, kcol + g)),
            pl.BlockSpec((None, n_lat_tok, HEAD_DIM), lambda bi, g, i: (bi, 0, vcol + g)),
            pl.BlockSpec((None, n_ctx_tok, HEAD_DIM), lambda bi, g, i: (bi, 0, kcol + g)),
            pl.BlockSpec((None, n_ctx_tok, HEAD_DIM), lambda bi, g, i: (bi, 0, vcol + g)),
            pl.BlockSpec((None, tq, gw), lambda bi, g, i: (bi, i, GB0 // gw + g)),
        ],
        out_specs=pl.BlockSpec((None, tq, gw), lambda bi, g, i: (bi, i, g)),
        out_shape=jax.ShapeDtypeStruct((b, n_lat_tok, B_Q_HEADS * HEAD_DIM), BF16),
        scratch_shapes=[
            pltpu.VMEM((REP * tq, LANES), F32),
            pltpu.VMEM((REP * tq, LANES), F32),
            pltpu.VMEM((REP * tq, HEAD_DIM), F32),
            pltpu.VMEM((2, REP * tq, TK_B), BF16),
            pltpu.VMEM((2, REP * tq, LANES), F32),
        ],
        compiler_params=_params(("parallel", "parallel", "parallel")),
        name="attn_dense",
    )(q_lat, q_lat, q_lat, q_ctx, q_ctx, q_lat)
    y_ctx = pl.pallas_call(
        _attn_b_ctx_kernel,
        grid=(b, B_KV_HEADS),
        in_specs=[
            pl.BlockSpec((None, n_ctx_tok, gw), lambda bi, g: (bi, 0, QB0 // gw + g)),
            pl.BlockSpec((None, n_ctx_tok, HEAD_DIM), lambda bi, g: (bi, 0, kcol + g)),
            pl.BlockSpec((None, n_ctx_tok, HEAD_DIM), lambda bi, g: (bi, 0, vcol + g)),
            pl.BlockSpec((None, n_ctx_tok, gw), lambda bi, g: (bi, 0, GB0 // gw + g)),
        ],
        out_specs=pl.BlockSpec((None, n_ctx_tok, gw), lambda bi, g: (bi, 0, g)),
        out_shape=jax.ShapeDtypeStruct((b, n_ctx_tok, B_Q_HEADS * HEAD_DIM), BF16),
        compiler_params=_params(("parallel", "parallel")),
        name="attn_dense_ctx",
    )(q_ctx, q_ctx, q_ctx, q_ctx)
    return y_lat, y_ctx


def _residual(x, y, mod, w_post):
    return x + mod[2:3] * _rms(y, w_post)


def _attn_out_kernel(ya_ref, yb_ref, wa_ref, wb_ref, x_ref, mod_ref, np_ref, o_ref):
    y = _dot(ya_ref[...], wa_ref[...]) + _dot(yb_ref[...], wb_ref[...])
    o_ref[...] = _residual(x_ref[...], y, mod_ref[...], np_ref[...])


def _attn_out(parts, ya, yb, wa, wb, xs, modl, norm_post):
    b, d = modl.shape[0], modl.shape[-1]

    def call(p):
        return pl.pallas_call(
            _attn_out_kernel,
            grid=(b, p.rows // p.tm),
            in_specs=[
                pl.BlockSpec((None, p.tm, wa.shape[0]), _ROW),
                pl.BlockSpec((None, p.tm, wb.shape[0]), _ROW),
                pl.BlockSpec(wa.shape, _CONST),
                pl.BlockSpec(wb.shape, _CONST),
                pl.BlockSpec((None, p.tm, d), _ROW),
                _mod_spec(p, d),
                pl.BlockSpec((1, d), _CONST),
            ],
            out_specs=pl.BlockSpec((None, p.tm, d), _ROW),
            out_shape=jax.ShapeDtypeStruct((b, p.rows, d), F32),
            compiler_params=_params(("parallel", "parallel")),
            name="attn_out",
        )(ya[p.idx], yb[p.idx], wa, wb, xs[p.idx], modl, norm_post)

    return tuple(call(p) for p in parts)


def _softplus(t):
    return jnp.maximum(t, 0.0) + jnp.log(1.0 + jnp.exp(-jnp.abs(t)))


def _ssm_in_kernel(x_ref, xp_ref, xn_ref, mod_ref, np_ref, wz_ref, wx_ref, wdt_ref, wdtt_ref,
                   cw_ref, cb_ref, dtb_ref, dtbt_ref, z_ref, xbc_ref, dt_ref, dtt_ref):
    i = pl.program_id(1)
    tm = x_ref.shape[0]
    mod = mod_ref[...]
    w_pre = np_ref[...]
    h = _pre_norm(x_ref[...], mod, w_pre)
    hb = h.astype(BF16)
    has_prev = (i > 0).astype(F32)
    has_next = (i < pl.num_programs(1) - 1).astype(F32)
    hp = _pre_norm(xp_ref[...], mod, w_pre) * has_prev
    hn = _pre_norm(xn_ref[...], mod, w_pre) * has_next
    ext = jnp.concatenate([hp, h, hn], axis=0).astype(BF16)

    z_ref[...] = _silu(_dot(hb, wz_ref[...])).astype(BF16)
    dt_ref[...] = _softplus(_dot(hb, wdt_ref[...]) + dtb_ref[...])
    dtt_ref[...] = _softplus(_dot_nt(wdtt_ref[...], hb) + dtbt_ref[...])

    nblk = CONV_COLS
    nt = tm // 8
    sub = lax.broadcasted_iota(jnp.int32, (8, nblk), 0)
    for j in range(wx_ref.shape[1] // nblk):
        cs = slice(j * nblk, (j + 1) * nblk)
        u3 = _dot(ext, wx_ref[:, cs]).reshape(nt + 2, 8, nblk)
        dn = pltpu.roll(u3, 1, 1)
        upw = pltpu.roll(u3, 7, 1)
        u_prev = jnp.where(sub == 0, dn[0:nt], dn[1:nt + 1])
        u_next = jnp.where(sub == 7, upw[2:nt + 2], upw[1:nt + 1])
        conv = (cb_ref[:, cs] + cw_ref[0:1, cs] * u_prev + cw_ref[1:2, cs] * u3[1:nt + 1]
                + cw_ref[2:3, cs] * u_next)
        xbc_ref[:, cs] = _silu(conv).reshape(tm, nblk).astype(BF16)


def _ssm_in(parts, xs, modl, norm_pre, wz, wx, wdt, wdtt, cw, cb, dtb, dtbt):
    b, d = modl.shape[0], modl.shape[-1]
    nh2 = wdt.shape[1]

    def call(p):
        r8 = p.tm // 8
        last8 = p.rows // 8 - 1
        src = xs[p.idx]
        return pl.pallas_call(
            _ssm_in_kernel,
            grid=(b, p.rows // p.tm),
            in_specs=[
                pl.BlockSpec((None, p.tm, d), _ROW),
                pl.BlockSpec((None, 8, d), lambda bi, i: (bi, jnp.maximum(i * r8 - 1, 0), 0)),
                pl.BlockSpec((None, 8, d), lambda bi, i: (bi, jnp.minimum((i + 1) * r8, last8), 0)),
                _mod_spec(p, d),
                pl.BlockSpec((1, d), _CONST),
                pl.BlockSpec(wz.shape, _CONST),
                pl.BlockSpec(wx.shape, _CONST),
                pl.BlockSpec(wdt.shape, _CONST),
                pl.BlockSpec(wdtt.shape, _CONST),
                pl.BlockSpec(cw.shape, _CONST),
                pl.BlockSpec(cb.shape, _CONST),
                pl.BlockSpec(dtb.shape, _CONST),
                pl.BlockSpec(dtbt.shape, _CONST),
            ],
            out_specs=[
                pl.BlockSpec((None, p.tm, wz.shape[1]), _ROW),
                pl.BlockSpec((None, p.tm, wx.shape[1]), _ROW),
                pl.BlockSpec((None, p.tm, nh2), _ROW),
                pl.BlockSpec((None, nh2, p.tm), lambda bi, i: (bi, 0, i)),
            ],
            out_shape=[
                jax.ShapeDtypeStruct((b, p.rows, wz.shape[1]), BF16),
                jax.ShapeDtypeStruct((b, p.rows, wx.shape[1]), BF16),
                jax.ShapeDtypeStruct((b, p.rows, nh2), F32),
                jax.ShapeDtypeStruct((b, nh2, p.rows), F32),
            ],
            compiler_params=_params(("parallel", "parallel")),
            name="ssm_in",
        )(src, src, src, modl, norm_pre, wz, wx, wdt, wdtt, cw, cb, dtb, dtbt)

    lat, cx = (call(p) for p in parts)
    return tuple(zip(lat, cx))


def _split3(t):
    hi = t.astype(BF16)
    r1 = t - hi.astype(F32)
    mid = r1.astype(BF16)
    lo = (r1 - mid.astype(F32)).astype(BF16)
    return hi, mid, lo


def _ones_dot_lhs(tri01, a):
    return _dot(jnp.concatenate([tri01] * 3, axis=1), jnp.concatenate(_split3(a), axis=0))


def _ones_dot_rhs(at, tri01):
    return _dot(jnp.concatenate(_split3(at), axis=1), jnp.concatenate([tri01] * 3, axis=0))


def _ssd_direction(xbc_ref, rs, dt, dtt, a_row, a_col, h_sc, y_ref, reverse, hoff):
    q_len = SSM_CHUNK
    d_inner = SSM_HEADS * SSM_HEAD_DIM
    gw = SSM_REP * SSM_HEAD_DIM
    row = lax.broadcasted_iota(jnp.int32, (q_len, q_len), 0)
    col = lax.broadcasted_iota(jnp.int32, (q_len, q_len), 1)
    lower = row >= col
    upper = row <= col
    mask = upper if reverse else lower
    tri = jnp.where(mask, 1.0, 0.0).astype(BF16)
    tri_t = jnp.where(lower if reverse else upper, 1.0, 0.0).astype(BF16)

    a = dt * a_row
    at = dtt * a_col
    acum = _ones_dot_lhs(tri, a)
    acum_t = _ones_dot_rhs(at, tri_t)
    total_t = jnp.sum(at, axis=1, keepdims=True)
    acum_t = acum_t - jnp.log2(dtt)
    w_t = jnp.exp2(total_t - acum_t)
    etot = jnp.exp2(jnp.sum(a, axis=0, keepdims=True))
    head_of_col = jnp.right_shift(lax.broadcasted_iota(jnp.int32, (q_len, gw), 1),
                                  SSM_HEAD_DIM.bit_length() - 1)

    def group(g):
        b_g = xbc_ref[rs, d_inner + g * D_STATE:d_inner + (g + 1) * D_STATE]
        c_g = xbc_ref[rs, d_inner + (SSM_GROUPS + g) * D_STATE:d_inner + (SSM_GROUPS + g + 1) * D_STATE]
        gsl = slice(g * gw, (g + 1) * gw)
        x_g = xbc_ref[rs, gsl]
        cb = _dot_nt(c_g, b_g).astype(BF16)
        b_t = b_g.astype(F32).T.astype(BF16)
        h_g = h_sc[g]
        y_state = _dot(c_g, h_g.astype(BF16))
        ms, bws, bdx = [], [], []
        etot_row = None
        e_in = None
        for r in range(SSM_REP):
            h = hoff + g * SSM_REP + r
            acb = jnp.broadcast_to(acum[:, h:h + 1], (q_len, q_len))
            decay = jnp.exp2(jnp.where(mask, acb - acum_t[h:h + 1, :], NEG))
            ms.append(cb * decay.astype(BF16))
            bws.append(b_t * jnp.broadcast_to(w_t[h:h + 1, :], b_t.shape).astype(BF16))
            sel = head_of_col == r
            bdx.append(jnp.where(sel, x_g, jnp.zeros_like(x_g)))
            e_q = jnp.exp2(acb)
            e_q = jnp.concatenate([e_q] * (gw // q_len), axis=1)
            e_in = e_q if e_in is None else jnp.where(sel, e_q, e_in)
            e_r = jnp.broadcast_to(etot[:, h:h + 1], (1, gw))
            etot_row = e_r if etot_row is None else jnp.where(head_of_col[0:1] == r, e_r, etot_row)
        bdx = jnp.concatenate(bdx, axis=0)
        lhs = jnp.concatenate([jnp.concatenate(ms, axis=1), jnp.concatenate(bws, axis=1)], axis=0)
        res = _dot(lhs, bdx)
        y = res[0:q_len] + y_state * e_in
        y_ref[rs, gsl] = y.astype(BF16)
        h_sc[g] = h_g * etot_row + res[q_len:]

    return group


def _ssd_kernel(*refs, n_sub, has_init, emit_state):
    xf_ref, xb_ref, dtf_ref, dtb_ref, dttf_ref, dttb_ref, alog_ref, alogt_ref = refs[:8]
    refs = refs[8:]
    if has_init:
        hf0_ref, hb0_ref = refs[:2]
        refs = refs[2:]
    yf_ref, yb_ref = refs[:2]
    refs = refs[2:]
    if emit_state:
        hf_out_ref, hb_out_ref = refs[:2]
        refs = refs[2:]
    hf_sc, hb_sc = refs

    @pl.when(pl.program_id(1) == 0)
    def _():
        if has_init:
            hf_sc[...] = hf0_ref[...]
            hb_sc[...] = hb0_ref[...]
        else:
            hf_sc[...] = jnp.zeros(hf_sc.shape, F32)
            hb_sc[...] = jnp.zeros(hb_sc.shape, F32)

    a_row = -jnp.exp(alog_ref[...]) * LOG2E
    a_col = -jnp.exp(alogt_ref[...]) * LOG2E
    for sub in range(n_sub):
        rs = slice(sub * SSM_CHUNK, (sub + 1) * SSM_CHUNK)
        fwd = _ssd_direction(xf_ref, rs, dtf_ref[rs, :], dttf_ref[:, rs], a_row, a_col, hf_sc, yf_ref, False, 0)
        for g in range(SSM_GROUPS):
            fwd(g)
    for sub in reversed(range(n_sub)):
        rs = slice(sub * SSM_CHUNK, (sub + 1) * SSM_CHUNK)
        bwd = _ssd_direction(xb_ref, rs, dtb_ref[rs, :], dttb_ref[:, rs], a_row, a_col, hb_sc, yb_ref, True,
                             SSM_HEADS)
        for g in range(SSM_GROUPS):
            bwd(g)

    if emit_state:
        hf_out_ref[...] = hf_sc[...]
        hb_out_ref[...] = hb_sc[...]


def _ssd_stream(xbc, dt, dtt, alog, alogt, init, emit_state):
    b, rows, cw = xbc.shape
    n_sub = min(SSD_SUB, rows // SSM_CHUNK)
    q_len = n_sub * SSM_CHUNK
    assert rows % q_len == 0
    n_c = rows // q_len
    d_inner = SSM_HEADS * SSM_HEAD_DIM
    nh2 = dt.shape[-1]
    state_shape = (SSM_GROUPS, D_STATE, SSM_REP * SSM_HEAD_DIM)
    asc = lambda bi, j: (bi, j, 0)
    desc = lambda bi, j: (bi, n_c - 1 - j, 0)
    state_spec = pl.BlockSpec((None,) + state_shape, lambda bi, j: (bi, 0, 0, 0))
    in_specs = [
        pl.BlockSpec((None, q_len, cw), asc),
        pl.BlockSpec((None, q_len, cw), desc),
        pl.BlockSpec((None, q_len, nh2), asc),
        pl.BlockSpec((None, q_len, nh2), desc),
        pl.BlockSpec((None, nh2, q_len), lambda bi, j: (bi, 0, j)),
        pl.BlockSpec((None, nh2, q_len), lambda bi, j: (bi, 0, n_c - 1 - j)),
        pl.BlockSpec(alog.shape, _CONST),
        pl.BlockSpec(alogt.shape, _CONST),
    ]
    args = [xbc, xbc, dt, dt, dtt, dtt, alog, alogt]
    if init is not None:
        in_specs += [state_spec, state_spec]
        args += list(init)
    out_specs = [pl.BlockSpec((None, q_len, d_inner), asc), pl.BlockSpec((None, q_len, d_inner), desc)]
    out_shape = [jax.ShapeDtypeStruct((b, rows, d_inner), BF16)] * 2
    if emit_state:
        out_specs += [state_spec, state_spec]
        out_shape += [jax.ShapeDtypeStruct((b,) + state_shape, F32)] * 2
    return pl.pallas_call(
        functools.partial(_ssd_kernel, n_sub=n_sub, has_init=init is not None, emit_state=emit_state),
        grid=(b, n_c),
        in_specs=in_specs,
        out_specs=out_specs,
        out_shape=out_shape,
        scratch_shapes=[pltpu.VMEM(state_shape, F32)] * 2,
        compiler_params=_params(("parallel", "arbitrary")),
        name="ssd_scan",
    )(*args)


def _ssd(xbc, dt, dtt, alog, alogt):
    yf_c, yb_c, hf, hb = _ssd_stream(xbc[1], dt[1], dtt[1], alog, alogt, None, True)
    yf_l, yb_l = _ssd_stream(xbc[0], dt[0], dtt[0], alog, alogt, (hf, hb), False)
    return (yf_l, yf_c), (yb_l, yb_c)


def _ssm_out_kernel(yf_ref, yb_ref, xs_ref, z_ref, dsk_ref, nw_ref, w_ref, x_ref, mod_ref, np_ref, o_ref):
    y = yf_ref[...].astype(F32) + yb_ref[...].astype(F32) + dsk_ref[...] * xs_ref[...].astype(F32)
    gated = y * z_ref[...].astype(F32)
    gsz = gated.shape[1] // SSM_GROUPS
    parts = []
    for g in range(SSM_GROUPS):
        t = gated[:, g * gsz:(g + 1) * gsz]
        parts.append(t * lax.rsqrt(jnp.mean(t * t, axis=-1, keepdims=True) + EPS))
    gn = (jnp.concatenate(parts, axis=1) * nw_ref[...]).astype(BF16)
    o_ref[...] = _residual(x_ref[...], _dot(gn, w_ref[...]), mod_ref[...], np_ref[...])


def _ssm_out(parts, yf, yb, xbc, z, dsk, nw, w, xs, modl, norm_post):
    b, d = modl.shape[0], modl.shape[-1]
    di = w.shape[0]

    def call(p):
        wide = pl.BlockSpec((None, p.tm, di), _ROW)
        return pl.pallas_call(
            _ssm_out_kernel,
            grid=(b, p.rows // p.tm),
            in_specs=[
                wide, wide, wide, wide,
                pl.BlockSpec((1, di), _CONST),
                pl.BlockSpec((1, di), _CONST),
                pl.BlockSpec(w.shape, _CONST),
                pl.BlockSpec((None, p.tm, d), _ROW),
                _mod_spec(p, d),
                pl.BlockSpec((1, d), _CONST),
            ],
            out_specs=pl.BlockSpec((None, p.tm, d), _ROW),
            out_shape=jax.ShapeDtypeStruct((b, p.rows, d), F32),
            compiler_params=_params(("parallel", "parallel")),
            name="ssm_out",
        )(yf[p.idx], yb[p.idx], xbc[p.idx], z[p.idx], dsk, nw, w, xs[p.idx], modl, norm_post)

    return tuple(call(p) for p in parts)


def _rope_tables(n_lat_tok, n_ctx_tok):
    t = np.arange(n_lat_tok)
    n_freq = HEAD_DIM // 4
    inv = 1.0 / (ROPE_THETA ** (jnp.arange(n_freq, dtype=F32) / n_freq))
    rowp = jnp.asarray(t // GRID_W, F32)
    colp = jnp.asarray(t % GRID_W, F32)
    ang = jnp.concatenate([rowp[:, None] * inv, colp[:, None] * inv], axis=-1)
    cos, sin = jnp.cos(ang), jnp.sin(ang)
    lat = (jnp.concatenate([cos, cos], axis=-1), jnp.concatenate([-sin, sin], axis=-1))
    cx = (jnp.ones((n_ctx_tok, HEAD_DIM), F32), jnp.zeros((n_ctx_tok, HEAD_DIM), F32))
    return lat, cx


_DEINT = np.concatenate([np.arange(0, HEAD_DIM, 2), np.arange(1, HEAD_DIM, 2)])


def _attn_weight(w):
    d = w.shape[0]
    w = w.astype(BF16)

    def deint(seg):
        n = seg.shape[1] // HEAD_DIM
        return seg.reshape(d, n, HEAD_DIM // 2, 2).transpose(0, 1, 3, 2).reshape(d, n * HEAD_DIM)

    qa, ka, va, ga, qb, kb, vb, gb = jnp.split(w, [1024, 1280, 1536, 2560, 3584, 3840, 4096], axis=1)
    return jnp.concatenate([deint(qa), ga, deint(qb), gb, deint(ka), va, deint(kb), vb], axis=1)


def kernel(x, c, ctx, c_ctx, w_ada, b_ada, norm_pre, norm_post, attn_w_in, attn_w_out, attn_sink,
           attn_q_norm, attn_k_norm, ssm_w_in, ssm_conv_w, ssm_conv_b, ssm_dt_bias, ssm_a_log, ssm_d,
           ssm_norm, ssm_w_out):
    bsz, n_lat_tok, d = x.shape
    n_ctx_tok = ctx.shape[1]
    depth = w_ada.shape[0]
    assert n_lat_tok % TM_OUT == 0 and n_lat_tok % TM == 0 and n_ctx_tok % LANES == 0 and bsz <= 7
    d_inner = SSM_HEADS * SSM_HEAD_DIM
    bc_w = 2 * SSM_GROUPS * D_STATE

    xs = (x, ctx)
    in_parts = _parts(n_lat_tok, n_ctx_tok, TM)
    out_parts = _parts(n_lat_tok, n_ctx_tok, TM_OUT)
    cc = jnp.zeros((8, d), F32).at[:bsz].set(c).at[bsz].set(c_ctx)
    mod = _modulation(cc, w_ada, b_ada)
    mod = mod.reshape(depth, 8, 3, d)
    tables = _rope_tables(n_lat_tok, n_ctx_tok)

    for l in range(depth):
        l_out_parts = out_parts[:1] if l == depth - 1 else out_parts
        modl = jnp.stack([mod[l, :bsz], jnp.broadcast_to(mod[l, bsz], (bsz, 3, d))], axis=1)
        npre = norm_pre[l].reshape(1, d)
        npost = norm_post[l].reshape(1, d)
        i = l // 2
        if l % 2 == 0:
            w = _attn_weight(attn_w_in[i])
            qn = attn_q_norm[i][_DEINT].reshape(1, HEAD_DIM)
            kn = attn_k_norm[i][_DEINT].reshape(1, HEAD_DIM)
            qkv = _attn_in(in_parts, xs, modl, npre, w, tables, qn, kn)
            sink_b = jnp.broadcast_to(attn_sink[i][:, None], (A_Q_HEADS, HEAD_DIM))
            ya = _attn_a(qkv, sink_b)
            yb = _attn_b(qkv)
            wo = attn_w_out[i].astype(BF16)
            aq = A_Q_HEADS * HEAD_DIM
            xs = _attn_out(l_out_parts, ya, yb, wo[:aq], wo[aq:], xs, modl, npost)
        else:
            w = ssm_w_in[i]
            wz = w[:, :d_inner].astype(BF16)
            wx = w[:, d_inner:2 * d_inner + bc_w].astype(BF16)
            wdt = jnp.pad(w[:, 2 * d_inner + bc_w:], ((0, 0), (0, HPAD - 2 * SSM_HEADS))).astype(BF16)
            dtb = jnp.pad(ssm_dt_bias[i].reshape(1, -1), ((0, 0), (0, HPAD - 2 * SSM_HEADS)))
            z, xbc, dt, dtt = _ssm_in(in_parts, xs, modl, npre, wz, wx, wdt, wdt.T, ssm_conv_w[i],
                                      ssm_conv_b[i].reshape(1, -1), dtb, dtb.reshape(-1, 1))
            alog = jnp.pad(ssm_a_log[i].reshape(1, -1), ((0, 0), (0, HPAD - 2 * SSM_HEADS)))
            yf, ybk = _ssd(xbc, dt, dtt, alog, alog.reshape(-1, 1))
            dsk = jnp.repeat(ssm_d[i], SSM_HEAD_DIM).reshape(1, d_inner)
            xs = _ssm_out(l_out_parts, yf, ybk, xbc, z, dsk, ssm_norm[i].reshape(1, d_inner),
                          ssm_w_out[i].astype(BF16), xs, modl, npost)
    return xs[0]
```

```python
import functools
from typing import NamedTuple

import numpy as np
import jax
import jax.numpy as jnp
from jax import lax
from jax.experimental import pallas as pl
from jax.experimental.pallas import tpu as pltpu

F32 = jnp.float32
BF16 = jnp.bfloat16

EPS = 1e-6
GRID_W = 64
ROPE_THETA = 10000.0
HEAD_DIM = 128
A_Q_HEADS = 8
A_KV_HEADS = 2
B_Q_HEADS = 8
B_KV_HEADS = 2
REP = 4
WINDOW = 128
SSM_HEAD_DIM = 64
SSM_HEADS = 32
SSM_GROUPS = 8
SSM_REP = SSM_HEADS // SSM_GROUPS
D_STATE = 128
SSM_CHUNK = 128
SSD_SUB = 4
HPAD = 128

V7X_VMEM_BYTES = 64 * 1024 * 1024
VMEM_LIMIT = V7X_VMEM_BYTES - 8 * 1024 * 1024

TM = 256
TM_OUT = 512
CONV_COLS = 512
TQ_A = 128
NB_A = 8
TQ_B = 512
TK_B = 512
LANES = 128
LOG2E = 1.4426950408889634

QA0, GA0, QB0, GB0, KA0, VA0, KB0, VB0 = 0, 1024, 2048, 3072, 4096, 4352, 4608, 4864
ATTN_COLS = 5120

NEG = -1e30


def _params(sem, vmem=VMEM_LIMIT):
    return pltpu.CompilerParams(dimension_semantics=sem, vmem_limit_bytes=vmem)


def _silu(t):
    return t * (1.0 / (1.0 + jnp.exp2(t * (-LOG2E))))


def _rms(t, w):
    return t * lax.rsqrt(jnp.mean(t * t, axis=-1, keepdims=True) + EPS) * w


def _dot(a, b):
    return jnp.dot(a, b, preferred_element_type=F32)


def _dot_nt(a, b):
    return lax.dot_general(a, b, (((1,), (1,)), ((), ())), preferred_element_type=F32)


def _dot_hi(a, b):
    return jnp.dot(a, b, preferred_element_type=F32, precision=lax.Precision.HIGHEST)


class _Part(NamedTuple):
    idx: int
    rows: int
    tm: int


def _parts(n_lat_tok, n_ctx_tok, tm):
    return (_Part(0, n_lat_tok, tm), _Part(1, n_ctx_tok, n_ctx_tok))


_ROW = lambda bi, i: (bi, i, 0)
_CONST = lambda bi, i: (0, 0)


def _mod_spec(p, d):
    return pl.BlockSpec((None, None, 3, d), lambda bi, i: (bi, p.idx, 0, 0))


def _mod_kernel(cc_ref, w_ref, b_ref, o_ref):
    o_ref[...] = _dot_hi(_silu(cc_ref[...]), w_ref[...]) + b_ref[...]


def _modulation(cc, w_ada, b_ada):
    depth, d, d3 = w_ada.shape
    return pl.pallas_call(
        _mod_kernel,
        grid=(depth,),
        in_specs=[
            pl.BlockSpec((8, d), lambda l: (0, 0)),
            pl.BlockSpec((None, d, d3), lambda l: (l, 0, 0)),
            pl.BlockSpec((None, 1, d3), lambda l: (l, 0, 0)),
        ],
        out_specs=pl.BlockSpec((None, 8, d3), lambda l: (l, 0, 0)),
        out_shape=jax.ShapeDtypeStruct((depth, 8, d3), F32),
        compiler_params=_params(("arbitrary",)),
        name="modulation",
    )(cc, w_ada, b_ada.reshape(depth, 1, d3))


def _pre_norm(x, mod, w):
    return _rms(x, w) * (1.0 + mod[1:2]) + mod[0:1]


def _attn_in_kernel(x_ref, mod_ref, np_ref, w_ref, cos_ref, sin_ref, qn_ref, kn_ref, o_ref):
    hb = _pre_norm(x_ref[...], mod_ref[...], np_ref[...]).astype(BF16)
    cos = cos_ref[...]
    sin = sin_ref[...]
    scale = HEAD_DIM ** -0.5

    def rope(t):
        return t * cos + pltpu.roll(t, HEAD_DIM // 2, 1) * sin

    nblk = 512
    for j in range(ATTN_COLS // nblk):
        c0 = j * nblk
        t = _dot(hb, w_ref[:, c0:c0 + nblk])
        for hh in range(nblk // HEAD_DIM):
            col = c0 + hh * HEAD_DIM
            th = t[:, hh * HEAD_DIM:(hh + 1) * HEAD_DIM]
            if col < GA0:
                th = rope(th) * (scale * LOG2E)
            elif col < QB0 or GB0 <= col < KA0:
                th = _silu(th)
            elif col < GB0:
                th = rope(_rms(th, qn_ref[...])) * (scale * LOG2E)
            elif col < VA0:
                th = rope(th)
            elif KB0 <= col < VB0:
                th = rope(_rms(th, kn_ref[...]))
            o_ref[:, col:col + HEAD_DIM] = th.astype(BF16)


def _attn_in(parts, xs, modl, norm_pre, w, tables, qn, kn):
    b, d = modl.shape[0], modl.shape[-1]

    def call(p):
        table = pl.BlockSpec((p.tm, HEAD_DIM), lambda bi, i: (i, 0))
        return pl.pallas_call(
            _attn_in_kernel,
            grid=(b, p.rows // p.tm),
            in_specs=[
                pl.BlockSpec((None, p.tm, d), _ROW),
                _mod_spec(p, d),
                pl.BlockSpec((1, d), _CONST),
                pl.BlockSpec((d, ATTN_COLS), _CONST),
                table,
                table,
                pl.BlockSpec((1, HEAD_DIM), _CONST),
                pl.BlockSpec((1, HEAD_DIM), _CONST),
            ],
            out_specs=pl.BlockSpec((None, p.tm, ATTN_COLS), _ROW),
            out_shape=jax.ShapeDtypeStruct((b, p.rows, ATTN_COLS), BF16),
            compiler_params=_params(("parallel", "parallel")),
            name="attn_in",
        )(xs[p.idx], modl, norm_pre, w, *tables[p.idx], qn, kn)

    return tuple(call(p) for p in parts)


def _sink_column(sink_ref, g, n):
    return jnp.concatenate(
        [jnp.broadcast_to(sink_ref[g * REP + h:g * REP + h + 1, 0:1] * LOG2E, (n, 1)) for h in range(REP)], axis=0)


def _attn_a_kernel(q_ref, kp_ref, kc_ref, kn_ref, vp_ref, vc_ref, vn_ref, kx_ref, vx_ref,
                   g_ref, sink_ref, o_ref, *, n_lat):
    i = pl.program_id(1)
    tq = TQ_A
    rows = REP * tq
    r = lax.broadcasted_iota(jnp.int32, (rows, tq), 0) & (tq - 1)
    c = lax.broadcasted_iota(jnp.int32, (rows, tq), 1)
    for a in range(NB_A):
        blk = NB_A * i + a
        rq = slice(a * tq, (a + 1) * tq)
        m_prev = (c - r) >= jnp.where(blk > 0, 0, tq)
        m_next = (r - c) >= jnp.where(blk < n_lat - 1, 0, tq)
        for g in range(A_KV_HEADS):
            gs = slice(g * HEAD_DIM, (g + 1) * HEAD_DIM)
            if a == 0:
                k_p, v_p = kp_ref[:, gs], vp_ref[:, gs]
            else:
                k_p, v_p = kc_ref[(a - 1) * tq:a * tq, gs], vc_ref[(a - 1) * tq:a * tq, gs]
            if a == NB_A - 1:
                k_n, v_n = kn_ref[:, gs], vn_ref[:, gs]
            else:
                k_n, v_n = kc_ref[(a + 1) * tq:(a + 2) * tq, gs], vc_ref[(a + 1) * tq:(a + 2) * tq, gs]
            k_c, v_c = kc_ref[rq, gs], vc_ref[rq, gs]
            q = jnp.concatenate(
                [q_ref[rq, (g * REP + h) * HEAD_DIM:(g * REP + h + 1) * HEAD_DIM] for h in range(REP)], axis=0)
            sp = jnp.where(m_prev, _dot_nt(q, k_p), NEG)
            sc = _dot_nt(q, k_c)
            sn = jnp.where(m_next, _dot_nt(q, k_n), NEG)
            sx = _dot_nt(q, kx_ref[:, gs])
            sk = _sink_column(sink_ref, g, tq)
            sx0, sx1 = sx[:, 0:tq], sx[:, tq:2 * tq]
            m_t = jnp.maximum(jnp.maximum(jnp.maximum(sp, sc), jnp.maximum(sn, sx0)), sx1)
            m = jnp.maximum(jnp.max(m_t, axis=1, keepdims=True), sk)
            pp = jnp.exp2(sp - m)
            pc = jnp.exp2(sc - m)
            pn = jnp.exp2(sn - m)
            px = jnp.exp2(sx - m)
            den_t = (pp + pc) + (pn + px[:, 0:tq]) + px[:, tq:2 * tq]
            den = jnp.sum(den_t, axis=1, keepdims=True) + jnp.exp2(sk - m)
            o = (_dot(pp.astype(BF16), v_p) + _dot(pc.astype(BF16), v_c)
                 + _dot(pn.astype(BF16), v_n) + _dot(px.astype(BF16), vx_ref[:, gs])) * (1.0 / den)
            for h in range(REP):
                cs = slice((g * REP + h) * HEAD_DIM, (g * REP + h + 1) * HEAD_DIM)
                o_ref[rq, cs] = (o[h * tq:(h + 1) * tq] * g_ref[rq, cs].astype(F32)).astype(BF16)


def _attn_a_ctx_kernel(q_ref, kx_ref, vx_ref, g_ref, sink_ref, o_ref):
    n = q_ref.shape[0]
    for g in range(A_KV_HEADS):
        gs = slice(g * HEAD_DIM, (g + 1) * HEAD_DIM)
        q = jnp.concatenate(
            [q_ref[:, (g * REP + h) * HEAD_DIM:(g * REP + h + 1) * HEAD_DIM] for h in range(REP)], axis=0)
        sx = _dot_nt(q, kx_ref[:, gs])
        sk = _sink_column(sink_ref, g, n)
        m = jnp.maximum(jnp.max(sx, axis=1, keepdims=True), sk)
        px = jnp.exp2(sx - m)
        den = jnp.sum(px, axis=1, keepdims=True) + jnp.exp2(sk - m)
        o = _dot(px.astype(BF16), vx_ref[:, gs]) * (1.0 / den)
        for h in range(REP):
            cs = slice((g * REP + h) * HEAD_DIM, (g * REP + h + 1) * HEAD_DIM)
            o_ref[:, cs] = (o[h * n:(h + 1) * n] * g_ref[:, cs].astype(F32)).astype(BF16)


def _attn_a(qkv, sink_b):
    q_lat, q_ctx = qkv
    b, n_lat_tok, _ = q_lat.shape
    ctx_len = q_ctx.shape[1]
    tq = TQ_A
    n_lat = n_lat_tok // tq
    kvw = A_KV_HEADS * HEAD_DIM
    step = NB_A * tq
    assert ctx_len == 2 * tq and n_lat_tok % step == 0
    qw = A_Q_HEADS * HEAD_DIM
    prev = lambda cb: (lambda bi, i: (bi, jnp.maximum(NB_A * i - 1, 0), cb))
    cur = lambda cb: (lambda bi, i: (bi, i, cb))
    nxt = lambda cb: (lambda bi, i: (bi, jnp.minimum(NB_A * (i + 1), n_lat - 1), cb))
    whole = lambda cb: (lambda bi, i: (bi, 0, cb))
    kb, vb = KA0 // kvw, VA0 // kvw
    sink_spec = pl.BlockSpec((A_Q_HEADS, HEAD_DIM), _CONST)
    y_lat = pl.pallas_call(
        functools.partial(_attn_a_kernel, n_lat=n_lat),
        grid=(b, n_lat_tok // step),
        in_specs=[
            pl.BlockSpec((None, step, qw), cur(QA0 // qw)),
            pl.BlockSpec((None, tq, kvw), prev(kb)),
            pl.BlockSpec((None, step, kvw), cur(kb)),
            pl.BlockSpec((None, tq, kvw), nxt(kb)),
            pl.BlockSpec((None, tq, kvw), prev(vb)),
            pl.BlockSpec((None, step, kvw), cur(vb)),
            pl.BlockSpec((None, tq, kvw), nxt(vb)),
            pl.BlockSpec((None, ctx_len, kvw), whole(kb)),
            pl.BlockSpec((None, ctx_len, kvw), whole(vb)),
            pl.BlockSpec((None, step, qw), cur(GA0 // qw)),
            sink_spec,
        ],
        out_specs=pl.BlockSpec((None, step, qw), _ROW),
        out_shape=jax.ShapeDtypeStruct((b, n_lat_tok, qw), BF16),
        compiler_params=_params(("parallel", "parallel")),
        name="attn_window",
    )(q_lat, q_lat, q_lat, q_lat, q_lat, q_lat, q_lat, q_ctx, q_ctx, q_lat, sink_b)
    y_ctx = pl.pallas_call(
        _attn_a_ctx_kernel,
        grid=(b, 1),
        in_specs=[
            pl.BlockSpec((None, ctx_len, qw), whole(QA0 // qw)),
            pl.BlockSpec((None, ctx_len, kvw), whole(kb)),
            pl.BlockSpec((None, ctx_len, kvw), whole(vb)),
            pl.BlockSpec((None, ctx_len, qw), whole(GA0 // qw)),
            sink_spec,
        ],
        out_specs=pl.BlockSpec((None, ctx_len, qw), _ROW),
        out_shape=jax.ShapeDtypeStruct((b, ctx_len, qw), BF16),
        compiler_params=_params(("parallel", "parallel")),
        name="attn_window_ctx",
    )(q_ctx, q_ctx, q_ctx, q_ctx, sink_b)
    return y_lat, y_ctx


def _attn_b_kernel(q_ref, k_ref, v_ref, kx_ref, vx_ref, g_ref, o_ref, m_sc, l_sc, acc_sc, p_sc, alpha_sc):
    tq, tk = TQ_B, TK_B
    n_lat_tok = k_ref.shape[0]
    q = jnp.concatenate([q_ref[:, h * HEAD_DIM:(h + 1) * HEAD_DIM] for h in range(REP)], axis=0)
    m_sc[...] = jnp.full(m_sc.shape, NEG, F32)
    l_sc[...] = jnp.zeros(l_sc.shape, F32)
    acc_sc[...] = jnp.zeros(acc_sc.shape, F32)

    rb = 128

    def scores(k, slot):
        nk = k.shape[0]
        s = _dot_nt(q, k)
        for r0 in range(0, REP * tq, rb):
            rs = slice(r0, r0 + rb)
            s_b = s[rs]
            m_old = m_sc[rs]
            m_new = jnp.maximum(m_old, jnp.max(s_b, axis=1, keepdims=True))
            alpha = jnp.exp2(m_old - m_new)
            p = jnp.exp2(s_b - jnp.tile(m_new, (1, nk // LANES)))
            l_sc[rs] = alpha * l_sc[rs] + jnp.sum(p, axis=1, keepdims=True)
            m_sc[rs] = m_new
            alpha_sc[slot, rs] = alpha
            p_sc[slot, rs, 0:nk] = p.astype(BF16)

    def accumulate(slot, v):
        nk = v.shape[0]
        acc_sc[...] = alpha_sc[slot] * acc_sc[...] + _dot(p_sc[slot, :, 0:nk], v)

    def kv(ref, ci):
        if isinstance(ci, int):
            return ref[ci * tk:(ci + 1) * tk, :]
        return ref[pl.ds(pl.multiple_of(ci * tk, tk), tk), :]

    n_main = n_lat_tok // tk
    assert n_main % 2 == 0 and n_lat_tok % tk == 0
    scores(kv(k_ref, 0), 0)

    def pair(c1):
        scores(kv(k_ref, c1), 1)
        accumulate(0, kv(v_ref, c1 - 1))
        scores(kv(k_ref, c1 + 1), 0)
        accumulate(1, kv(v_ref, c1))

    def body(j, carry):
        pair(4 * j + 1)
        pair(4 * j + 3)
        return carry

    n_quads = (n_main - 2) // 4
    lax.fori_loop(0, n_quads, body, 0)
    for c1 in range(4 * n_quads + 1, n_main - 1, 2):
        pair(c1)
    scores(kv(k_ref, n_main - 1), 1)
    accumulate(0, kv(v_ref, n_main - 2))
    scores(kx_ref[...], 0)
    accumulate(1, kv(v_ref, n_main - 1))
    accumulate(0, vx_ref[...])

    o = acc_sc[...] * (1.0 / l_sc[...])
    for h in range(REP):
        cs = slice(h * HEAD_DIM, (h + 1) * HEAD_DIM)
        o_ref[:, cs] = (o[h * tq:(h + 1) * tq] * g_ref[:, cs].astype(F32)).astype(BF16)


def _attn_b_ctx_kernel(q_ref, k_ref, v_ref, g_ref, o_ref):
    n = q_ref.shape[0]
    q = jnp.concatenate([q_ref[:, h * HEAD_DIM:(h + 1) * HEAD_DIM] for h in range(REP)], axis=0)
    s = _dot_nt(q, k_ref[...])
    p = jnp.exp2(s - jnp.max(s, axis=1, keepdims=True))
    o = _dot(p.astype(BF16), v_ref[...]) * (1.0 / jnp.sum(p, axis=1, keepdims=True))
    for h in range(REP):
        cs = slice(h * HEAD_DIM, (h + 1) * HEAD_DIM)
        o_ref[:, cs] = (o[h * n:(h + 1) * n] * g_ref[:, cs].astype(F32)).astype(BF16)


def _attn_b(qkv):
    q_lat, q_ctx = qkv
    b, n_lat_tok, _ = q_lat.shape
    n_ctx_tok = q_ctx.shape[1]
    tq = TQ_B
    gw = REP * HEAD_DIM
    assert n_lat_tok % tq == 0 and n_ctx_tok <= TK_B and n_ctx_tok % LANES == 0
    kcol, vcol = KB0 // HEAD_DIM, VB0 // HEAD_DIM
    y_lat = pl.pallas_call(
        _attn_b_kernel,
        grid=(b, B_KV_HEADS, n_lat_tok // tq),
        in_specs=[
            pl.BlockSpec((None, tq, gw), lambda bi, g, i: (bi, i, QB0 // gw + g)),
            pl.BlockSpec((None, n_lat_tok, HEAD_DIM), lambda bi, g, i: (bi, 0, kcol + g)),
            pl.BlockSpec((None, n_lat_tok, HEAD_DIM), lambda bi, g, i: (bi, 0, vcol + g)),
            pl.BlockSpec((None, n_ctx_tok, HEAD_DIM), lambda bi, g, i: (bi, 0, kcol + g)),
            pl.BlockSpec((None, n_ctx_tok, HEAD_DIM), lambda bi, g, i: (bi, 0, vcol + g)),
            pl.BlockSpec((None, tq, gw), lambda bi, g, i: (bi, i, GB0 // gw + g)),
        ],
        out_specs=pl.BlockSpec((None, tq, gw), lambda bi, g, i: (bi, i, g)),
        out_shape=jax.ShapeDtypeStruct((b, n_lat_tok, B_Q_HEADS * HEAD_DIM), BF16),
        scratch_shapes=[
            pltpu.VMEM((REP * tq, LANES), F32),
            pltpu.VMEM((REP * tq, LANES), F32),
            pltpu.VMEM((REP * tq, HEAD_DIM), F32),
            pltpu.VMEM((2, REP * tq, TK_B), BF16),
            pltpu.VMEM((2, REP * tq, LANES), F32),
        ],
        compiler_params=_params(("parallel", "parallel", "parallel")),
        name="attn_dense",
    )(q_lat, q_lat, q_lat, q_ctx, q_ctx, q_lat)
    y_ctx = pl.pallas_call(
        _attn_b_ctx_kernel,
        grid=(b, B_KV_HEADS),
        in_specs=[
            pl.BlockSpec((None, n_ctx_tok, gw), lambda bi, g: (bi, 0, QB0 // gw + g)),
            pl.BlockSpec((None, n_ctx_tok, HEAD_DIM), lambda bi, g: (bi, 0, kcol + g)),
            pl.BlockSpec((None, n_ctx_tok, HEAD_DIM), lambda bi, g: (bi, 0, vcol + g)),
            pl.BlockSpec((None, n_ctx_tok, gw), lambda bi, g: (bi, 0, GB0 // gw + g)),
        ],
        out_specs=pl.BlockSpec((None, n_ctx_tok, gw), lambda bi, g: (bi, 0, g)),
        out_shape=jax.ShapeDtypeStruct((b, n_ctx_tok, B_Q_HEADS * HEAD_DIM), BF16),
        compiler_params=_params(("parallel", "parallel")),
        name="attn_dense_ctx",
    )(q_ctx, q_ctx, q_ctx, q_ctx)
    return y_lat, y_ctx


def _residual(x, y, mod, w_post):
    return x + mod[2:3] * _rms(y, w_post)


def _attn_out_kernel(ya_ref, yb_ref, wa_ref, wb_ref, x_ref, mod_ref, np_ref, o_ref):
    y = _dot(ya_ref[...], wa_ref[...]) + _dot(yb_ref[...], wb_ref[...])
    o_ref[...] = _residual(x_ref[...], y, mod_ref[...], np_ref[...])


def _attn_out(parts, ya, yb, wa, wb, xs, modl, norm_post):
    b, d = modl.shape[0], modl.shape[-1]

    def call(p):
        return pl.pallas_call(
            _attn_out_kernel,
            grid=(b, p.rows // p.tm),
            in_specs=[
                pl.BlockSpec((None, p.tm, wa.shape[0]), _ROW),
                pl.BlockSpec((None, p.tm, wb.shape[0]), _ROW),
                pl.BlockSpec(wa.shape, _CONST),
                pl.BlockSpec(wb.shape, _CONST),
                pl.BlockSpec((None, p.tm, d), _ROW),
                _mod_spec(p, d),
                pl.BlockSpec((1, d), _CONST),
            ],
            out_specs=pl.BlockSpec((None, p.tm, d), _ROW),
            out_shape=jax.ShapeDtypeStruct((b, p.rows, d), F32),
            compiler_params=_params(("parallel", "parallel")),
            name="attn_out",
        )(ya[p.idx], yb[p.idx], wa, wb, xs[p.idx], modl, norm_post)

    return tuple(call(p) for p in parts)


def _softplus(t):
    return jnp.maximum(t, 0.0) + jnp.log(1.0 + jnp.exp(-jnp.abs(t)))


def _ssm_in_kernel(x_ref, xp_ref, xn_ref, mod_ref, np_ref, wz_ref, wx_ref, wdt_ref, wdtt_ref,
                   cw_ref, cb_ref, dtb_ref, dtbt_ref, z_ref, xbc_ref, dt_ref, dtt_ref):
    i = pl.program_id(1)
    tm = x_ref.shape[0]
    mod = mod_ref[...]
    w_pre = np_ref[...]
    h = _pre_norm(x_ref[...], mod, w_pre)
    hb = h.astype(BF16)
    has_prev = (i > 0).astype(F32)
    has_next = (i < pl.num_programs(1) - 1).astype(F32)
    hp = _pre_norm(xp_ref[...], mod, w_pre) * has_prev
    hn = _pre_norm(xn_ref[...], mod, w_pre) * has_next
    ext = jnp.concatenate([hp, h, hn], axis=0).astype(BF16)

    z_ref[...] = _silu(_dot(hb, wz_ref[...])).astype(BF16)
    dt_ref[...] = _softplus(_dot(hb, wdt_ref[...]) + dtb_ref[...])
    dtt_ref[...] = _softplus(_dot_nt(wdtt_ref[...], hb) + dtbt_ref[...])

    nblk = CONV_COLS
    nt = tm // 8
    sub = lax.broadcasted_iota(jnp.int32, (8, nblk), 0)
    for j in range(wx_ref.shape[1] // nblk):
        cs = slice(j * nblk, (j + 1) * nblk)
        u3 = _dot(ext, wx_ref[:, cs]).reshape(nt + 2, 8, nblk)
        dn = pltpu.roll(u3, 1, 1)
        upw = pltpu.roll(u3, 7, 1)
        u_prev = jnp.where(sub == 0, dn[0:nt], dn[1:nt + 1])
        u_next = jnp.where(sub == 7, upw[2:nt + 2], upw[1:nt + 1])
        conv = (cb_ref[:, cs] + cw_ref[0:1, cs] * u_prev + cw_ref[1:2, cs] * u3[1:nt + 1]
                + cw_ref[2:3, cs] * u_next)
        xbc_ref[:, cs] = _silu(conv).reshape(tm, nblk).astype(BF16)


def _ssm_in(parts, xs, modl, norm_pre, wz, wx, wdt, wdtt, cw, cb, dtb, dtbt):
    b, d = modl.shape[0], modl.shape[-1]
    nh2 = wdt.shape[1]

    def call(p):
        r8 = p.tm // 8
        last8 = p.rows // 8 - 1
        src = xs[p.idx]
        return pl.pallas_call(
            _ssm_in_kernel,
            grid=(b, p.rows // p.tm),
            in_specs=[
                pl.BlockSpec((None, p.tm, d), _ROW),
                pl.BlockSpec((None, 8, d), lambda bi, i: (bi, jnp.maximum(i * r8 - 1, 0), 0)),
                pl.BlockSpec((None, 8, d), lambda bi, i: (bi, jnp.minimum((i + 1) * r8, last8), 0)),
                _mod_spec(p, d),
                pl.BlockSpec((1, d), _CONST),
                pl.BlockSpec(wz.shape, _CONST),
                pl.BlockSpec(wx.shape, _CONST),
                pl.BlockSpec(wdt.shape, _CONST),
                pl.BlockSpec(wdtt.shape, _CONST),
                pl.BlockSpec(cw.shape, _CONST),
                pl.BlockSpec(cb.shape, _CONST),
                pl.BlockSpec(dtb.shape, _CONST),
                pl.BlockSpec(dtbt.shape, _CONST),
            ],
            out_specs=[
                pl.BlockSpec((None, p.tm, wz.shape[1]), _ROW),
                pl.BlockSpec((None, p.tm, wx.shape[1]), _ROW),
                pl.BlockSpec((None, p.tm, nh2), _ROW),
                pl.BlockSpec((None, nh2, p.tm), lambda bi, i: (bi, 0, i)),
            ],
            out_shape=[
                jax.ShapeDtypeStruct((b, p.rows, wz.shape[1]), BF16),
                jax.ShapeDtypeStruct((b, p.rows, wx.shape[1]), BF16),
                jax.ShapeDtypeStruct((b, p.rows, nh2), F32),
                jax.ShapeDtypeStruct((b, nh2, p.rows), F32),
            ],
            compiler_params=_params(("parallel", "parallel")),
            name="ssm_in",
        )(src, src, src, modl, norm_pre, wz, wx, wdt, wdtt, cw, cb, dtb, dtbt)

    lat, cx = (call(p) for p in parts)
    return tuple(zip(lat, cx))


def _split3(t):
    hi = t.astype(BF16)
    r1 = t - hi.astype(F32)
    mid = r1.astype(BF16)
    lo = (r1 - mid.astype(F32)).astype(BF16)
    return hi, mid, lo


def _ones_dot_lhs(tri01, a):
    return _dot(jnp.concatenate([tri01] * 3, axis=1), jnp.concatenate(_split3(a), axis=0))


def _ones_dot_rhs(at, tri01):
    return _dot(jnp.concatenate(_split3(at), axis=1), jnp.concatenate([tri01] * 3, axis=0))


def _ssd_direction(xbc_ref, rs, dt, dtt, a_row, a_col, h_sc, y_ref, reverse, hoff):
    q_len = SSM_CHUNK
    d_inner = SSM_HEADS * SSM_HEAD_DIM
    gw = SSM_REP * SSM_HEAD_DIM
    row = lax.broadcasted_iota(jnp.int32, (q_len, q_len), 0)
    col = lax.broadcasted_iota(jnp.int32, (q_len, q_len), 1)
    lower = row >= col
    upper = row <= col
    mask = upper if reverse else lower
    tri = jnp.where(mask, 1.0, 0.0).astype(BF16)
    tri_t = jnp.where(lower if reverse else upper, 1.0, 0.0).astype(BF16)

    a = dt * a_row
    at = dtt * a_col
    acum = _ones_dot_lhs(tri, a)
    acum_t = _ones_dot_rhs(at, tri_t)
    total_t = jnp.sum(at, axis=1, keepdims=True)
    acum_t = acum_t - jnp.log2(dtt)
    w_t = jnp.exp2(total_t - acum_t)
    etot = jnp.exp2(jnp.sum(a, axis=0, keepdims=True))
    head_of_col = jnp.right_shift(lax.broadcasted_iota(jnp.int32, (q_len, gw), 1),
                                  SSM_HEAD_DIM.bit_length() - 1)

    def group(g):
        b_g = xbc_ref[rs, d_inner + g * D_STATE:d_inner + (g + 1) * D_STATE]
        c_g = xbc_ref[rs, d_inner + (SSM_GROUPS + g) * D_STATE:d_inner + (SSM_GROUPS + g + 1) * D_STATE]
        gsl = slice(g * gw, (g + 1) * gw)
        x_g = xbc_ref[rs, gsl]
        cb = _dot_nt(c_g, b_g).astype(BF16)
        b_t = b_g.astype(F32).T.astype(BF16)
        h_g = h_sc[g]
        y_state = _dot(c_g, h_g.astype(BF16))
        ms, bws, bdx = [], [], []
        etot_row = None
        e_in = None
        for r in range(SSM_REP):
            h = hoff + g * SSM_REP + r
            acb = jnp.broadcast_to(acum[:, h:h + 1], (q_len, q_len))
            decay = jnp.exp2(jnp.where(mask, acb - acum_t[h:h + 1, :], NEG))
            ms.append(cb * decay.astype(BF16))
            bws.append(b_t * jnp.broadcast_to(w_t[h:h + 1, :], b_t.shape).astype(BF16))
            sel = head_of_col == r
            bdx.append(jnp.where(sel, x_g, jnp.zeros_like(x_g)))
            e_q = jnp.exp2(acb)
            e_q = jnp.concatenate([e_q] * (gw // q_len), axis=1)
            e_in = e_q if e_in is None else jnp.where(sel, e_q, e_in)
            e_r = jnp.broadcast_to(etot[:, h:h + 1], (1, gw))
            etot_row = e_r if etot_row is None else jnp.where(head_of_col[0:1] == r, e_r, etot_row)
        bdx = jnp.concatenate(bdx, axis=0)
        lhs = jnp.concatenate([jnp.concatenate(ms, axis=1), jnp.concatenate(bws, axis=1)], axis=0)
        res = _dot(lhs, bdx)
        y = res[0:q_len] + y_state * e_in
        y_ref[rs, gsl] = y.astype(BF16)
        h_sc[g] = h_g * etot_row + res[q_len:]

    return group


def _ssd_kernel(*refs, n_sub, has_init, emit_state):
    xf_ref, xb_ref, dtf_ref, dtb_ref, dttf_ref, dttb_ref, alog_ref, alogt_ref = refs[:8]
    refs = refs[8:]
    if has_init:
        hf0_ref, hb0_ref = refs[:2]
        refs = refs[2:]
    yf_ref, yb_ref = refs[:2]
    refs = refs[2:]
    if emit_state:
        hf_out_ref, hb_out_ref = refs[:2]
        refs = refs[2:]
    hf_sc, hb_sc = refs

    @pl.when(pl.program_id(1) == 0)
    def _():
        if has_init:
            hf_sc[...] = hf0_ref[...]
            hb_sc[...] = hb0_ref[...]
        else:
            hf_sc[...] = jnp.zeros(hf_sc.shape, F32)
            hb_sc[...] = jnp.zeros(hb_sc.shape, F32)

    a_row = -jnp.exp(alog_ref[...]) * LOG2E
    a_col = -jnp.exp(alogt_ref[...]) * LOG2E
    for sub in range(n_sub):
        rs = slice(sub * SSM_CHUNK, (sub + 1) * SSM_CHUNK)
        fwd = _ssd_direction(xf_ref, rs, dtf_ref[rs, :], dttf_ref[:, rs], a_row, a_col, hf_sc, yf_ref, False, 0)
        for g in range(SSM_GROUPS):
            fwd(g)
    for sub in reversed(range(n_sub)):
        rs = slice(sub * SSM_CHUNK, (sub + 1) * SSM_CHUNK)
        bwd = _ssd_direction(xb_ref, rs, dtb_ref[rs, :], dttb_ref[:, rs], a_row, a_col, hb_sc, yb_ref, True,
                             SSM_HEADS)
        for g in range(SSM_GROUPS):
            bwd(g)

    if emit_state:
        hf_out_ref[...] = hf_sc[...]
        hb_out_ref[...] = hb_sc[...]


def _ssd_stream(xbc, dt, dtt, alog, alogt, init, emit_state):
    b, rows, cw = xbc.shape
    n_sub = min(SSD_SUB, rows // SSM_CHUNK)
    q_len = n_sub * SSM_CHUNK
    assert rows % q_len == 0
    n_c = rows // q_len
    d_inner = SSM_HEADS * SSM_HEAD_DIM
    nh2 = dt.shape[-1]
    state_shape = (SSM_GROUPS, D_STATE, SSM_REP * SSM_HEAD_DIM)
    asc = lambda bi, j: (bi, j, 0)
    desc = lambda bi, j: (bi, n_c - 1 - j, 0)
    state_spec = pl.BlockSpec((None,) + state_shape, lambda bi, j: (bi, 0, 0, 0))
    in_specs = [
        pl.BlockSpec((None, q_len, cw), asc),
        pl.BlockSpec((None, q_len, cw), desc),
        pl.BlockSpec((None, q_len, nh2), asc),
        pl.BlockSpec((None, q_len, nh2), desc),
        pl.BlockSpec((None, nh2, q_len), lambda bi, j: (bi, 0, j)),
        pl.BlockSpec((None, nh2, q_len), lambda bi, j: (bi, 0, n_c - 1 - j)),
        pl.BlockSpec(alog.shape, _CONST),
        pl.BlockSpec(alogt.shape, _CONST),
    ]
    args = [xbc, xbc, dt, dt, dtt, dtt, alog, alogt]
    if init is not None:
        in_specs += [state_spec, state_spec]
        args += list(init)
    out_specs = [pl.BlockSpec((None, q_len, d_inner), asc), pl.BlockSpec((None, q_len, d_inner), desc)]
    out_shape = [jax.ShapeDtypeStruct((b, rows, d_inner), BF16)] * 2
    if emit_state:
        out_specs += [state_spec, state_spec]
        out_shape += [jax.ShapeDtypeStruct((b,) + state_shape, F32)] * 2
    return pl.pallas_call(
        functools.partial(_ssd_kernel, n_sub=n_sub, has_init=init is not None, emit_state=emit_state),
        grid=(b, n_c),
        in_specs=in_specs,
        out_specs=out_specs,
        out_shape=out_shape,
        scratch_shapes=[pltpu.VMEM(state_shape, F32)] * 2,
        compiler_params=_params(("parallel", "arbitrary")),
        name="ssd_scan",
    )(*args)


def _ssd(xbc, dt, dtt, alog, alogt):
    yf_c, yb_c, hf, hb = _ssd_stream(xbc[1], dt[1], dtt[1], alog, alogt, None, True)
    yf_l, yb_l = _ssd_stream(xbc[0], dt[0], dtt[0], alog, alogt, (hf, hb), False)
    return (yf_l, yf_c), (yb_l, yb_c)


def _ssm_out_kernel(yf_ref, yb_ref, xs_ref, z_ref, dsk_ref, nw_ref, w_ref, x_ref, mod_ref, np_ref, o_ref):
    y = (yf_ref[...] + yb_ref[...]).astype(F32) + dsk_ref[...] * xs_ref[...].astype(F32)
    gated = y * z_ref[...].astype(F32)
    gsz = gated.shape[1] // SSM_GROUPS
    parts = []
    for g in range(SSM_GROUPS):
        t = gated[:, g * gsz:(g + 1) * gsz]
        parts.append(t * lax.rsqrt(jnp.mean(t * t, axis=-1, keepdims=True) + EPS))
    gn = (jnp.concatenate(parts, axis=1) * nw_ref[...]).astype(BF16)
    o_ref[...] = _residual(x_ref[...], _dot(gn, w_ref[...]), mod_ref[...], np_ref[...])


def _ssm_out(parts, yf, yb, xbc, z, dsk, nw, w, xs, modl, norm_post):
    b, d = modl.shape[0], modl.shape[-1]
    di = w.shape[0]

    def call(p):
        wide = pl.BlockSpec((None, p.tm, di), _ROW)
        return pl.pallas_call(
            _ssm_out_kernel,
            grid=(b, p.rows // p.tm),
            in_specs=[
                wide, wide, wide, wide,
                pl.BlockSpec((1, di), _CONST),
                pl.BlockSpec((1, di), _CONST),
                pl.BlockSpec(w.shape, _CONST),
                pl.BlockSpec((None, p.tm, d), _ROW),
                _mod_spec(p, d),
                pl.BlockSpec((1, d), _CONST),
            ],
            out_specs=pl.BlockSpec((None, p.tm, d), _ROW),
            out_shape=jax.ShapeDtypeStruct((b, p.rows, d), F32),
            compiler_params=_params(("parallel", "parallel")),
            name="ssm_out",
        )(yf[p.idx], yb[p.idx], xbc[p.idx], z[p.idx], dsk, nw, w, xs[p.idx], modl, norm_post)

    return tuple(call(p) for p in parts)


def _rope_tables(n_lat_tok, n_ctx_tok):
    t = np.arange(n_lat_tok)
    n_freq = HEAD_DIM // 4
    inv = 1.0 / (ROPE_THETA ** (jnp.arange(n_freq, dtype=F32) / n_freq))
    rowp = jnp.asarray(t // GRID_W, F32)
    colp = jnp.asarray(t % GRID_W, F32)
    ang = jnp.concatenate([rowp[:, None] * inv, colp[:, None] * inv], axis=-1)
    cos, sin = jnp.cos(ang), jnp.sin(ang)
    lat = (jnp.concatenate([cos, cos], axis=-1), jnp.concatenate([-sin, sin], axis=-1))
    cx = (jnp.ones((n_ctx_tok, HEAD_DIM), F32), jnp.zeros((n_ctx_tok, HEAD_DIM), F32))
    return lat, cx


_DEINT = np.concatenate([np.arange(0, HEAD_DIM, 2), np.arange(1, HEAD_DIM, 2)])


def _attn_weight(w):
    d = w.shape[0]
    w = w.astype(BF16)

    def deint(seg):
        n = seg.shape[1] // HEAD_DIM
        return seg.reshape(d, n, HEAD_DIM // 2, 2).transpose(0, 1, 3, 2).reshape(d, n * HEAD_DIM)

    qa, ka, va, ga, qb, kb, vb, gb = jnp.split(w, [1024, 1280, 1536, 2560, 3584, 3840, 4096], axis=1)
    return jnp.concatenate([deint(qa), ga, deint(qb), gb, deint(ka), va, deint(kb), vb], axis=1)


def kernel(x, c, ctx, c_ctx, w_ada, b_ada, norm_pre, norm_post, attn_w_in, attn_w_out, attn_sink,
           attn_q_norm, attn_k_norm, ssm_w_in, ssm_conv_w, ssm_conv_b, ssm_dt_bias, ssm_a_log, ssm_d,
           ssm_norm, ssm_w_out):
    bsz, n_lat_tok, d = x.shape
    n_ctx_tok = ctx.shape[1]
    depth = w_ada.shape[0]
    assert n_lat_tok % TM_OUT == 0 and n_lat_tok % TM == 0 and n_ctx_tok % LANES == 0 and bsz <= 7
    d_inner = SSM_HEADS * SSM_HEAD_DIM
    bc_w = 2 * SSM_GROUPS * D_STATE

    xs = (x, ctx)
    in_parts = _parts(n_lat_tok, n_ctx_tok, TM)
    out_parts = _parts(n_lat_tok, n_ctx_tok, TM_OUT)
    cc = jnp.zeros((8, d), F32).at[:bsz].set(c).at[bsz].set(c_ctx)
    mod = _modulation(cc, w_ada, b_ada)
    mod = mod.reshape(depth, 8, 3, d)
    tables = _rope_tables(n_lat_tok, n_ctx_tok)

    for l in range(depth):
        l_out_parts = out_parts[:1] if l == depth - 1 else out_parts
        modl = jnp.stack([mod[l, :bsz], jnp.broadcast_to(mod[l, bsz], (bsz, 3, d))], axis=1)
        npre = norm_pre[l].reshape(1, d)
        npost = norm_post[l].reshape(1, d)
        i = l // 2
        if l % 2 == 0:
            w = _attn_weight(attn_w_in[i])
            qn = attn_q_norm[i][_DEINT].reshape(1, HEAD_DIM)
            kn = attn_k_norm[i][_DEINT].reshape(1, HEAD_DIM)
            qkv = _attn_in(in_parts, xs, modl, npre, w, tables, qn, kn)
            sink_b = jnp.broadcast_to(attn_sink[i][:, None], (A_Q_HEADS, HEAD_DIM))
            ya = _attn_a(qkv, sink_b)
            yb = _attn_b(qkv)
            wo = attn_w_out[i].astype(BF16)
            aq = A_Q_HEADS * HEAD_DIM
            xs = _attn_out(l_out_parts, ya, yb, wo[:aq], wo[aq:], xs, modl, npost)
        else:
            w = ssm_w_in[i]
            wz = w[:, :d_inner].astype(BF16)
            wx = w[:, d_inner:2 * d_inner + bc_w].astype(BF16)
            wdt = jnp.pad(w[:, 2 * d_inner + bc_w:], ((0, 0), (0, HPAD - 2 * SSM_HEADS))).astype(BF16)
            dtb = jnp.pad(ssm_dt_bias[i].reshape(1, -1), ((0, 0), (0, HPAD - 2 * SSM_HEADS)))
            z, xbc, dt, dtt = _ssm_in(in_parts, xs, modl, npre, wz, wx, wdt, wdt.T, ssm_conv_w[i],
                                      ssm_conv_b[i].reshape(1, -1), dtb, dtb.reshape(-1, 1))
            alog = jnp.pad(ssm_a_log[i].reshape(1, -1), ((0, 0), (0, HPAD - 2 * SSM_HEADS)))
            yf, ybk = _ssd(xbc, dt, dtt, alog, alog.reshape(-1, 1))
            dsk = jnp.repeat(ssm_d[i], SSM_HEAD_DIM).reshape(1, d_inner)
            xs = _ssm_out(l_out_parts, yf, ybk, xbc, z, dsk, ssm_norm[i].reshape(1, d_inner),
                          ssm_w_out[i].astype(BF16), xs, modl, npost)
    return xs[0]
```
